```python
import math
import jax
import jax.numpy as jnp
from jax import lax
import numpy as np

D_MODEL = 1024
BATCH = 8
SEQ = 2048
DEPTH = 2
DEC_BATCH = 32
DEC_SEQ = 4
PAST_LEN = 8192
PAGE_SIZE = 128

MIX_W = D_MODEL
GDN_H = 4
GDN_DK = MIX_W // (2 * GDN_H)
GDN_DV = GDN_DK
GDN_W = GDN_H * GDN_DV
GDN_CONV = 4
GDN_CHUNK = 64
DIFF_H = 4
DIFF_DH = MIX_W // (4 * DIFF_H)
DIFF_DV = 2 * DIFF_DH
DIFF_W = DIFF_H * DIFF_DV
IN_W = 4 * GDN_W + 2 * GDN_H + 3 * DIFF_W
SPLITS = (3 * GDN_W, 4 * GDN_W, 4 * GDN_W + GDN_H, 4 * GDN_W + 2 * GDN_H,
          4 * GDN_W + 2 * GDN_H + DIFF_W, 4 * GDN_W + 2 * GDN_H + 2 * DIFF_W)
ROPE_THETA = 10000.0
ATTN_BLOCK = 128
MEM_LEN = 256
CA_H = 4
CA_DH = D_MODEL // 8
CA_W = CA_H * CA_DH
D_FF = ((8 * D_MODEL // 3 + 127) // 128) * 128
FFN_CONV = 3
EPS = 1e-6

kernel_name = 'hybrid_gdn_diffattn_memxattn_convffn_step'


def rms_norm(x, gain):
    x32 = x.astype(jnp.float32)
    y = x32 * lax.rsqrt(jnp.mean(x32 * x32, axis=-1, keepdims=True) + EPS)
    return (y * gain.astype(jnp.float32)).astype(x.dtype)


def l2_norm(x):
    return x * lax.rsqrt(jnp.sum(x * x, axis=-1, keepdims=True) + EPS)


def rotary(x, pos):
    half = x.shape[-1] // 2
    inv = ROPE_THETA ** (-jnp.arange(half, dtype=jnp.float32) / half)
    ang = pos.astype(jnp.float32)[:, None] * inv[None, :]
    cos = jnp.cos(ang)[None, :, None, :]
    sin = jnp.sin(ang)[None, :, None, :]
    x32 = x.astype(jnp.float32)
    x1, x2 = x32[..., :half], x32[..., half:]
    return jnp.concatenate([x1 * cos - x2 * sin, x2 * cos + x1 * sin], axis=-1).astype(x.dtype)


def causal_dwconv(x, buf, w):
    width, L = w.shape[0], x.shape[1]
    xp = jnp.concatenate([buf.astype(x.dtype), x], axis=1)
    w = w.astype(x.dtype)
    y = xp[:, 0:L] * w[0]
    for j in range(1, width):
        y = y + xp[:, j:j + L] * w[j]
    return y, xp[:, L:]


def gated_delta_rule(q, k, v, g, beta, s0):
    B, L, H, dk = q.shape
    dv = v.shape[-1]
    C = math.gcd(L, GDN_CHUNK)
    N = L // C

    def chunks(t):
        t = t.reshape((B, N, C, H) + t.shape[3:])
        return jnp.moveaxis(t, (1, 3), (0, 2))

    qc = chunks(q) * (dk ** -0.5)
    kc = chunks(k)
    vc = chunks(v)
    bc = chunks(beta)
    gc = jnp.cumsum(chunks(g), axis=-1)
    idx = jnp.arange(C)
    lower = idx[:, None] >= idx[None, :]
    strict = idx[:, None] > idx[None, :]
    decay = jnp.exp(jnp.where(lower, gc[..., :, None] - gc[..., None, :], -jnp.inf))
    kb = kc * bc[..., None]
    m = jnp.where(strict, jnp.einsum('nbhid,nbhjd->nbhij', kb, kc) * decay, 0.0)
    eye = jnp.eye(C, dtype=jnp.float32)
    t = lax.linalg.triangular_solve(eye + m, jnp.broadcast_to(eye, m.shape),
                                    left_side=True, lower=True, unit_diagonal=True)
    u = t @ (vc * bc[..., None])
    w = t @ (kb * jnp.exp(gc)[..., None])
    aqk = jnp.einsum('nbhid,nbhjd->nbhij', qc, kc) * decay
    qg = qc * jnp.exp(gc)[..., None]
    kg = kc * jnp.exp(gc[..., -1:] - gc)[..., None]
    gl = jnp.exp(gc[..., -1])

    def step(S, xs):
        u_i, w_i, aqk_i, qg_i, kg_i, gl_i = xs
        v_new = u_i - w_i @ S
        o = qg_i @ S + aqk_i @ v_new
        S = S * gl_i[..., None, None] + jnp.swapaxes(kg_i, -1, -2) @ v_new
        return S, o

    S, o = lax.scan(step, s0, (u, w, aqk, qg, kg, gl))
    o = jnp.moveaxis(o, (0, 2), (1, 3)).reshape(B, L, H, dv)
    return o, S


def diff_attention(q, k, v, q_pos, k_pos, lam):
    B, Lq, H2, dh = q.shape
    H, dv = v.shape[2], v.shape[3]
    Lk = k.shape[1]
    qb = ATTN_BLOCK if Lq % ATTN_BLOCK == 0 else Lq
    nb = Lq // qb
    q_blocks = jnp.moveaxis(q.reshape(B, nb, qb, H2, dh), 1, 0)
    pos_blocks = q_pos.reshape(nb, qb)
    scale = dh ** -0.5

    def block(args):
        qi, pi = args
        s = jnp.einsum('bqhd,bkhd->bhqk', qi, k).astype(jnp.float32) * scale
        s = jnp.where(k_pos[None, None, None, :] <= pi[None, None, :, None], s, -jnp.inf)
        p = jax.nn.softmax(s, axis=-1).reshape(B, H, 2, qb, Lk)
        a = p[:, :, 0] - lam * p[:, :, 1]
        return jnp.einsum('bhqk,bkhe->bqhe', a.astype(v.dtype), v)

    out = lax.map(block, (q_blocks, pos_blocks))
    return jnp.moveaxis(out, 0, 1).reshape(B, Lq, H, dv)


def memory_attention(q, mk, mv):
    s = jnp.einsum('bqhd,bkhd->bhqk', q, mk).astype(jnp.float32) * (q.shape[-1] ** -0.5)
    p = jax.nn.softmax(s, axis=-1)
    return jnp.einsum('bhqk,bkhd->bqhd', p.astype(mv.dtype), mv)


def memory_kv(mem, lp):
    B, M, _ = mem.shape
    mn = rms_norm(mem, lp['norm_mem'])
    mk = rms_norm((mn @ lp['w_ck']).reshape(B, M, CA_H, CA_DH), lp['knorm_cross'])
    mv = (mn @ lp['w_cv']).reshape(B, M, CA_H, CA_DH)
    return mk, mv


def trunk_layer(x, lp, layer, pos, conv_buf, s0, past_k, past_v, mem_k, mem_v, ffn_buf):
    B, L, _ = x.shape
    f32 = jnp.float32
    h = rms_norm(x, lp['norm_mix'])
    z = h @ lp['w_in']
    qkv_raw, gate, b_raw, a_raw, dq, dk, dv = jnp.split(z, SPLITS, axis=-1)
    qkv, conv_new = causal_dwconv(qkv_raw, conv_buf, lp['conv_qkv'])
    qkv = jax.nn.silu(qkv.astype(f32))
    gq = l2_norm(qkv[..., :GDN_W].reshape(B, L, GDN_H, GDN_DK))
    gk = l2_norm(qkv[..., GDN_W:2 * GDN_W].reshape(B, L, GDN_H, GDN_DK))
    gv = qkv[..., 2 * GDN_W:].reshape(B, L, GDN_H, GDN_DV)
    beta = jax.nn.sigmoid(b_raw.astype(f32))
    g = -jnp.exp(lp['a_log'].astype(f32)) * jax.nn.softplus(a_raw.astype(f32) + lp['dt_bias'].astype(f32))
    o_gdn, s_new = gated_delta_rule(gq, gk, gv, g, beta, s0.astype(f32))
    o_gdn = rms_norm(o_gdn, lp['gdn_norm']) * jax.nn.silu(gate.astype(f32).reshape(B, L, GDN_H, GDN_DV))
    o_gdn = o_gdn.reshape(B, L, GDN_W).astype(x.dtype)
    qd = rotary(rms_norm(dq.reshape(B, L, 2 * DIFF_H, DIFF_DH), lp['qnorm_diff']), pos)
    kd = rotary(rms_norm(dk.reshape(B, L, 2 * DIFF_H, DIFF_DH), lp['knorm_diff']), pos)
    vd = dv.reshape(B, L, DIFF_H, DIFF_DV)
    if past_k is None:
        keys, vals, k_pos = kd, vd, pos
    else:
        keys = jnp.concatenate([past_k.astype(x.dtype), kd], axis=1)
        vals = jnp.concatenate([past_v.astype(x.dtype), vd], axis=1)
        k_pos = jnp.concatenate([jnp.arange(past_k.shape[1], dtype=pos.dtype), pos])
    lam_init = 0.8 - 0.6 * math.exp(-0.3 * layer)
    lam = (jnp.exp(jnp.sum(lp['lam_q1'].astype(f32) * lp['lam_k1'].astype(f32)))
           - jnp.exp(jnp.sum(lp['lam_q2'].astype(f32) * lp['lam_k2'].astype(f32))) + lam_init)
    o_diff = diff_attention(qd, keys, vals, pos, k_pos, lam)
    o_diff = (rms_norm(o_diff, lp['diff_norm']) * (1.0 - lam_init)).reshape(B, L, DIFF_W)
    x = x + jnp.concatenate([o_gdn, o_diff.astype(x.dtype)], axis=-1) @ lp['w_out']
    hc = rms_norm(x, lp['norm_cross'])
    qc = rms_norm((hc @ lp['w_cq']).reshape(B, L, CA_H, CA_DH), lp['qnorm_cross'])
    oc = memory_attention(qc, mem_k.astype(x.dtype), mem_v.astype(x.dtype))
    x = x + oc.reshape(B, L, CA_W) @ lp['w_co']
    hf = rms_norm(x, lp['norm_ffn'])
    gt, ffn_new = causal_dwconv(hf @ lp['w_gate'], ffn_buf, lp['conv_ffn'])
    x = x + (jax.nn.silu(gt) * (hf @ lp['w_up'])) @ lp['w_down']
    return x, kd, vd, s_new, conv_new, ffn_new


def setup_inputs(seed: int = 0) -> dict:
    key = jax.random.key(seed)
    ks = iter(jax.random.split(key, 64))
    f32 = jnp.float32

    def nrm(shape, scale=1.0):
        return jax.random.normal(next(ks), shape, f32) * scale

    def gain(n):
        return 1.0 + 0.02 * nrm((DEPTH, n))

    n_pages = PAST_LEN // PAGE_SIZE
    n_used = DEC_BATCH * n_pages
    n_pool = n_used + max(1, n_used // 4)
    page_table = jax.random.permutation(next(ks), n_pool)[:n_used].astype(jnp.int32).reshape(DEC_BATCH, n_pages)
    dt = jnp.exp(jax.random.uniform(next(ks), (DEPTH, GDN_H), f32, math.log(1e-3), math.log(1e-1)))
    dt_bias = dt + jnp.log(-jnp.expm1(-dt))
    a_log = jnp.log(jax.random.uniform(next(ks), (DEPTH, GDN_H), f32, 1.0, 16.0))
    return {
        'x_prompt': nrm((BATCH, SEQ, D_MODEL)),
        'x_sample': nrm((DEC_BATCH, DEC_SEQ, D_MODEL)),
        'mem_prompt': nrm((BATCH, MEM_LEN, D_MODEL)),
        'cache_k': nrm((DEPTH, n_pool, PAGE_SIZE, 2 * DIFF_H, DIFF_DH)),
        'cache_v': nrm((DEPTH, n_pool, PAGE_SIZE, DIFF_H, DIFF_DV)),
        'page_table': page_table,
        'state_gdn': nrm((DEPTH, DEC_BATCH, GDN_H, GDN_DK, GDN_DV), 0.3),
        'state_gdn_conv': nrm((DEPTH, DEC_BATCH, GDN_CONV - 1, 3 * GDN_W)),
        'cache_mem_k': nrm((DEPTH, DEC_BATCH, MEM_LEN, CA_H, CA_DH)),
        'cache_mem_v': nrm((DEPTH, DEC_BATCH, MEM_LEN, CA_H, CA_DH)),
        'state_ffn_conv': nrm((DEPTH, DEC_BATCH, FFN_CONV - 1, D_FF)),
        'norm_mix': gain(D_MODEL),
        'w_in': nrm((DEPTH, D_MODEL, IN_W), D_MODEL ** -0.5),
        'conv_qkv': nrm((DEPTH, GDN_CONV, 3 * GDN_W), GDN_CONV ** -0.5),
        'a_log': a_log,
        'dt_bias': dt_bias,
        'gdn_norm': gain(GDN_DV),
        'qnorm_diff': gain(DIFF_DH),
        'knorm_diff': gain(DIFF_DH),
        'lam_q1': nrm((DEPTH, DIFF_DH), 0.1),
        'lam_k1': nrm((DEPTH, DIFF_DH), 0.1),
        'lam_q2': nrm((DEPTH, DIFF_DH), 0.1),
        'lam_k2': nrm((DEPTH, DIFF_DH), 0.1),
        'diff_norm': gain(DIFF_DV),
        'w_out': nrm((DEPTH, MIX_W, D_MODEL), MIX_W ** -0.5),
        'norm_cross': gain(D_MODEL),
        'norm_mem': gain(D_MODEL),
        'w_cq': nrm((DEPTH, D_MODEL, CA_W), D_MODEL ** -0.5),
        'w_ck': nrm((DEPTH, D_MODEL, CA_W), D_MODEL ** -0.5),
        'w_cv': nrm((DEPTH, D_MODEL, CA_W), D_MODEL ** -0.5),
        'qnorm_cross': gain(CA_DH),
        'knorm_cross': gain(CA_DH),
        'w_co': nrm((DEPTH, CA_W, D_MODEL), CA_W ** -0.5),
        'norm_ffn': gain(D_MODEL),
        'w_gate': nrm((DEPTH, D_MODEL, D_FF), D_MODEL ** -0.5),
        'w_up': nrm((DEPTH, D_MODEL, D_FF), D_MODEL ** -0.5),
        'conv_ffn': nrm((DEPTH, FFN_CONV, D_FF), FFN_CONV ** -0.5),
        'w_down': nrm((DEPTH, D_FF, D_MODEL), D_FF ** -0.5),
    }


def reference(x_prompt, x_sample, mem_prompt, cache_k, cache_v, page_table, state_gdn, state_gdn_conv,
              cache_mem_k, cache_mem_v, state_ffn_conv, norm_mix, w_in, conv_qkv, a_log, dt_bias, gdn_norm,
              qnorm_diff, knorm_diff, lam_q1, lam_k1, lam_q2, lam_k2, diff_norm, w_out, norm_cross, norm_mem,
              w_cq, w_ck, w_cv, qnorm_cross, knorm_cross, w_co, norm_ffn, w_gate, w_up, conv_ffn, w_down):
    params = {
        'norm_mix': norm_mix, 'w_in': w_in, 'conv_qkv': conv_qkv, 'a_log': a_log, 'dt_bias': dt_bias,
        'gdn_norm': gdn_norm, 'qnorm_diff': qnorm_diff, 'knorm_diff': knorm_diff, 'lam_q1': lam_q1,
        'lam_k1': lam_k1, 'lam_q2': lam_q2, 'lam_k2': lam_k2, 'diff_norm': diff_norm, 'w_out': w_out,
        'norm_cross': norm_cross, 'norm_mem': norm_mem, 'w_cq': w_cq, 'w_ck': w_ck, 'w_cv': w_cv,
        'qnorm_cross': qnorm_cross, 'knorm_cross': knorm_cross, 'w_co': w_co, 'norm_ffn': norm_ffn,
        'w_gate': w_gate, 'w_up': w_up, 'conv_ffn': conv_ffn, 'w_down': w_down,
    }
    Bp, Bs = x_prompt.shape[0], x_sample.shape[0]
    past_len = page_table.shape[1] * PAGE_SIZE
    p_pos = jnp.arange(x_prompt.shape[1], dtype=jnp.int32)
    s_pos = past_len + jnp.arange(x_sample.shape[1], dtype=jnp.int32)
    xp, xs = x_prompt, x_sample
    p_out, s_out = [], []
    for l in range(DEPTH):
        lp = {name: arr[l] for name, arr in params.items()}
        mk_p, mv_p = memory_kv(mem_prompt, lp)
        xp, kp, vp, sp, cp, fp = trunk_layer(
            xp, lp, l, p_pos,
            jnp.zeros((Bp, GDN_CONV - 1, 3 * GDN_W), xp.dtype),
            jnp.zeros((Bp, GDN_H, GDN_DK, GDN_DV), jnp.float32),
            None, None, mk_p, mv_p,
            jnp.zeros((Bp, FFN_CONV - 1, D_FF), xp.dtype))
        p_out.append((kp, vp, sp, cp, mk_p, mv_p, fp))
        past_k = cache_k[l][page_table].reshape(Bs, past_len, 2 * DIFF_H, DIFF_DH)
        past_v = cache_v[l][page_table].reshape(Bs, past_len, DIFF_H, DIFF_DV)
        xs, ks_, vs_, ss, cs, fs = trunk_layer(
            xs, lp, l, s_pos, state_gdn_conv[l], state_gdn[l], past_k, past_v,
            cache_mem_k[l], cache_mem_v[l], state_ffn_conv[l])
        s_out.append((ks_, vs_, ss, cs, fs))
    pk, pv, pg, pc, pmk, pmv, pf = [jnp.stack(t, axis=0) for t in zip(*p_out)]
    sk, sv, sg, sc, sf = [jnp.stack(t, axis=0) for t in zip(*s_out)]
    return (xp, xs, pk, pv, pg, pc, pmk, pmv, pf, sk, sv, sg, sc, sf)
```

```python
import functools
import math

import jax
import jax.numpy as jnp
from jax import lax
from jax.experimental import pallas as pl
from jax.experimental.pallas import tpu as pltpu

F32 = jnp.float32
BF16 = jnp.bfloat16

EPS = 1e-6
ROPE_THETA = 10000.0
GDN_H = 4
GDN_D = 128
GDN_W = GDN_H * GDN_D
GDN_CONV = 4
GDN_CHUNK = 64
DIFF_H = 4
DIFF_DH = 64
DIFF_DV = 128
DIFF_W = DIFF_H * DIFF_DV
CA_H = 4
CA_DH = 128
CA_W = CA_H * CA_DH
FFN_CONV = 3
PAGE = 128
LANE = 128
VMEM_LIMIT = 56 * 1024 * 1024
NOT_SAME_CHUNK = 99.0


def _cparams(*sem):
    return pltpu.CompilerParams(dimension_semantics=sem, vmem_limit_bytes=VMEM_LIMIT)


def _dot(a, b):
    return jnp.dot(a, b, preferred_element_type=F32)


def _dot_nt(a, b):
    return lax.dot_general(a, b, (((1,), (1,)), ((), ())), preferred_element_type=F32)


def _dot_tn(a, b):
    return lax.dot_general(a, b, (((0,), (0,)), ((), ())), preferred_element_type=F32)


def _sigmoid(x):
    return 1.0 / (1.0 + jnp.exp(-x))


def _silu(x):
    return x * _sigmoid(x)


def _softplus(x):
    return jnp.maximum(x, 0.0) + jnp.log1p(jnp.exp(-jnp.abs(x)))


def _rms(x, gain):
    return x * lax.rsqrt(jnp.mean(x * x, axis=-1, keepdims=True) + EPS) * gain


def _const_spec(shape):
    nd = len(shape)
    return pl.BlockSpec(shape, lambda *_: (0,) * nd)


def _weight_spec(shape):
    nd = len(shape)
    return pl.BlockSpec(shape, lambda *_: (0,) * nd, pipeline_mode=pl.Buffered(1))


def _inproj_kernel(x_ref, gain_ref, w_ref, qg_ref, kg_ref, cos_ref, sa_ref, sb_ref, gmat_ref,
                   qkv_ref, gate_ref, ba_ref, q_ref, k_ref, v_ref):
    h = _rms(x_ref[...], gain_ref[...]).astype(BF16)
    c_qkv, c_gate = 3 * GDN_W, 4 * GDN_W
    c_q, c_k, c_v = c_gate + DIFF_W, c_gate + 2 * DIFF_W, c_gate + 3 * DIFF_W

    def proj(a, b):
        return _dot(h, w_ref[:, a:b])

    qkv_ref[...] = proj(0, c_qkv)
    gate_ref[...] = proj(c_qkv, c_gate).astype(gate_ref.dtype)
    v_ref[...] = proj(c_k, c_v)
    ba_ref[...] = proj(c_v, c_v + LANE)
    cos, sa, sb = cos_ref[...], sa_ref[...], sb_ref[...]
    half = DIFF_DH // 2

    def norm_rot(z, g):
        ms = _dot((z * z).astype(BF16), gmat_ref[...])
        y = z * lax.rsqrt(ms + EPS) * g
        return y * cos + pltpu.roll(y, DIFF_W - half, 1) * sa + pltpu.roll(y, half, 1) * sb

    q_ref[...] = (norm_rot(proj(c_gate, c_q), qg_ref[...]) * (DIFF_DH ** -0.5)).astype(q_ref.dtype)
    k_ref[...] = norm_rot(proj(c_q, c_k), kg_ref[...])


def _inproj(x, gain, w, qg, kg, cos, sa, sb, gmat, *, tm, q_dtype):
    m, d = x.shape
    npos = cos.shape[0] // tm
    nw = w.shape[1]
    row = lambda i: (i, 0)
    pos = lambda i: (i % npos, 0)
    return pl.pallas_call(
        _inproj_kernel,
        grid=(m // tm,),
        in_specs=[pl.BlockSpec((tm, d), row), _const_spec((1, d)), _weight_spec((d, nw)),
                  _const_spec((1, DIFF_W)), _const_spec((1, DIFF_W)),
                  pl.BlockSpec((tm, DIFF_W), pos), pl.BlockSpec((tm, DIFF_W), pos),
                  pl.BlockSpec((tm, DIFF_W), pos), _const_spec((DIFF_W, DIFF_W))],
        out_specs=[pl.BlockSpec((tm, 3 * GDN_W), row), pl.BlockSpec((tm, GDN_W), row),
                   pl.BlockSpec((tm, LANE), row), pl.BlockSpec((tm, DIFF_W), row),
                   pl.BlockSpec((tm, DIFF_W), row), pl.BlockSpec((tm, DIFF_W), row)],
        out_shape=[jax.ShapeDtypeStruct((m, 3 * GDN_W), F32), jax.ShapeDtypeStruct((m, GDN_W), BF16),
                   jax.ShapeDtypeStruct((m, LANE), F32), jax.ShapeDtypeStruct((m, DIFF_W), q_dtype),
                   jax.ShapeDtypeStruct((m, DIFF_W), F32), jax.ShapeDtypeStruct((m, DIFF_W), F32)],
        compiler_params=_cparams("parallel"),
        name="inproj",
    )(x, gain, w, qg, kg, cos, sa, sb, gmat)


def _gdn_kernel(x_ref, gate_ref, ba_ref, cw_ref, alog_ref, dtb_ref, gn_ref, s0_ref, cb_ref, tri_ref, lv_ref,
                o_ref, sout_ref, xbuf, s_scr, *pad_scr, lt, n_valid, chunk):
    j = pl.program_id(1)
    padded = n_valid < lt
    tail = GDN_CONV - 1

    @pl.when(j == 0)
    def _():
        s_scr[...] = s0_ref[0]
        xbuf[0:8, :] = jnp.zeros((8, 3 * GDN_W), F32)
        xbuf[8 - tail:8, :] = cb_ref[0]

    @pl.when(j > 0)
    def _():
        xbuf[0:8, :] = xbuf[lt:lt + 8, :]

    if padded:
        gate_scr, ba_scr = pad_scr
        xbuf[8:8 + lt, :] = jnp.zeros((lt, 3 * GDN_W), F32)
        xbuf[8:8 + n_valid, :] = x_ref[0]
        gate_scr[...] = jnp.zeros(gate_scr.shape, F32)
        gate_scr[0:n_valid, :] = gate_ref[0].astype(F32)
        ba_scr[...] = jnp.zeros(ba_scr.shape, F32)
        ba_scr[0:n_valid, :] = ba_ref[0]
        gate_src, ba = gate_scr, ba_scr[...]
    else:
        xbuf[8:8 + lt, :] = x_ref[...]
        gate_src, ba = gate_ref, ba_ref[...]

    def conv_act(c0):
        w = cw_ref[:, c0:c0 + GDN_D]
        y = xbuf[8 - tail:8 - tail + lt, c0:c0 + GDN_D] * w[0:1]
        for t in range(1, GDN_CONV):
            y = y + xbuf[8 - tail + t:8 - tail + t + lt, c0:c0 + GDN_D] * w[t:t + 1]
        return _silu(y)

    def l2n(z):
        return z * lax.rsqrt(jnp.sum(z * z, axis=-1, keepdims=True) + EPS)

    beta_all = _sigmoid(ba)
    g_all = -jnp.exp(alog_ref[...]) * _softplus(ba + dtb_ref[...])
    if padded:
        valid = lax.broadcasted_iota(jnp.int32, (lt, 1), 0) < n_valid
        beta_all = jnp.where(valid, beta_all, 0.0)
        g_all = jnp.where(valid, g_all, 0.0)
    g_hi = g_all.astype(BF16)
    g_lo = (g_all - g_hi.astype(F32)).astype(BF16)
    gc_all = _dot(tri_ref[...], g_hi) + _dot(tri_ref[...], g_lo)
    gc_rows = gc_all.T

    lv = lv_ref[...]
    incl = lv < NOT_SAME_CHUNK
    strict = jnp.logical_and(incl, lv >= 0.0)
    eye = jnp.where(lv == -1.0, 1.0, 0.0)
    n_levels = int(math.log2(chunk))

    for h in range(GDN_H):
        gcol = gc_all[:, GDN_H + h:GDN_H + h + 1]
        grow = gc_rows[GDN_H + h:GDN_H + h + 1, :]
        bcol = beta_all[:, h:h + 1]
        decay = jnp.where(incl, jnp.exp(jnp.where(incl, gcol - grow, 0.0)), 0.0)
        q = l2n(conv_act(h * GDN_D)) * (GDN_D ** -0.5)
        k = l2n(conv_act(GDN_W + h * GDN_D))
        v = conv_act(2 * GDN_W + h * GDN_D)
        mm = jnp.where(strict, bcol * _dot_nt(k, k) * decay, 0.0)
        aqk = _dot_nt(q, k) * decay
        x = eye - jnp.where(lv == 0.0, mm, 0.0)
        for lev in range(1, n_levels):
            x = x - _dot(x, _dot(jnp.where(lv == float(lev), mm, 0.0), x))
        eg = jnp.exp(gcol)
        uw = _dot(x, jnp.concatenate([v * bcol, k * (bcol * eg)], axis=1))
        u, w = uw[:, :GDN_D], uw[:, GDN_D:]
        qg = q * eg
        for c in range(lt // chunk):
            r0, r1 = c * chunk, (c + 1) * chunk
            glast = gcol[r1 - 1:r1, :]
            kg = k[r0:r1] * jnp.exp(glast - gcol[r0:r1])
            s = s_scr[h]
            wq = _dot(jnp.concatenate([w[r0:r1], qg[r0:r1]], axis=0), s)
            vn = u[r0:r1] - wq[:chunk]
            pieces = [vn if cc == c else jnp.zeros((chunk, GDN_D), F32) for cc in range(lt // chunk)]
            vn_full = pieces[0] if len(pieces) == 1 else jnp.concatenate(pieces, axis=0)
            o = wq[chunk:] + _dot(aqk[r0:r1, :], vn_full)
            s_scr[h] = s * jnp.exp(glast) + _dot_tn(kg, vn)
            gate = gate_src[r0:r1, h * GDN_D:(h + 1) * GDN_D].astype(F32)
            on = _rms(o, gn_ref[...]) * _silu(gate)
            if padded:
                if r0 < n_valid:
                    o_ref[0, :, h * GDN_D:(h + 1) * GDN_D] = on[0:n_valid].astype(o_ref.dtype)
            else:
                o_ref[r0:r1, h * GDN_D:(h + 1) * GDN_D] = on.astype(o_ref.dtype)

    @pl.when(j == pl.num_programs(1) - 1)
    def _():
        sout_ref[0] = s_scr[...]


def _chunk_level_codes(lt, chunk):
    i = jnp.arange(lt)[:, None]
    j = jnp.arange(lt)[None, :]
    code = jnp.full((lt, lt), NOT_SAME_CHUNK, F32)
    for lev in reversed(range(int(math.log2(chunk)))):
        code = jnp.where((i // (2 << lev)) == (j // (2 << lev)), float(lev), code)
    code = jnp.where(i > j, code, NOT_SAME_CHUNK)
    return jnp.where(i == j, -1.0, code).astype(F32)


def _gdn(qkv, gate, ba, cw, alog, dtb, gn, s0, cb, *, nb, seq, lt):
    padded = seq < lt
    n_valid = seq if padded else lt
    nt = 1 if padded else seq // lt
    chunk = GDN_CHUNK
    i = jnp.arange(lt)
    tri = ((i[:, None] // chunk == i[None, :] // chunk) & (i[:, None] >= i[None, :])).astype(BF16)
    lv = _chunk_level_codes(lt, chunk)
    if padded:
        tile = lambda c: pl.BlockSpec((1, seq, c), lambda b, j: (b, 0, 0))
        out_o = jax.ShapeDtypeStruct((nb, seq, GDN_W), BF16)
        scratch = [pltpu.VMEM((lt, GDN_W), F32), pltpu.VMEM((lt, LANE), F32)]
    else:
        tile = lambda c: pl.BlockSpec((lt, c), lambda b, j: (b * nt + j, 0))
        out_o = jax.ShapeDtypeStruct((nb * seq, GDN_W), BF16)
        scratch = []
    return pl.pallas_call(
        functools.partial(_gdn_kernel, lt=lt, n_valid=n_valid, chunk=chunk),
        grid=(nb, nt),
        in_specs=[tile(3 * GDN_W), tile(GDN_W), tile(LANE),
                  _const_spec((GDN_CONV, 3 * GDN_W)), _const_spec((1, LANE)), _const_spec((1, LANE)),
                  _const_spec((1, GDN_D)),
                  pl.BlockSpec((1, GDN_H, GDN_D, GDN_D), lambda b, j: (b, 0, 0, 0)),
                  pl.BlockSpec((1, GDN_CONV - 1, 3 * GDN_W), lambda b, j: (b, 0, 0)),
                  _const_spec((lt, lt)), _const_spec((lt, lt))],
        out_specs=[tile(GDN_W), pl.BlockSpec((1, GDN_H, GDN_D, GDN_D), lambda b, j: (b, 0, 0, 0))],
        out_shape=[out_o, jax.ShapeDtypeStruct((nb, GDN_H, GDN_D, GDN_D), F32)],
        scratch_shapes=[pltpu.VMEM((lt + 8, 3 * GDN_W), F32), pltpu.VMEM((GDN_H, GDN_D, GDN_D), F32)] + scratch,
        compiler_params=_cparams("parallel", "arbitrary"),
        name="gdn",
    )(qkv, gate, ba, cw, alog, dtb, gn, s0, cb, tri, lv)


def _lambda(l1q, l1k, l2q, l2k, lam_init):
    return (jnp.exp(jnp.sum(l1q[...] * l1k[...], axis=-1, keepdims=True))
            - jnp.exp(jnp.sum(l2q[...] * l2k[...], axis=-1, keepdims=True)) + lam_init)


def _flash_kernel(qi_tab, ki_tab, q_ref, k_ref, v_ref, l1q, l1k, l2q, l2k, dn_ref, o_ref,
                  m_scr, l_scr, acc_scr, *, tq, tk, lam_init):
    p = pl.program_id(1)
    qi, ki = qi_tab[p], ki_tab[p]

    @pl.when(ki == 0)
    def _():
        m_scr[...] = jnp.full(m_scr.shape, -jnp.inf, F32)
        l_scr[...] = jnp.zeros(l_scr.shape, F32)
        acc_scr[...] = jnp.zeros(acc_scr.shape, F32)

    rows = qi * tq + lax.broadcasted_iota(jnp.int32, (tq, tk), 0)
    cols = ki * tk + lax.broadcasted_iota(jnp.int32, (tq, tk), 1)
    visible = cols <= rows
    lane = lax.broadcasted_iota(jnp.int32, (1, DIFF_DV), 1)
    for h in range(DIFF_H):
        hs = slice(h * DIFF_DV, (h + 1) * DIFF_DV)
        qh = q_ref[:, hs]
        kh = k_ref[:, hs].astype(BF16)
        vh = v_ref[:, hs].astype(BF16)
        for c in range(2):
            qc = jnp.where((lane // DIFF_DH) == c, qh, jnp.zeros_like(qh))
            s = jnp.where(visible, _dot_nt(qc, kh), -jnp.inf)
            idx = 2 * h + c
            m_old = m_scr[idx]
            m_new = jnp.maximum(m_old, jnp.max(s, axis=-1, keepdims=True))
            alpha = jnp.exp(m_old - m_new)
            pr = jnp.exp(s - m_new)
            l_scr[idx] = alpha * l_scr[idx] + jnp.sum(pr, axis=-1, keepdims=True)
            acc_scr[idx] = alpha * acc_scr[idx] + _dot(pr.astype(BF16), vh)
            m_scr[idx] = m_new

    @pl.when(ki == qi)
    def _():
        lam = _lambda(l1q, l1k, l2q, l2k, lam_init)
        for h in range(DIFF_H):
            o = acc_scr[2 * h] / l_scr[2 * h] - lam * (acc_scr[2 * h + 1] / l_scr[2 * h + 1])
            o_ref[:, h * DIFF_DV:(h + 1) * DIFF_DV] = (_rms(o, dn_ref[...]) * (1.0 - lam_init)).astype(o_ref.dtype)


def _flash(q, k, v, lams, dn, *, nb, seq, tq, lam_init):
    nq = seq // tq
    pairs = [(a, b) for a in range(nq) for b in range(a + 1)]
    qi_tab = jnp.array([a for a, _ in pairs], jnp.int32)
    ki_tab = jnp.array([b for _, b in pairs], jnp.int32)
    qmap = lambda b, p, qt, kt: (b * nq + qt[p], 0)
    kmap = lambda b, p, qt, kt: (b * nq + kt[p], 0)
    cmap = lambda b, p, qt, kt: (0, 0)
    grid_spec = pltpu.PrefetchScalarGridSpec(
        num_scalar_prefetch=2,
        grid=(nb, len(pairs)),
        in_specs=[pl.BlockSpec((tq, DIFF_W), qmap), pl.BlockSpec((tq, DIFF_W), kmap),
                  pl.BlockSpec((tq, DIFF_W), kmap)]
                 + [pl.BlockSpec((1, DIFF_DH), cmap)] * 4 + [pl.BlockSpec((1, DIFF_DV), cmap)],
        out_specs=pl.BlockSpec((tq, DIFF_W), qmap),
        scratch_shapes=[pltpu.VMEM((2 * DIFF_H, tq, 1), F32), pltpu.VMEM((2 * DIFF_H, tq, 1), F32),
                        pltpu.VMEM((2 * DIFF_H, tq, DIFF_DV), F32)],
    )
    return pl.pallas_call(
        functools.partial(_flash_kernel, tq=tq, tk=tq, lam_init=lam_init),
        grid_spec=grid_spec,
        out_shape=jax.ShapeDtypeStruct((nb * seq, DIFF_W), BF16),
        compiler_params=_cparams("parallel", "arbitrary"),
        name="diff_flash",
    )(qi_tab, ki_tab, q, k, v, *lams, dn)


def _decode_kernel(pt_ref, q_ref, qmask_ref, kn_ref, vn_ref, l1q, l1k, l2q, l2k, dn_ref, *rest,
                   n_new, ppg, lam_init):
    kpages, vpages = rest[:ppg], rest[ppg:2 * ppg]
    o_ref = rest[2 * ppg]
    qb_scr, pad_scr, m_scr, l_scr, acc_scr = rest[2 * ppg + 1:]
    g = pl.program_id(1)
    nrow = 2 * DIFF_H * 8

    @pl.when(g == 0)
    def _():
        pad_scr[...] = jnp.zeros(pad_scr.shape, F32)
        pad_scr[0:n_new, :] = q_ref[0].astype(F32)
        q8 = pad_scr[0:8, :]
        qb_scr[...] = (jnp.concatenate([q8] * (2 * DIFF_H), axis=0) * qmask_ref[...]).astype(BF16)
        m_scr[...] = jnp.full(m_scr.shape, -jnp.inf, F32)
        l_scr[...] = jnp.zeros(l_scr.shape, F32)
        acc_scr[...] = jnp.zeros(acc_scr.shape, F32)

    def update(s, vals):
        m_old = m_scr[...]
        m_new = jnp.maximum(m_old, jnp.max(s, axis=-1, keepdims=True))
        alpha = jnp.exp(m_old - m_new)
        pr = jnp.exp(s - m_new)
        l_scr[...] = alpha * l_scr[...] + jnp.sum(pr, axis=-1, keepdims=True)
        pv = _dot(pr[:, 0:PAGE].astype(BF16), vals[0])
        for jj in range(1, len(vals)):
            pv = pv + _dot(pr[:, jj * PAGE:(jj + 1) * PAGE].astype(BF16), vals[jj])
        acc_scr[...] = alpha * acc_scr[...] + pv
        m_scr[...] = m_new

    qb = qb_scr[...]
    s = jnp.concatenate([_dot(qb, kp[...].astype(BF16)) for kp in kpages], axis=1)
    update(s, [jnp.concatenate([vp[:, h, :] for h in range(DIFF_H)], axis=1).astype(BF16) for vp in vpages])

    @pl.when(g == pl.num_programs(1) - 1)
    def _():
        pad_scr[...] = jnp.zeros(pad_scr.shape, F32)
        pad_scr[0:n_new, :] = kn_ref[0]
        kpad = pad_scr[...].astype(BF16)
        pad_scr[0:n_new, :] = vn_ref[0]
        vpad = pad_scr[...].astype(BF16)
        tok = lax.broadcasted_iota(jnp.int32, (nrow, PAGE), 0) % 8
        key = lax.broadcasted_iota(jnp.int32, (nrow, PAGE), 1)
        ok = jnp.logical_and(key < n_new, key <= tok)
        update(jnp.where(ok, _dot_nt(qb, kpad), -jnp.inf), [vpad])
        lam = _lambda(l1q, l1k, l2q, l2k, lam_init)
        on = acc_scr[...] / l_scr[...]
        half = nrow // 2
        for h in range(DIFF_H):
            hs = slice(h * DIFF_DV, (h + 1) * DIFF_DV)
            o = on[8 * h:8 * h + 8, hs] - lam * on[half + 8 * h:half + 8 * h + 8, hs]
            o_ref[0, :, hs] = (_rms(o, dn_ref[...]) * (1.0 - lam_init)).astype(o_ref.dtype)


def _decode(page_table, q, kn, vn, cache_k, cache_v, lams, dn, *, layer, ppg, lam_init):
    nb, n_new, _ = q.shape
    n_pages = page_table.shape[1]
    nrow = 2 * DIFF_H * 8
    r = jnp.arange(nrow)
    hc = 2 * ((r // 8) % DIFF_H) + r // (nrow // 2)
    qmask = (jnp.arange(DIFF_W)[None, :] // DIFF_DH == hc[:, None]).astype(F32)
    per_b = lambda c: pl.BlockSpec((1, n_new, c), lambda b, g, pt: (b, 0, 0))
    cmap = lambda b, g, pt: (0, 0)

    def kpage_spec(jj):
        return pl.BlockSpec((None, None, DIFF_W, PAGE),
                            lambda b, g, pt: (layer, pt[b * n_pages + g * ppg + jj], 0, 0))

    def vpage_spec(jj):
        return pl.BlockSpec((None, None, PAGE, DIFF_H, DIFF_DV),
                            lambda b, g, pt: (layer, pt[b * n_pages + g * ppg + jj], 0, 0, 0))

    grid_spec = pltpu.PrefetchScalarGridSpec(
        num_scalar_prefetch=1,
        grid=(nb, n_pages // ppg),
        in_specs=[per_b(DIFF_W), pl.BlockSpec((nrow, DIFF_W), cmap), per_b(DIFF_W), per_b(DIFF_W)]
                 + [pl.BlockSpec((1, DIFF_DH), cmap)] * 4 + [pl.BlockSpec((1, DIFF_DV), cmap)]
                 + [kpage_spec(jj) for jj in range(ppg)] + [vpage_spec(jj) for jj in range(ppg)],
        out_specs=pl.BlockSpec((1, 8, DIFF_W), lambda b, g, pt: (b, 0, 0)),
        scratch_shapes=[pltpu.VMEM((nrow, DIFF_W), BF16), pltpu.VMEM((PAGE, DIFF_W), F32),
                        pltpu.VMEM((nrow, 1), F32), pltpu.VMEM((nrow, 1), F32),
                        pltpu.VMEM((nrow, DIFF_W), F32)],
    )
    ck = cache_k.transpose(0, 1, 3, 4, 2).reshape(cache_k.shape[0], cache_k.shape[1], DIFF_W, PAGE)
    out = pl.pallas_call(
        functools.partial(_decode_kernel, n_new=n_new, ppg=ppg, lam_init=lam_init),
        grid_spec=grid_spec,
        out_shape=jax.ShapeDtypeStruct((nb, 8, DIFF_W), BF16),
        compiler_params=_cparams("parallel", "arbitrary"),
        name="diff_decode",
    )(page_table.reshape(-1), q, qmask, kn, vn, *lams, dn, *([ck] * ppg), *([cache_v] * ppg))
    return out[:, :n_new]


def _head_rms(z, gain, width):
    outs = [_rms(z[:, a:a + width], gain) for a in range(0, z.shape[-1], width)]
    return jnp.concatenate(outs, axis=1)


def _mixout_kernel(x_ref, og_ref, od_ref, wo_ref, nc_ref, wq_ref, qn_ref, x1_ref, qc_ref):
    mix = jnp.concatenate([og_ref[...], od_ref[...]], axis=1)
    x1 = x_ref[...] + _dot(mix, wo_ref[...])
    x1_ref[...] = x1
    hc = _rms(x1, nc_ref[...]).astype(BF16)
    qc = _head_rms(_dot(hc, wq_ref[...]), qn_ref[...], CA_DH) * (CA_DH ** -0.5)
    qc_ref[...] = qc.astype(qc_ref.dtype)


def _mixout(x, og, od, wo, nc, wq, qn, *, tm, q_dtype):
    m, d = x.shape
    row = lambda i: (i, 0)
    return pl.pallas_call(
        _mixout_kernel,
        grid=(m // tm,),
        in_specs=[pl.BlockSpec((tm, d), row), pl.BlockSpec((tm, GDN_W), row), pl.BlockSpec((tm, DIFF_W), row),
                  _weight_spec(wo.shape), _const_spec((1, d)), _weight_spec(wq.shape), _const_spec((1, CA_DH))],
        out_specs=[pl.BlockSpec((tm, d), row), pl.BlockSpec((tm, CA_W), row)],
        out_shape=[jax.ShapeDtypeStruct((m, d), F32), jax.ShapeDtypeStruct((m, CA_W), q_dtype)],
        compiler_params=_cparams("parallel"),
        name="mixout",
    )(x, og, od, wo, nc, wq, qn)


def _memkv_kernel(m_ref, nm_ref, wk_ref, wv_ref, kn_ref, mk_ref, mv_ref):
    mn = _rms(m_ref[...], nm_ref[...]).astype(BF16)
    mk_ref[...] = _head_rms(_dot(mn, wk_ref[...]), kn_ref[...], CA_DH)
    mv_ref[...] = _dot(mn, wv_ref[...])


def _memkv(mem, nm, wk, wv, kn, *, tm):
    m, d = mem.shape
    row = lambda i: (i, 0)
    return pl.pallas_call(
        _memkv_kernel,
        grid=(m // tm,),
        in_specs=[pl.BlockSpec((tm, d), row), _const_spec((1, d)), _const_spec(wk.shape), _const_spec(wv.shape),
                  _const_spec((1, CA_DH))],
        out_specs=[pl.BlockSpec((tm, CA_W), row)] * 2,
        out_shape=[jax.ShapeDtypeStruct((m, CA_W), F32)] * 2,
        compiler_params=_cparams("parallel"),
        name="memkv",
    )(mem, nm, wk, wv, kn)


def _cross_kernel(q_ref, mk_ref, mv_ref, o_ref, *pad_scr, n_valid):
    if pad_scr:
        pad_scr[0][...] = jnp.zeros(pad_scr[0].shape, F32)
        pad_scr[0][0:n_valid, :] = q_ref[0].astype(F32)
        q = pad_scr[0][...].astype(BF16)
    else:
        q = q_ref[...]
    heads_axis = len(mk_ref.shape) == 3 and mk_ref.shape[1] == CA_H
    for h in range(CA_H):
        hs = slice(h * CA_DH, (h + 1) * CA_DH)
        mk = mk_ref[:, h, :] if heads_axis else mk_ref[0, :, hs]
        mv = mv_ref[:, h, :] if heads_axis else mv_ref[0, :, hs]
        s = _dot_nt(q[:, hs], mk.astype(BF16))
        e = jnp.exp(s - jnp.max(s, axis=-1, keepdims=True))
        pr = e / jnp.sum(e, axis=-1, keepdims=True)
        oh = _dot(pr.astype(BF16), mv.astype(BF16))
        if pad_scr:
            o_ref[0, :, hs] = oh[0:n_valid].astype(o_ref.dtype)
        else:
            o_ref[:, hs] = oh.astype(o_ref.dtype)


def _cross(q, mk, mv, *, nb, seq, tq, layer=None):
    if layer is None:
        mem_spec = pl.BlockSpec((1, mk.shape[1], CA_W), lambda b, i: (b, 0, 0))
    else:
        mem_spec = pl.BlockSpec((None, None, mk.shape[2], CA_H, CA_DH), lambda b, i: (layer, b, 0, 0, 0))
    if seq < 8:
        nq = 1
        q_spec = pl.BlockSpec((1, seq, CA_W), lambda b, i: (b, 0, 0))
        out_shape = jax.ShapeDtypeStruct((nb, seq, CA_W), BF16)
        scratch = [pltpu.VMEM((8, CA_W), F32)]
    else:
        nq = seq // tq
        q_spec = pl.BlockSpec((tq, CA_W), lambda b, i: (b * nq + i, 0))
        out_shape = jax.ShapeDtypeStruct((nb * seq, CA_W), BF16)
        scratch = []
    return pl.pallas_call(
        functools.partial(_cross_kernel, n_valid=seq),
        grid=(nb, nq),
        in_specs=[q_spec, mem_spec, mem_spec],
        out_specs=q_spec,
        out_shape=out_shape,
        scratch_shapes=scratch,
        compiler_params=_cparams("parallel", "arbitrary"),
        name="cross_attn",
    )(q, mk, mv)


def _ffn_kernel(x_ref, oc_ref, wco_ref, nf_ref, wg_ref, wu_ref, cw_ref, wd_ref, hist_ref,
                y_ref, hout_ref, gbuf, acc, *, tm, stride, ncol):
    j = pl.program_id(1)
    d_ff = wg_ref.shape[1]
    hrows = (FFN_CONV - 1) * stride
    base = gbuf.shape[0] - tm

    @pl.when(j == 0)
    def _():
        gbuf[base - hrows:base, :] = hist_ref[0]

    @pl.when(j > 0)
    def _():
        gbuf[base - hrows:base, :] = gbuf[base + tm - hrows:base + tm, :]

    x2 = x_ref[...] + _dot(oc_ref[...], wco_ref[...])
    hf = _rms(x2, nf_ref[...]).astype(BF16)
    acc[...] = x2
    for c0 in range(0, d_ff, ncol):
        cs = slice(c0, c0 + ncol)
        gbuf[base:base + tm, cs] = _dot(hf, wg_ref[:, cs])
        w = cw_ref[:, cs]
        gt = gbuf[base - hrows:base - hrows + tm, cs] * w[0:1]
        for t in range(1, FFN_CONV):
            off = base - hrows + t * stride
            gt = gt + gbuf[off:off + tm, cs] * w[t:t + 1]
        act = (_silu(gt) * _dot(hf, wu_ref[:, cs])).astype(BF16)
        acc[...] += _dot(act, wd_ref[cs, :])
    y_ref[...] = acc[...]
    hout_ref[0] = gbuf[base + tm - hrows:base + tm, :]


def _ffn(x, oc, wco, nf, wg, wu, cw, wd, hist, *, ngroups, tm, stride, ncol):
    m, d = x.shape
    d_ff = wg.shape[1]
    nt = m // (ngroups * tm)
    hrows = (FFN_CONV - 1) * stride
    base = -(-hrows // 8) * 8
    row = lambda b, j: (b * nt + j, 0)
    grp = lambda b, j: (b, 0, 0)
    return pl.pallas_call(
        functools.partial(_ffn_kernel, tm=tm, stride=stride, ncol=ncol),
        grid=(ngroups, nt),
        in_specs=[pl.BlockSpec((tm, d), row), pl.BlockSpec((tm, CA_W), row), _weight_spec(wco.shape),
                  _const_spec((1, d)), _weight_spec(wg.shape), _weight_spec(wu.shape),
                  _const_spec((FFN_CONV, d_ff)), _weight_spec(wd.shape),
                  pl.BlockSpec((1, hrows, d_ff), grp)],
        out_specs=[pl.BlockSpec((tm, d), row), pl.BlockSpec((1, hrows, d_ff), grp)],
        out_shape=[jax.ShapeDtypeStruct((m, d), F32), jax.ShapeDtypeStruct((ngroups, hrows, d_ff), F32)],
        scratch_shapes=[pltpu.VMEM((base + tm, d_ff), F32), pltpu.VMEM((tm, d), F32)],
        compiler_params=_cparams("parallel", "arbitrary"),
        name="ffn",
    )(x, oc, wco, nf, wg, wu, cw, wd, hist)


def _rotary_tables(pos):
    half = DIFF_DH // 2
    inv = ROPE_THETA ** (-jnp.arange(half, dtype=F32) / half)
    ang = pos.astype(F32)[:, None] * inv[None, :]
    cos, sin, zero = jnp.cos(ang), jnp.sin(ang), jnp.zeros_like(ang)
    reps = (1, 2 * DIFF_H)
    return (jnp.tile(jnp.concatenate([cos, cos], axis=1), reps),
            jnp.tile(jnp.concatenate([-sin, zero], axis=1), reps),
            jnp.tile(jnp.concatenate([zero, sin], axis=1), reps))


def kernel(x_prompt, x_sample, mem_prompt, cache_k, cache_v, page_table, state_gdn, state_gdn_conv, cache_mem_k, cache_mem_v, state_ffn_conv, norm_mix, w_in, conv_qkv, a_log, dt_bias, gdn_norm, qnorm_diff, knorm_diff, lam_q1, lam_k1, lam_q2, lam_k2, diff_norm, w_out, norm_cross, norm_mem, w_cq, w_ck, w_cv, qnorm_cross, knorm_cross, w_co, norm_ffn, w_gate, w_up, conv_ffn, w_down):
    depth = w_in.shape[0]
    bp, lp, d = x_prompt.shape
    bs, ls, _ = x_sample.shape
    mlen = mem_prompt.shape[1]
    d_ff = w_gate.shape[2]
    past_len = page_table.shape[1] * PAGE
    mp, ms = bp * lp, bs * ls

    cos_p, sa_p, sb_p = _rotary_tables(jnp.arange(lp, dtype=jnp.int32))
    cos_s, sa_s, sb_s = [jnp.tile(t, (bs, 1)) for t in
                         _rotary_tables(past_len + jnp.arange(ls, dtype=jnp.int32))]
    ii = jnp.arange(DIFF_W)
    gmat = jnp.where(ii[:, None] // DIFF_DH == ii[None, :] // DIFF_DH, 1.0 / DIFF_DH, 0.0).astype(BF16)
    row = lambda a: a.reshape(1, -1).astype(F32)
    lane_pad = lambda a, off: jnp.zeros((1, LANE), F32).at[0, off:off + a.shape[0]].set(a.astype(F32))
    c_qkv, c_gate = 3 * GDN_W, 4 * GDN_W
    c_ba = c_gate + 2 * GDN_H

    xp = x_prompt.reshape(mp, d)
    xs = x_sample.reshape(ms, d)
    memp = mem_prompt.reshape(bp * mlen, d)
    zeros_s0 = jnp.zeros((bp, GDN_H, GDN_D, GDN_D), F32)
    zeros_cb = jnp.zeros((bp, GDN_CONV - 1, 3 * GDN_W), F32)
    zeros_fb = jnp.zeros((bp, FFN_CONV - 1, d_ff), F32)

    p_out, s_out = [], []
    for l in range(depth):
        lam_init = 0.8 - 0.6 * math.exp(-0.3 * l)
        wl = w_in[l]
        w_in_l = jnp.concatenate([wl[:, :c_gate], wl[:, c_ba:], wl[:, c_gate:c_ba],
                                  jnp.zeros((d, LANE - 2 * GDN_H), F32)], axis=1).astype(BF16)
        qg = jnp.tile(row(qnorm_diff[l]), (1, 2 * DIFF_H))
        kg = jnp.tile(row(knorm_diff[l]), (1, 2 * DIFF_H))
        lams = (row(lam_q1[l]), row(lam_k1[l]), row(lam_q2[l]), row(lam_k2[l]))
        alog = lane_pad(a_log[l], GDN_H)
        dtb = lane_pad(dt_bias[l], GDN_H)
        w_out_l, w_cq_l, w_co_l = w_out[l].astype(BF16), w_cq[l].astype(BF16), w_co[l].astype(BF16)
        w_g_l, w_u_l, w_d_l = w_gate[l].astype(BF16), w_up[l].astype(BF16), w_down[l].astype(BF16)
        inproj = functools.partial(_inproj, gain=row(norm_mix[l]), w=w_in_l, qg=qg, kg=kg, gmat=gmat)
        gdn = functools.partial(_gdn, cw=conv_qkv[l], alog=alog, dtb=dtb, gn=row(gdn_norm[l]))
        mixout = functools.partial(_mixout, wo=w_out_l, nc=row(norm_cross[l]), wq=w_cq_l, qn=row(qnorm_cross[l]))
        ffn = functools.partial(_ffn, wco=w_co_l, nf=row(norm_ffn[l]), wg=w_g_l, wu=w_u_l, cw=conv_ffn[l],
                                wd=w_d_l)

        mk_p, mv_p = _memkv(memp, row(norm_mem[l]), w_ck[l].astype(BF16), w_cv[l].astype(BF16),
                            row(knorm_cross[l]), tm=512)
        qkv, gate, ba, q, k, v = inproj(xp, cos=cos_p, sa=sa_p, sb=sb_p, tm=512, q_dtype=BF16)
        og, sp = gdn(qkv, gate, ba, s0=zeros_s0, cb=zeros_cb, nb=bp, seq=lp, lt=256)
        od = _flash(q, k, v, lams, row(diff_norm[l]), nb=bp, seq=lp, tq=512, lam_init=lam_init)
        x1, qc = mixout(xp, og, od, tm=512, q_dtype=BF16)
        oc = _cross(qc, mk_p.reshape(bp, mlen, CA_W), mv_p.reshape(bp, mlen, CA_W), nb=bp, seq=lp, tq=512)
        xp, fp = ffn(x1, oc, hist=zeros_fb, ngroups=bp, tm=512, stride=1, ncol=1408)
        p_out.append((k.reshape(bp, lp, 2 * DIFF_H, DIFF_DH), v.reshape(bp, lp, DIFF_H, DIFF_DV), sp,
                      qkv.reshape(bp, lp, c_qkv)[:, lp - (GDN_CONV - 1):],
                      mk_p.reshape(bp, mlen, CA_H, CA_DH), mv_p.reshape(bp, mlen, CA_H, CA_DH), fp))

        qkv, gate, ba, q, k, v = inproj(xs, cos=cos_s, sa=sa_s, sb=sb_s, tm=ms, q_dtype=F32)
        r3 = lambda a: a.reshape(bs, ls, a.shape[-1])
        og, ss = gdn(r3(qkv), r3(gate), r3(ba), s0=state_gdn[l], cb=state_gdn_conv[l], nb=bs, seq=ls,
                     lt=GDN_CHUNK)
        od = _decode(page_table, r3(q), r3(k), r3(v), cache_k, cache_v, lams, row(diff_norm[l]),
                     layer=l, ppg=8, lam_init=lam_init)
        x1, qc = mixout(xs, og.reshape(ms, GDN_W), od.reshape(ms, DIFF_W), tm=ms, q_dtype=F32)
        oc = _cross(r3(qc), cache_mem_k, cache_mem_v, nb=bs, seq=ls, tq=ls, layer=l)
        tmaj = lambda a: a.reshape(bs, ls, -1).transpose(1, 0, 2).reshape(ms, -1)
        hist = state_ffn_conv[l].transpose(1, 0, 2).reshape(1, (FFN_CONV - 1) * bs, d_ff)
        y_t, fs_t = ffn(tmaj(x1), tmaj(oc), hist=hist, ngroups=1, tm=ms, stride=bs, ncol=1408)
        xs = y_t.reshape(ls, bs, d).transpose(1, 0, 2).reshape(ms, d)
        fs = fs_t.reshape(FFN_CONV - 1, bs, d_ff).transpose(1, 0, 2)
        conv_s = jnp.concatenate([state_gdn_conv[l], r3(qkv)], axis=1)[:, ls:]
        s_out.append((k.reshape(bs, ls, 2 * DIFF_H, DIFF_DH), v.reshape(bs, ls, DIFF_H, DIFF_DV), ss, conv_s, fs))

    pk, pv, pg, pc, pmk, pmv, pf = [jnp.stack(t, axis=0) for t in zip(*p_out)]
    sk, sv, sg, sc, sf = [jnp.stack(t, axis=0) for t in zip(*s_out)]
    return (xp.reshape(bp, lp, d), xs.reshape(bs, ls, d), pk, pv, pg, pc, pmk, pmv, pf, sk, sv, sg, sc, sf)
```

```python
import functools
import math

import jax
import jax.numpy as jnp
from jax import lax
from jax.experimental import pallas as pl
from jax.experimental.pallas import tpu as pltpu

F32 = jnp.float32
BF16 = jnp.bfloat16

EPS = 1e-6
ROPE_THETA = 10000.0
GDN_H = 4
GDN_D = 128
GDN_W = GDN_H * GDN_D
GDN_CONV = 4
GDN_CHUNK = 64
DIFF_H = 4
DIFF_DH = 64
DIFF_DV = 128
DIFF_W = DIFF_H * DIFF_DV
CA_H = 4
CA_DH = 128
CA_W = CA_H * CA_DH
FFN_CONV = 3
PAGE = 128
LANE = 128
VMEM_LIMIT = 56 * 1024 * 1024
NOT_SAME_CHUNK = 99.0


def _cparams(*sem):
    return pltpu.CompilerParams(dimension_semantics=sem, vmem_limit_bytes=VMEM_LIMIT)


def _dot(a, b):
    return jnp.dot(a, b, preferred_element_type=F32)


def _dot_nt(a, b):
    return lax.dot_general(a, b, (((1,), (1,)), ((), ())), preferred_element_type=F32)


def _dot_tn(a, b):
    return lax.dot_general(a, b, (((0,), (0,)), ((), ())), preferred_element_type=F32)


def _sigmoid(x):
    return 1.0 / (1.0 + jnp.exp(-x))


def _silu(x):
    return x * _sigmoid(x)


def _softplus(x):
    return jnp.maximum(x, 0.0) + jnp.log1p(jnp.exp(-jnp.abs(x)))


def _rms(x, gain):
    return x * lax.rsqrt(jnp.mean(x * x, axis=-1, keepdims=True) + EPS) * gain


def _const_spec(shape):
    nd = len(shape)
    return pl.BlockSpec(shape, lambda *_: (0,) * nd)


def _weight_spec(shape):
    nd = len(shape)
    return pl.BlockSpec(shape, lambda *_: (0,) * nd, pipeline_mode=pl.Buffered(1))


def _inproj_kernel(x_ref, gain_ref, w_ref, qg_ref, kg_ref, cos_ref, sa_ref, sb_ref, gmat_ref,
                   qkv_ref, gate_ref, ba_ref, q_ref, k_ref, v_ref):
    h = _rms(x_ref[...], gain_ref[...]).astype(BF16)
    c_qkv, c_gate = 3 * GDN_W, 4 * GDN_W
    c_q, c_k, c_v = c_gate + DIFF_W, c_gate + 2 * DIFF_W, c_gate + 3 * DIFF_W

    def proj(a, b):
        return _dot(h, w_ref[:, a:b])

    qkv_ref[...] = proj(0, c_qkv)
    gate_ref[...] = proj(c_qkv, c_gate).astype(gate_ref.dtype)
    v_ref[...] = proj(c_k, c_v)
    ba_ref[...] = proj(c_v, c_v + LANE)
    cos, sa, sb = cos_ref[...], sa_ref[...], sb_ref[...]
    half = DIFF_DH // 2

    def norm_rot(z, g):
        ms = _dot((z * z).astype(BF16), gmat_ref[...])
        y = z * lax.rsqrt(ms + EPS) * g
        return y * cos + pltpu.roll(y, DIFF_W - half, 1) * sa + pltpu.roll(y, half, 1) * sb

    q_ref[...] = (norm_rot(proj(c_gate, c_q), qg_ref[...]) * (DIFF_DH ** -0.5)).astype(q_ref.dtype)
    k_ref[...] = norm_rot(proj(c_q, c_k), kg_ref[...])


def _inproj(x, gain, w, qg, kg, cos, sa, sb, gmat, *, tm, q_dtype):
    m, d = x.shape
    npos = cos.shape[0] // tm
    nw = w.shape[1]
    row = lambda i: (i, 0)
    pos = lambda i: (i % npos, 0)
    return pl.pallas_call(
        _inproj_kernel,
        grid=(m // tm,),
        in_specs=[pl.BlockSpec((tm, d), row), _const_spec((1, d)), _weight_spec((d, nw)),
                  _const_spec((1, DIFF_W)), _const_spec((1, DIFF_W)),
                  pl.BlockSpec((tm, DIFF_W), pos), pl.BlockSpec((tm, DIFF_W), pos),
                  pl.BlockSpec((tm, DIFF_W), pos), _const_spec((DIFF_W, DIFF_W))],
        out_specs=[pl.BlockSpec((tm, 3 * GDN_W), row), pl.BlockSpec((tm, GDN_W), row),
                   pl.BlockSpec((tm, LANE), row), pl.BlockSpec((tm, DIFF_W), row),
                   pl.BlockSpec((tm, DIFF_W), row), pl.BlockSpec((tm, DIFF_W), row)],
        out_shape=[jax.ShapeDtypeStruct((m, 3 * GDN_W), F32), jax.ShapeDtypeStruct((m, GDN_W), BF16),
                   jax.ShapeDtypeStruct((m, LANE), F32), jax.ShapeDtypeStruct((m, DIFF_W), q_dtype),
                   jax.ShapeDtypeStruct((m, DIFF_W), F32), jax.ShapeDtypeStruct((m, DIFF_W), F32)],
        compiler_params=_cparams("parallel"),
        name="inproj",
    )(x, gain, w, qg, kg, cos, sa, sb, gmat)


def _gdn_kernel(x_ref, gate_ref, ba_ref, cw_ref, alog_ref, dtb_ref, gn_ref, s0_ref, cb_ref, tri_ref, lv_ref,
                o_ref, sout_ref, xbuf, s_scr, *pad_scr, lt, n_valid, chunk):
    j = pl.program_id(1)
    padded = n_valid < lt
    tail = GDN_CONV - 1

    @pl.when(j == 0)
    def _():
        for h in range(GDN_H):
            s_scr[h] = s0_ref[0, h].T
        xbuf[0:8, :] = jnp.zeros((8, 3 * GDN_W), F32)
        xbuf[8 - tail:8, :] = cb_ref[0]

    @pl.when(j > 0)
    def _():
        xbuf[0:8, :] = xbuf[lt:lt + 8, :]

    if padded:
        gate_scr, ba_scr = pad_scr
        xbuf[8:8 + lt, :] = jnp.zeros((lt, 3 * GDN_W), F32)
        xbuf[8:8 + n_valid, :] = x_ref[0]
        gate_scr[...] = jnp.zeros(gate_scr.shape, F32)
        gate_scr[0:n_valid, :] = gate_ref[0].astype(F32)
        ba_scr[...] = jnp.zeros(ba_scr.shape, F32)
        ba_scr[0:n_valid, :] = ba_ref[0]
        gate_src, ba = gate_scr, ba_scr[...]
    else:
        xbuf[8:8 + lt, :] = x_ref[...]
        gate_src, ba = gate_ref, ba_ref[...]

    def conv_act(c0):
        w = cw_ref[:, c0:c0 + GDN_D]
        y = xbuf[8 - tail:8 - tail + lt, c0:c0 + GDN_D] * w[0:1]
        for t in range(1, GDN_CONV):
            y = y + xbuf[8 - tail + t:8 - tail + t + lt, c0:c0 + GDN_D] * w[t:t + 1]
        return _silu(y)

    def l2n(z):
        return z * lax.rsqrt(jnp.sum(z * z, axis=-1, keepdims=True) + EPS)

    beta_all = _sigmoid(ba)
    g_all = -jnp.exp(alog_ref[...]) * _softplus(ba + dtb_ref[...])
    if padded:
        valid = lax.broadcasted_iota(jnp.int32, (lt, 1), 0) < n_valid
        beta_all = jnp.where(valid, beta_all, 0.0)
        g_all = jnp.where(valid, g_all, 0.0)
    g_hi = g_all.astype(BF16)
    g_lo = (g_all - g_hi.astype(F32)).astype(BF16)
    gc_all = _dot(tri_ref[...], g_hi) + _dot(tri_ref[...], g_lo)
    gc_rows = gc_all.T

    lv = lv_ref[...]
    incl = lv < NOT_SAME_CHUNK
    strict = jnp.logical_and(incl, lv >= 0.0)
    eye = jnp.where(lv == -1.0, 1.0, 0.0)
    n_levels = int(math.log2(chunk))

    heads = range(GDN_H)
    gcol = [gc_all[:, GDN_H + h:GDN_H + h + 1] for h in heads]
    bcol = [beta_all[:, h:h + 1] for h in heads]
    eg = [jnp.exp(gcol[h]) for h in heads]
    k, qg, mm, aqk, rhs = [], [], [], [], []
    for h in heads:
        grow = gc_rows[GDN_H + h:GDN_H + h + 1, :]
        decay = jnp.where(incl, jnp.exp(jnp.where(incl, gcol[h] - grow, 0.0)), 0.0)
        qh = l2n(conv_act(h * GDN_D)) * (GDN_D ** -0.5)
        kh = l2n(conv_act(GDN_W + h * GDN_D))
        vh = conv_act(2 * GDN_W + h * GDN_D)
        kb = kh.astype(BF16)
        mm.append(jnp.where(strict, bcol[h] * _dot_nt(kb, kb) * decay, 0.0))
        aqk.append((_dot_nt(qh.astype(BF16), kb) * decay).astype(BF16))
        rhs.append(jnp.concatenate([vh * bcol[h], kh * (bcol[h] * eg[h])], axis=1).astype(BF16))
        k.append(kh)
        qg.append(qh * eg[h])
    x = [eye - jnp.where(lv == 0.0, mm[h], 0.0) for h in heads]
    for lev in range(1, n_levels):
        xb = [x[h].astype(BF16) for h in heads]
        y = [_dot(jnp.where(lv == float(lev), mm[h], 0.0).astype(BF16), xb[h]).astype(BF16) for h in heads]
        x = [x[h] - _dot(xb[h], y[h]) for h in heads]
    uwb = [_dot(x[h].astype(BF16), rhs[h]).astype(BF16) for h in heads]
    ou = [_dot(aqk[h], uwb[h]) for h in heads]
    qp = [(qg[h] - ou[h][:, GDN_D:]).astype(BF16) for h in heads]
    for c in range(lt // chunk):
        r0, r1 = c * chunk, (c + 1) * chunk
        for h in heads:
            glast = gcol[h][r1 - 1:r1, :]
            kg = (k[h][r0:r1] * jnp.exp(glast - gcol[h][r0:r1])).astype(BF16)
            ab = _dot_tn(uwb[h][r0:r1, :], kg)
            st = s_scr[h]
            stb = st.astype(BF16)
            o = _dot_nt(qp[h][r0:r1], stb) + ou[h][r0:r1, :GDN_D]
            s_scr[h] = st * jnp.exp(glast) + ab[:GDN_D] - _dot(stb, ab[GDN_D:].astype(BF16))
            gate = gate_src[r0:r1, h * GDN_D:(h + 1) * GDN_D].astype(F32)
            on = _rms(o, gn_ref[...]) * _silu(gate)
            if padded:
                if r0 < n_valid:
                    o_ref[0, :, h * GDN_D:(h + 1) * GDN_D] = on[0:n_valid].astype(o_ref.dtype)
            else:
                o_ref[r0:r1, h * GDN_D:(h + 1) * GDN_D] = on.astype(o_ref.dtype)

    @pl.when(j == pl.num_programs(1) - 1)
    def _():
        for h in range(GDN_H):
            sout_ref[0, h] = s_scr[h].T


def _chunk_level_codes(lt, chunk):
    i = jnp.arange(lt)[:, None]
    j = jnp.arange(lt)[None, :]
    code = jnp.full((lt, lt), NOT_SAME_CHUNK, F32)
    for lev in reversed(range(int(math.log2(chunk)))):
        code = jnp.where((i // (2 << lev)) == (j // (2 << lev)), float(lev), code)
    code = jnp.where(i > j, code, NOT_SAME_CHUNK)
    return jnp.where(i == j, -1.0, code).astype(F32)


def _gdn(qkv, gate, ba, cw, alog, dtb, gn, s0, cb, *, nb, seq, lt):
    padded = seq < lt
    n_valid = seq if padded else lt
    nt = 1 if padded else seq // lt
    chunk = GDN_CHUNK
    i = jnp.arange(lt)
    tri = ((i[:, None] // chunk == i[None, :] // chunk) & (i[:, None] >= i[None, :])).astype(BF16)
    lv = _chunk_level_codes(lt, chunk)
    if padded:
        tile = lambda c: pl.BlockSpec((1, seq, c), lambda b, j: (b, 0, 0))
        out_o = jax.ShapeDtypeStruct((nb, seq, GDN_W), BF16)
        scratch = [pltpu.VMEM((lt, GDN_W), F32), pltpu.VMEM((lt, LANE), F32)]
    else:
        tile = lambda c: pl.BlockSpec((lt, c), lambda b, j: (b * nt + j, 0))
        out_o = jax.ShapeDtypeStruct((nb * seq, GDN_W), BF16)
        scratch = []
    return pl.pallas_call(
        functools.partial(_gdn_kernel, lt=lt, n_valid=n_valid, chunk=chunk),
        grid=(nb, nt),
        in_specs=[tile(3 * GDN_W), tile(GDN_W), tile(LANE),
                  _const_spec((GDN_CONV, 3 * GDN_W)), _const_spec((1, LANE)), _const_spec((1, LANE)),
                  _const_spec((1, GDN_D)),
                  pl.BlockSpec((1, GDN_H, GDN_D, GDN_D), lambda b, j: (b, 0, 0, 0)),
                  pl.BlockSpec((1, GDN_CONV - 1, 3 * GDN_W), lambda b, j: (b, 0, 0)),
                  _const_spec((lt, lt)), _const_spec((lt, lt))],
        out_specs=[tile(GDN_W), pl.BlockSpec((1, GDN_H, GDN_D, GDN_D), lambda b, j: (b, 0, 0, 0))],
        out_shape=[out_o, jax.ShapeDtypeStruct((nb, GDN_H, GDN_D, GDN_D), F32)],
        scratch_shapes=[pltpu.VMEM((lt + 8, 3 * GDN_W), F32), pltpu.VMEM((GDN_H, GDN_D, GDN_D), F32)] + scratch,
        compiler_params=_cparams("parallel", "arbitrary"),
        name="gdn",
    )(qkv, gate, ba, cw, alog, dtb, gn, s0, cb, tri, lv)


def _lambda(l1q, l1k, l2q, l2k, lam_init):
    return (jnp.exp(jnp.sum(l1q[...] * l1k[...], axis=-1, keepdims=True))
            - jnp.exp(jnp.sum(l2q[...] * l2k[...], axis=-1, keepdims=True)) + lam_init)


def _flash_kernel(qi_tab, ki_tab, q_ref, k_ref, v_ref, l1q, l1k, l2q, l2k, dn_ref, o_ref,
                  m_scr, l_scr, acc_scr, *, tq, tk, lam_init):
    p = pl.program_id(1)
    qi, ki = qi_tab[p], ki_tab[p]

    @pl.when(ki == 0)
    def _():
        m_scr[...] = jnp.full(m_scr.shape, -jnp.inf, F32)
        l_scr[...] = jnp.zeros(l_scr.shape, F32)
        acc_scr[...] = jnp.zeros(acc_scr.shape, F32)

    lane = lax.broadcasted_iota(jnp.int32, (1, DIFF_DV), 1)
    pairs = range(2 * DIFF_H)

    def step(masked):
        kc, vt = [], []
        for h in range(DIFF_H):
            hs = slice(h * DIFF_DV, (h + 1) * DIFF_DV)
            kh = k_ref[:, hs].astype(BF16)
            vt.append(v_ref[:, hs].T.astype(BF16))
            kc += [jnp.where((lane // DIFF_DH) == c, kh, jnp.zeros_like(kh)) for c in range(2)]
        st = [_dot_nt(kc[i], q_ref[:, (i // 2) * DIFF_DV:(i // 2 + 1) * DIFF_DV]) for i in pairs]
        if masked:
            keys = ki * tk + lax.broadcasted_iota(jnp.int32, (tk, tq), 0)
            queries = qi * tq + lax.broadcasted_iota(jnp.int32, (tk, tq), 1)
            visible = keys <= queries
            st = [jnp.where(visible, st[i], -jnp.inf) for i in pairs]
        m_old = [m_scr[i:i + 1, :] for i in pairs]
        m_new = [jnp.maximum(m_old[i], jnp.max(st[i], axis=0, keepdims=True)) for i in pairs]
        pt = [jnp.exp(st[i] - m_new[i]) for i in pairs]
        alpha = [jnp.exp(m_old[i] - m_new[i]) for i in pairs]
        for i in pairs:
            l_scr[i:i + 1, :] = alpha[i] * l_scr[i:i + 1, :] + jnp.sum(pt[i], axis=0, keepdims=True)
            acc_scr[i] = alpha[i] * acc_scr[i] + _dot(vt[i // 2], pt[i].astype(BF16))
            m_scr[i:i + 1, :] = m_new[i]

    @pl.when(ki < qi)
    def _():
        step(False)

    @pl.when(ki == qi)
    def _():
        step(True)
        lam = _lambda(l1q, l1k, l2q, l2k, lam_init)
        for h in range(DIFF_H):
            ot = (acc_scr[2 * h] / l_scr[2 * h:2 * h + 1, :]
                  - lam * (acc_scr[2 * h + 1] / l_scr[2 * h + 1:2 * h + 2, :]))
            o_ref[:, h * DIFF_DV:(h + 1) * DIFF_DV] = (_rms(ot.T, dn_ref[...]) * (1.0 - lam_init)).astype(o_ref.dtype)


def _flash(q, k, v, lams, dn, *, nb, seq, tq, lam_init):
    nq = seq // tq
    pairs = [(a, b) for a in range(nq) for b in range(a + 1)]
    qi_tab = jnp.array([a for a, _ in pairs], jnp.int32)
    ki_tab = jnp.array([b for _, b in pairs], jnp.int32)
    qmap = lambda b, p, qt, kt: (b * nq + qt[p], 0)
    kmap = lambda b, p, qt, kt: (b * nq + kt[p], 0)
    cmap = lambda b, p, qt, kt: (0, 0)
    grid_spec = pltpu.PrefetchScalarGridSpec(
        num_scalar_prefetch=2,
        grid=(nb, len(pairs)),
        in_specs=[pl.BlockSpec((tq, DIFF_W), qmap), pl.BlockSpec((tq, DIFF_W), kmap),
                  pl.BlockSpec((tq, DIFF_W), kmap)]
                 + [pl.BlockSpec((1, DIFF_DH), cmap)] * 4 + [pl.BlockSpec((1, DIFF_DV), cmap)],
        out_specs=pl.BlockSpec((tq, DIFF_W), qmap),
        scratch_shapes=[pltpu.VMEM((2 * DIFF_H, tq), F32), pltpu.VMEM((2 * DIFF_H, tq), F32),
                        pltpu.VMEM((2 * DIFF_H, DIFF_DV, tq), F32)],
    )
    return pl.pallas_call(
        functools.partial(_flash_kernel, tq=tq, tk=tq, lam_init=lam_init),
        grid_spec=grid_spec,
        out_shape=jax.ShapeDtypeStruct((nb * seq, DIFF_W), BF16),
        compiler_params=_cparams("parallel", "arbitrary"),
        name="diff_flash",
    )(qi_tab, ki_tab, q, k, v, *lams, dn)


def _decode_kernel(pt_ref, q_ref, qmask_ref, kn_ref, vn_ref, expand_ref, hmask_ref, l1q, l1k, l2q, l2k, dn_ref,
                   *rest, n_new, ppg, lam_init):
    kpages, vpages = rest[:ppg], rest[ppg:2 * ppg]
    o_ref = rest[2 * ppg]
    qb_scr, kpad_scr, vpad_scr, m_scr, l_scr, acc_scr = rest[2 * ppg + 1:]
    g = pl.program_id(1)
    nrow = 2 * DIFF_H * 8

    @pl.when(g == 0)
    def _():
        kpad_scr[...] = jnp.zeros(kpad_scr.shape, F32)
        kpad_scr[0:n_new, :] = q_ref[0].astype(F32)
        q8 = kpad_scr[0:8, :]
        qb_scr[...] = (jnp.concatenate([q8] * (2 * DIFF_H), axis=0) * qmask_ref[...]).astype(BF16)
        m_scr[...] = jnp.full(m_scr.shape, -jnp.inf, F32)
        l_scr[...] = jnp.zeros(l_scr.shape, F32)
        acc_scr[...] = jnp.zeros(acc_scr.shape, F32)

    def update(s, vals):
        m_old = m_scr[...]
        m_new = jnp.maximum(m_old, jnp.max(s, axis=-1, keepdims=True))
        alpha = jnp.exp(m_old - m_new)
        pr = jnp.exp(s - m_new)
        l_scr[...] = alpha * l_scr[...] + jnp.sum(pr, axis=-1, keepdims=True)
        prb = pr.astype(BF16)
        n = len(vals)
        stacked = prb if n == 1 else jnp.concatenate([prb[:, jj * PAGE:(jj + 1) * PAGE] for jj in range(n)], axis=0)
        spread = _dot(stacked, expand_ref[...])
        pes = [(spread[jj * nrow:(jj + 1) * nrow] * hmask_ref[...]).astype(BF16) for jj in range(n)]
        terms = [_dot(pes[jj], vals[jj]) for jj in range(n)]
        while len(terms) > 1:
            terms = [terms[i] + terms[i + 1] for i in range(0, len(terms) - 1, 2)] + terms[len(terms) & ~1:]
        acc_scr[...] = alpha * acc_scr[...] + terms[0]
        m_scr[...] = m_new

    qb = qb_scr[...]
    s = jnp.concatenate([_dot(qb, kp[...].astype(BF16)) for kp in kpages], axis=1)
    update(s, [vp[...].astype(BF16) for vp in vpages])

    @pl.when(g == pl.num_programs(1) - 1)
    def _():
        kpad_scr[...] = jnp.zeros(kpad_scr.shape, F32)
        kpad_scr[0:n_new, :] = kn_ref[0]
        vpad_scr[...] = jnp.zeros(vpad_scr.shape, F32)
        vpad_scr[0:n_new * DIFF_H, :] = vn_ref[0]
        tok = lax.broadcasted_iota(jnp.int32, (nrow, PAGE), 0) % 8
        key = lax.broadcasted_iota(jnp.int32, (nrow, PAGE), 1)
        ok = jnp.logical_and(key < n_new, key <= tok)
        update(jnp.where(ok, _dot_nt(qb, kpad_scr[...].astype(BF16)), -jnp.inf), [vpad_scr[...].astype(BF16)])
        lam = _lambda(l1q, l1k, l2q, l2k, lam_init)
        on = acc_scr[...] / l_scr[...]
        half = nrow // 2
        for h in range(DIFF_H):
            o = on[8 * h:8 * h + 8, :] - lam * on[half + 8 * h:half + 8 * h + 8, :]
            o_ref[0, :, h * DIFF_DV:(h + 1) * DIFF_DV] = (_rms(o, dn_ref[...]) * (1.0 - lam_init)).astype(o_ref.dtype)


def _decode(page_table, q, kn, vn, cache_k, cache_v, lams, dn, *, layer, ppg, lam_init):
    nb, n_new, _ = q.shape
    n_pages = page_table.shape[1]
    nrow = 2 * DIFF_H * 8
    r = jnp.arange(nrow)
    hc = 2 * ((r // 8) % DIFF_H) + r // (nrow // 2)
    qmask = (jnp.arange(DIFF_W)[None, :] // DIFF_DH == hc[:, None]).astype(F32)
    vrows = PAGE * DIFF_H
    jj = jnp.arange(vrows)
    expand = (jj[None, :] // DIFF_H == jnp.arange(PAGE)[:, None]).astype(BF16)
    hmask = (jj[None, :] % DIFF_H == ((r // 8) % DIFF_H)[:, None]).astype(F32)
    per_b = lambda n, c: pl.BlockSpec((1, n, c), lambda b, g, pt: (b, 0, 0))
    cmap = lambda b, g, pt: (0, 0)

    def page_spec(j, rows, cols):
        return pl.BlockSpec((None, None, rows, cols),
                            lambda b, g, pt: (layer, pt[b * n_pages + g * ppg + j], 0, 0))

    grid_spec = pltpu.PrefetchScalarGridSpec(
        num_scalar_prefetch=1,
        grid=(nb, n_pages // ppg),
        in_specs=[per_b(n_new, DIFF_W), pl.BlockSpec((nrow, DIFF_W), cmap), per_b(n_new, DIFF_W),
                  per_b(n_new * DIFF_H, DIFF_DV), pl.BlockSpec((PAGE, vrows), cmap),
                  pl.BlockSpec((nrow, vrows), cmap)]
                 + [pl.BlockSpec((1, DIFF_DH), cmap)] * 4 + [pl.BlockSpec((1, DIFF_DV), cmap)]
                 + [page_spec(j, DIFF_W, PAGE) for j in range(ppg)]
                 + [page_spec(j, vrows, DIFF_DV) for j in range(ppg)],
        out_specs=pl.BlockSpec((1, 8, DIFF_W), lambda b, g, pt: (b, 0, 0)),
        scratch_shapes=[pltpu.VMEM((nrow, DIFF_W), BF16), pltpu.VMEM((PAGE, DIFF_W), F32),
                        pltpu.VMEM((vrows, DIFF_DV), F32),
                        pltpu.VMEM((nrow, 1), F32), pltpu.VMEM((nrow, 1), F32),
                        pltpu.VMEM((nrow, DIFF_DV), F32)],
    )
    ck = cache_k.transpose(0, 1, 3, 4, 2).reshape(cache_k.shape[0], cache_k.shape[1], DIFF_W, PAGE)
    cv = cache_v.reshape(cache_v.shape[0], cache_v.shape[1], vrows, DIFF_DV)
    out = pl.pallas_call(
        functools.partial(_decode_kernel, n_new=n_new, ppg=ppg, lam_init=lam_init),
        grid_spec=grid_spec,
        out_shape=jax.ShapeDtypeStruct((nb, 8, DIFF_W), BF16),
        compiler_params=_cparams("parallel", "arbitrary"),
        name="diff_decode",
    )(page_table.reshape(-1), q, qmask, kn, vn.reshape(nb, n_new * DIFF_H, DIFF_DV), expand, hmask, *lams, dn,
      *([ck] * ppg), *([cv] * ppg))
    return out[:, :n_new]


def _head_rms(z, gain, width):
    outs = [_rms(z[:, a:a + width], gain) for a in range(0, z.shape[-1], width)]
    return jnp.concatenate(outs, axis=1)


def _mixout_kernel(x_ref, og_ref, od_ref, wo_ref, nc_ref, wq_ref, qn_ref, x1_ref, qc_ref):
    mix = jnp.concatenate([og_ref[...], od_ref[...]], axis=1)
    x1 = x_ref[...] + _dot(mix, wo_ref[...])
    x1_ref[...] = x1
    hc = _rms(x1, nc_ref[...]).astype(BF16)
    qc = _head_rms(_dot(hc, wq_ref[...]), qn_ref[...], CA_DH) * (CA_DH ** -0.5)
    qc_ref[...] = qc.astype(qc_ref.dtype)


def _mixout(x, og, od, wo, nc, wq, qn, *, tm, q_dtype):
    m, d = x.shape
    row = lambda i: (i, 0)
    return pl.pallas_call(
        _mixout_kernel,
        grid=(m // tm,),
        in_specs=[pl.BlockSpec((tm, d), row), pl.BlockSpec((tm, GDN_W), row), pl.BlockSpec((tm, DIFF_W), row),
                  _weight_spec(wo.shape), _const_spec((1, d)), _weight_spec(wq.shape), _const_spec((1, CA_DH))],
        out_specs=[pl.BlockSpec((tm, d), row), pl.BlockSpec((tm, CA_W), row)],
        out_shape=[jax.ShapeDtypeStruct((m, d), F32), jax.ShapeDtypeStruct((m, CA_W), q_dtype)],
        compiler_params=_cparams("parallel"),
        name="mixout",
    )(x, og, od, wo, nc, wq, qn)


def _memkv_kernel(m_ref, nm_ref, wk_ref, wv_ref, kn_ref, mk_ref, mv_ref):
    mn = _rms(m_ref[...], nm_ref[...]).astype(BF16)
    mk_ref[...] = _head_rms(_dot(mn, wk_ref[...]), kn_ref[...], CA_DH)
    mv_ref[...] = _dot(mn, wv_ref[...])


def _memkv(mem, nm, wk, wv, kn, *, tm):
    m, d = mem.shape
    row = lambda i: (i, 0)
    return pl.pallas_call(
        _memkv_kernel,
        grid=(m // tm,),
        in_specs=[pl.BlockSpec((tm, d), row), _const_spec((1, d)), _const_spec(wk.shape), _const_spec(wv.shape),
                  _const_spec((1, CA_DH))],
        out_specs=[pl.BlockSpec((tm, CA_W), row)] * 2,
        out_shape=[jax.ShapeDtypeStruct((m, CA_W), F32)] * 2,
        compiler_params=_cparams("parallel"),
        name="memkv",
    )(mem, nm, wk, wv, kn)


def _cross_kernel(q_ref, mk_ref, mv_ref, o_ref, *pad_scr, n_valid):
    if pad_scr:
        pad_scr[0][...] = jnp.zeros(pad_scr[0].shape, F32)
        pad_scr[0][0:n_valid, :] = q_ref[0].astype(F32)
        q = pad_scr[0][...].astype(BF16)
    else:
        q = q_ref[...]
    heads_axis = len(mk_ref.shape) == 3 and mk_ref.shape[1] == CA_H
    for h in range(CA_H):
        hs = slice(h * CA_DH, (h + 1) * CA_DH)
        mk = mk_ref[:, h, :] if heads_axis else mk_ref[0, :, hs]
        mv = mv_ref[:, h, :] if heads_axis else mv_ref[0, :, hs]
        s = _dot_nt(q[:, hs], mk.astype(BF16))
        e = jnp.exp(s - jnp.max(s, axis=-1, keepdims=True))
        pr = e / jnp.sum(e, axis=-1, keepdims=True)
        oh = _dot(pr.astype(BF16), mv.astype(BF16))
        if pad_scr:
            o_ref[0, :, hs] = oh[0:n_valid].astype(o_ref.dtype)
        else:
            o_ref[:, hs] = oh.astype(o_ref.dtype)


def _cross(q, mk, mv, *, nb, seq, tq, layer=None):
    if layer is None:
        mem_spec = pl.BlockSpec((1, mk.shape[1], CA_W), lambda b, i: (b, 0, 0))
    else:
        mem_spec = pl.BlockSpec((None, None, mk.shape[2], CA_H, CA_DH), lambda b, i: (layer, b, 0, 0, 0))
    if seq < 8:
        nq = 1
        q_spec = pl.BlockSpec((1, seq, CA_W), lambda b, i: (b, 0, 0))
        out_shape = jax.ShapeDtypeStruct((nb, seq, CA_W), BF16)
        scratch = [pltpu.VMEM((8, CA_W), F32)]
    else:
        nq = seq // tq
        q_spec = pl.BlockSpec((tq, CA_W), lambda b, i: (b * nq + i, 0))
        out_shape = jax.ShapeDtypeStruct((nb * seq, CA_W), BF16)
        scratch = []
    return pl.pallas_call(
        functools.partial(_cross_kernel, n_valid=seq),
        grid=(nb, nq),
        in_specs=[q_spec, mem_spec, mem_spec],
        out_specs=q_spec,
        out_shape=out_shape,
        scratch_shapes=scratch,
        compiler_params=_cparams("parallel", "arbitrary"),
        name="cross_attn",
    )(q, mk, mv)


def _ffn_kernel(x_ref, oc_ref, wco_ref, nf_ref, wg_ref, wu_ref, cw_ref, wd_ref, hist_ref,
                y_ref, hout_ref, gbuf, acc, *, tm, stride, ncol):
    j = pl.program_id(1)
    d_ff = wg_ref.shape[1]
    hrows = (FFN_CONV - 1) * stride
    base = gbuf.shape[0] - tm

    @pl.when(j == 0)
    def _():
        gbuf[base - hrows:base, :] = hist_ref[0]

    @pl.when(j > 0)
    def _():
        gbuf[base - hrows:base, :] = gbuf[base + tm - hrows:base + tm, :]

    x2 = x_ref[...] + _dot(oc_ref[...], wco_ref[...])
    hf = _rms(x2, nf_ref[...]).astype(BF16)
    acc[...] = x2
    for c0 in range(0, d_ff, ncol):
        cs = slice(c0, c0 + ncol)
        gbuf[base:base + tm, cs] = _dot(hf, wg_ref[:, cs])
        w = cw_ref[:, cs]
        gt = gbuf[base - hrows:base - hrows + tm, cs] * w[0:1]
        for t in range(1, FFN_CONV):
            off = base - hrows + t * stride
            gt = gt + gbuf[off:off + tm, cs] * w[t:t + 1]
        act = (_silu(gt) * _dot(hf, wu_ref[:, cs])).astype(BF16)
        acc[...] += _dot(act, wd_ref[cs, :])
    y_ref[...] = acc[...]
    hout_ref[0] = gbuf[base + tm - hrows:base + tm, :]


def _ffn(x, oc, wco, nf, wg, wu, cw, wd, hist, *, ngroups, tm, stride, ncol):
    m, d = x.shape
    d_ff = wg.shape[1]
    nt = m // (ngroups * tm)
    hrows = (FFN_CONV - 1) * stride
    base = -(-hrows // 8) * 8
    row = lambda b, j: (b * nt + j, 0)
    grp = lambda b, j: (b, 0, 0)
    return pl.pallas_call(
        functools.partial(_ffn_kernel, tm=tm, stride=stride, ncol=ncol),
        grid=(ngroups, nt),
        in_specs=[pl.BlockSpec((tm, d), row), pl.BlockSpec((tm, CA_W), row), _weight_spec(wco.shape),
                  _const_spec((1, d)), _weight_spec(wg.shape), _weight_spec(wu.shape),
                  _const_spec((FFN_CONV, d_ff)), _weight_spec(wd.shape),
                  pl.BlockSpec((1, hrows, d_ff), grp)],
        out_specs=[pl.BlockSpec((tm, d), row), pl.BlockSpec((1, hrows, d_ff), grp)],
        out_shape=[jax.ShapeDtypeStruct((m, d), F32), jax.ShapeDtypeStruct((ngroups, hrows, d_ff), F32)],
        scratch_shapes=[pltpu.VMEM((base + tm, d_ff), F32), pltpu.VMEM((tm, d), F32)],
        compiler_params=_cparams("parallel", "arbitrary"),
        name="ffn",
    )(x, oc, wco, nf, wg, wu, cw, wd, hist)


def _rotary_tables(pos):
    half = DIFF_DH // 2
    inv = ROPE_THETA ** (-jnp.arange(half, dtype=F32) / half)
    ang = pos.astype(F32)[:, None] * inv[None, :]
    cos, sin, zero = jnp.cos(ang), jnp.sin(ang), jnp.zeros_like(ang)
    reps = (1, 2 * DIFF_H)
    return (jnp.tile(jnp.concatenate([cos, cos], axis=1), reps),
            jnp.tile(jnp.concatenate([-sin, zero], axis=1), reps),
            jnp.tile(jnp.concatenate([zero, sin], axis=1), reps))


def kernel(x_prompt, x_sample, mem_prompt, cache_k, cache_v, page_table, state_gdn, state_gdn_conv, cache_mem_k, cache_mem_v, state_ffn_conv, norm_mix, w_in, conv_qkv, a_log, dt_bias, gdn_norm, qnorm_diff, knorm_diff, lam_q1, lam_k1, lam_q2, lam_k2, diff_norm, w_out, norm_cross, norm_mem, w_cq, w_ck, w_cv, qnorm_cross, knorm_cross, w_co, norm_ffn, w_gate, w_up, conv_ffn, w_down):
    depth = w_in.shape[0]
    bp, lp, d = x_prompt.shape
    bs, ls, _ = x_sample.shape
    mlen = mem_prompt.shape[1]
    d_ff = w_gate.shape[2]
    past_len = page_table.shape[1] * PAGE
    mp, ms = bp * lp, bs * ls

    cos_p, sa_p, sb_p = _rotary_tables(jnp.arange(lp, dtype=jnp.int32))
    cos_s, sa_s, sb_s = [jnp.tile(t, (bs, 1)) for t in
                         _rotary_tables(past_len + jnp.arange(ls, dtype=jnp.int32))]
    ii = jnp.arange(DIFF_W)
    gmat = jnp.where(ii[:, None] // DIFF_DH == ii[None, :] // DIFF_DH, 1.0 / DIFF_DH, 0.0).astype(BF16)
    row = lambda a: a.reshape(1, -1).astype(F32)
    lane_pad = lambda a, off: jnp.zeros((1, LANE), F32).at[0, off:off + a.shape[0]].set(a.astype(F32))
    c_qkv, c_gate = 3 * GDN_W, 4 * GDN_W
    c_ba = c_gate + 2 * GDN_H

    xp = x_prompt.reshape(mp, d)
    xs = x_sample.reshape(ms, d)
    memp = mem_prompt.reshape(bp * mlen, d)
    zeros_s0 = jnp.zeros((bp, GDN_H, GDN_D, GDN_D), F32)
    zeros_cb = jnp.zeros((bp, GDN_CONV - 1, 3 * GDN_W), F32)
    zeros_fb = jnp.zeros((bp, FFN_CONV - 1, d_ff), F32)

    p_out, s_out = [], []
    for l in range(depth):
        lam_init = 0.8 - 0.6 * math.exp(-0.3 * l)
        wl = w_in[l]
        w_in_l = jnp.concatenate([wl[:, :c_gate], wl[:, c_ba:], wl[:, c_gate:c_ba],
                                  jnp.zeros((d, LANE - 2 * GDN_H), F32)], axis=1).astype(BF16)
        qg = jnp.tile(row(qnorm_diff[l]), (1, 2 * DIFF_H))
        kg = jnp.tile(row(knorm_diff[l]), (1, 2 * DIFF_H))
        lams = (row(lam_q1[l]), row(lam_k1[l]), row(lam_q2[l]), row(lam_k2[l]))
        alog = lane_pad(a_log[l], GDN_H)
        dtb = lane_pad(dt_bias[l], GDN_H)
        w_out_l, w_cq_l, w_co_l = w_out[l].astype(BF16), w_cq[l].astype(BF16), w_co[l].astype(BF16)
        w_g_l, w_u_l, w_d_l = w_gate[l].astype(BF16), w_up[l].astype(BF16), w_down[l].astype(BF16)
        inproj = functools.partial(_inproj, gain=row(norm_mix[l]), w=w_in_l, qg=qg, kg=kg, gmat=gmat)
        gdn = functools.partial(_gdn, cw=conv_qkv[l], alog=alog, dtb=dtb, gn=row(gdn_norm[l]))
        mixout = functools.partial(_mixout, wo=w_out_l, nc=row(norm_cross[l]), wq=w_cq_l, qn=row(qnorm_cross[l]))
        ffn = functools.partial(_ffn, wco=w_co_l, nf=row(norm_ffn[l]), wg=w_g_l, wu=w_u_l, cw=conv_ffn[l],
                                wd=w_d_l)

        mk_p, mv_p = _memkv(memp, row(norm_mem[l]), w_ck[l].astype(BF16), w_cv[l].astype(BF16),
                            row(knorm_cross[l]), tm=512)
        qkv, gate, ba, q, k, v = inproj(xp, cos=cos_p, sa=sa_p, sb=sb_p, tm=512, q_dtype=BF16)
        og, sp = gdn(qkv, gate, ba, s0=zeros_s0, cb=zeros_cb, nb=bp, seq=lp, lt=256)
        od = _flash(q, k, v, lams, row(diff_norm[l]), nb=bp, seq=lp, tq=512, lam_init=lam_init)
        x1, qc = mixout(xp, og, od, tm=512, q_dtype=BF16)
        oc = _cross(qc, mk_p.reshape(bp, mlen, CA_W), mv_p.reshape(bp, mlen, CA_W), nb=bp, seq=lp, tq=512)
        xp, fp = ffn(x1, oc, hist=zeros_fb, ngroups=bp, tm=512, stride=1, ncol=1408)
        p_out.append((k.reshape(bp, lp, 2 * DIFF_H, DIFF_DH), v.reshape(bp, lp, DIFF_H, DIFF_DV), sp,
                      qkv.reshape(bp, lp, c_qkv)[:, lp - (GDN_CONV - 1):],
                      mk_p.reshape(bp, mlen, CA_H, CA_DH), mv_p.reshape(bp, mlen, CA_H, CA_DH), fp))

        qkv, gate, ba, q, k, v = inproj(xs, cos=cos_s, sa=sa_s, sb=sb_s, tm=ms, q_dtype=F32)
        r3 = lambda a: a.reshape(bs, ls, a.shape[-1])
        og, ss = gdn(r3(qkv), r3(gate), r3(ba), s0=state_gdn[l], cb=state_gdn_conv[l], nb=bs, seq=ls,
                     lt=GDN_CHUNK)
        od = _decode(page_table, r3(q), r3(k), r3(v), cache_k, cache_v, lams, row(diff_norm[l]),
                     layer=l, ppg=16, lam_init=lam_init)
        x1, qc = mixout(xs, og.reshape(ms, GDN_W), od.reshape(ms, DIFF_W), tm=ms, q_dtype=F32)
        oc = _cross(r3(qc), cache_mem_k, cache_mem_v, nb=bs, seq=ls, tq=ls, layer=l)
        tmaj = lambda a: a.reshape(bs, ls, -1).transpose(1, 0, 2).reshape(ms, -1)
        hist = state_ffn_conv[l].transpose(1, 0, 2).reshape(1, (FFN_CONV - 1) * bs, d_ff)
        y_t, fs_t = ffn(tmaj(x1), tmaj(oc), hist=hist, ngroups=1, tm=ms, stride=bs, ncol=1408)
        xs = y_t.reshape(ls, bs, d).transpose(1, 0, 2).reshape(ms, d)
        fs = fs_t.reshape(FFN_CONV - 1, bs, d_ff).transpose(1, 0, 2)
        conv_s = jnp.concatenate([state_gdn_conv[l], r3(qkv)], axis=1)[:, ls:]
        s_out.append((k.reshape(bs, ls, 2 * DIFF_H, DIFF_DH), v.reshape(bs, ls, DIFF_H, DIFF_DV), ss, conv_s, fs))

    pk, pv, pg, pc, pmk, pmv, pf = [jnp.stack(t, axis=0) for t in zip(*p_out)]
    sk, sv, sg, sc, sf = [jnp.stack(t, axis=0) for t in zip(*s_out)]
    return (xp.reshape(bp, lp, d), xs.reshape(bs, ls, d), pk, pv, pg, pc, pmk, pmv, pf, sk, sv, sg, sc, sf)
```

```python
import functools
import math

import jax
import jax.numpy as jnp
from jax import lax
from jax.experimental import pallas as pl
from jax.experimental.pallas import tpu as pltpu

F32 = jnp.float32
BF16 = jnp.bfloat16

EPS = 1e-6
ROPE_THETA = 10000.0
GDN_H = 4
GDN_D = 128
GDN_W = GDN_H * GDN_D
GDN_CONV = 4
GDN_CHUNK = 64
DIFF_H = 4
DIFF_DH = 64
DIFF_DV = 128
DIFF_W = DIFF_H * DIFF_DV
CA_H = 4
CA_DH = 128
CA_W = CA_H * CA_DH
FFN_CONV = 3
PAGE = 128
LANE = 128
VMEM_LIMIT = 56 * 1024 * 1024
NOT_SAME_CHUNK = 99.0
LOG2E = math.log2(math.e)


def _cparams(*sem):
    return pltpu.CompilerParams(dimension_semantics=sem, vmem_limit_bytes=VMEM_LIMIT)


def _dot(a, b):
    return jnp.dot(a, b, preferred_element_type=F32)


def _dot_nt(a, b):
    return lax.dot_general(a, b, (((1,), (1,)), ((), ())), preferred_element_type=F32)


def _dot_tn(a, b):
    return lax.dot_general(a, b, (((0,), (0,)), ((), ())), preferred_element_type=F32)


def _sigmoid(x):
    return 1.0 / (1.0 + jnp.exp(-x))


def _silu(x):
    return x * _sigmoid(x)


def _softplus(x):
    return jnp.maximum(x, 0.0) + jnp.log1p(jnp.exp(-jnp.abs(x)))


def _rms(x, gain):
    return x * lax.rsqrt(jnp.mean(x * x, axis=-1, keepdims=True) + EPS) * gain


def _const_spec(shape):
    nd = len(shape)
    return pl.BlockSpec(shape, lambda *_: (0,) * nd)


def _weight_spec(shape, layer=None):
    if layer is None:
        nd = len(shape)
        return pl.BlockSpec(shape, lambda *_: (0,) * nd, pipeline_mode=pl.Buffered(1))
    nd = len(shape) - 1
    return pl.BlockSpec((None,) + tuple(shape[1:]), lambda *_: (layer,) + (0,) * nd, pipeline_mode=pl.Buffered(1))


def _inproj_kernel(x_ref, gain_ref, w_ref, qg_ref, kg_ref, cos_ref, sa_ref, sb_ref, gmat_ref,
                   qkv_ref, gate_ref, ba_ref, q_ref, k_ref, v_ref):
    h = _rms(x_ref[...], gain_ref[...]).astype(BF16)
    c_qkv, c_gate = 3 * GDN_W, 4 * GDN_W
    c_q, c_k, c_v = c_gate + DIFF_W, c_gate + 2 * DIFF_W, c_gate + 3 * DIFF_W

    def proj(a, b):
        return _dot(h, w_ref[:, a:b])

    qkv_ref[...] = proj(0, c_qkv)
    gate_ref[...] = proj(c_qkv, c_gate).astype(gate_ref.dtype)
    v_ref[...] = proj(c_k, c_v)
    ba_ref[...] = proj(c_v, c_v + LANE)
    widen = lambda t: jnp.concatenate([t] * (DIFF_W // LANE), axis=1)
    cos, sa, sb = widen(cos_ref[...]), widen(sa_ref[...]), widen(sb_ref[...])
    half = DIFF_DH // 2

    def norm_rot(z, g):
        ms = _dot((z * z).astype(BF16), gmat_ref[...])
        y = z * lax.rsqrt(ms + EPS) * g
        return y * cos + pltpu.roll(y, DIFF_W - half, 1) * sa + pltpu.roll(y, half, 1) * sb

    q_ref[...] = (norm_rot(proj(c_gate, c_q), qg_ref[...]) * (LOG2E * DIFF_DH ** -0.5)).astype(q_ref.dtype)
    k_ref[...] = norm_rot(proj(c_q, c_k), kg_ref[...])


def _inproj(x, gain, w, qg, kg, cos, sa, sb, gmat, *, tm, q_dtype):
    m, d = x.shape
    npos = cos.shape[0] // tm
    nw = w.shape[1]
    row = lambda i: (i, 0)
    pos = lambda i: (i % npos, 0)
    return pl.pallas_call(
        _inproj_kernel,
        grid=(m // tm,),
        in_specs=[pl.BlockSpec((tm, d), row), _const_spec((1, d)), _weight_spec((d, nw)),
                  _const_spec((1, DIFF_W)), _const_spec((1, DIFF_W)),
                  pl.BlockSpec((tm, LANE), pos), pl.BlockSpec((tm, LANE), pos),
                  pl.BlockSpec((tm, LANE), pos), _const_spec((DIFF_W, DIFF_W))],
        out_specs=[pl.BlockSpec((tm, 3 * GDN_W), row), pl.BlockSpec((tm, GDN_W), row),
                   pl.BlockSpec((tm, LANE), row), pl.BlockSpec((tm, DIFF_W), row),
                   pl.BlockSpec((tm, DIFF_W), row), pl.BlockSpec((tm, DIFF_W), row)],
        out_shape=[jax.ShapeDtypeStruct((m, 3 * GDN_W), F32), jax.ShapeDtypeStruct((m, GDN_W), BF16),
                   jax.ShapeDtypeStruct((m, LANE), F32), jax.ShapeDtypeStruct((m, DIFF_W), q_dtype),
                   jax.ShapeDtypeStruct((m, DIFF_W), F32), jax.ShapeDtypeStruct((m, DIFF_W), F32)],
        compiler_params=_cparams("parallel"),
        name="inproj",
    )(x, gain, w, qg, kg, cos, sa, sb, gmat)


def _gdn_kernel(x_ref, gate_ref, ba_ref, cw_ref, alog_ref, dtb_ref, gn_ref, s0_ref, cb_ref, tri_ref, lv_ref,
                lvl_ref, o_ref, sout_ref, xbuf, s_scr, *pad_scr, lt, n_valid, chunk):
    j = pl.program_id(1)
    padded = n_valid < lt
    tail = GDN_CONV - 1

    @pl.when(j == 0)
    def _():
        for h in range(GDN_H):
            s_scr[h] = s0_ref[0, h].T
        xbuf[0:8, :] = jnp.zeros((8, 3 * GDN_W), F32)
        xbuf[8 - tail:8, :] = cb_ref[0]

    @pl.when(j > 0)
    def _():
        xbuf[0:8, :] = xbuf[lt:lt + 8, :]

    if padded:
        gate_scr, ba_scr = pad_scr
        xbuf[8:8 + lt, :] = jnp.zeros((lt, 3 * GDN_W), F32)
        xbuf[8:8 + n_valid, :] = x_ref[0]
        gate_scr[...] = jnp.zeros(gate_scr.shape, F32)
        gate_scr[0:n_valid, :] = gate_ref[0].astype(F32)
        ba_scr[...] = jnp.zeros(ba_scr.shape, F32)
        ba_scr[0:n_valid, :] = ba_ref[0]
        gate_src, ba = gate_scr, ba_scr[...]
    else:
        xbuf[8:8 + lt, :] = x_ref[...]
        gate_src, ba = gate_ref, ba_ref[...]

    def conv_act(c0):
        w = cw_ref[:, c0:c0 + GDN_D]
        y = xbuf[8 - tail:8 - tail + lt, c0:c0 + GDN_D] * w[0:1]
        for t in range(1, GDN_CONV):
            y = y + xbuf[8 - tail + t:8 - tail + t + lt, c0:c0 + GDN_D] * w[t:t + 1]
        return _silu(y)

    def l2n(z):
        return z * lax.rsqrt(jnp.sum(z * z, axis=-1, keepdims=True) + EPS)

    beta_all = _sigmoid(ba)
    g_all = -jnp.exp(alog_ref[...]) * _softplus(ba + dtb_ref[...])
    if padded:
        valid = lax.broadcasted_iota(jnp.int32, (lt, 1), 0) < n_valid
        beta_all = jnp.where(valid, beta_all, 0.0)
        g_all = jnp.where(valid, g_all, 0.0)
    g_hi = g_all.astype(BF16)
    g_lo = (g_all - g_hi.astype(F32)).astype(BF16)
    gc_all = _dot(tri_ref[...], g_hi) + _dot(tri_ref[...], g_lo)
    gc_rows = gc_all.T

    lv = lv_ref[...]
    bl = lv.shape[0]
    incl = lv < NOT_SAME_CHUNK
    strict = jnp.logical_and(incl, lv >= 0.0)
    eye = jnp.where(lv == -1.0, 1.0, 0.0)
    lev0 = lv == 0.0
    n_levels = int(math.log2(chunk))
    heads = range(GDN_H)
    units = [(h, r) for r in range(0, lt, bl) for h in heads]

    gcol = [gc_all[:, GDN_H + h:GDN_H + h + 1] for h in heads]
    bcol = [beta_all[:, h:h + 1] for h in heads]
    eg = [jnp.exp(gcol[h]) for h in heads]
    q = [l2n(conv_act(h * GDN_D)) * (GDN_D ** -0.5) for h in heads]
    k = [l2n(conv_act(GDN_W + h * GDN_D)) for h in heads]
    rhs = [jnp.concatenate([conv_act(2 * GDN_W + h * GDN_D) * bcol[h], k[h] * (bcol[h] * eg[h])],
                           axis=1).astype(BF16) for h in heads]
    qb = [q[h].astype(BF16) for h in heads]
    kb = [k[h].astype(BF16) for h in heads]
    qg = [q[h] * eg[h] for h in heads]
    mm, aqk = [], []
    for h, r in units:
        decay = jnp.where(incl, jnp.exp(gcol[h][r:r + bl] - gc_rows[GDN_H + h:GDN_H + h + 1, r:r + bl]), 0.0)
        mm.append(jnp.where(strict, bcol[h][r:r + bl] * _dot_nt(kb[h][r:r + bl], kb[h][r:r + bl]) * decay, 0.0))
        aqk.append((_dot_nt(qb[h][r:r + bl], kb[h][r:r + bl]) * decay).astype(BF16))
    x = [eye - jnp.where(lev0, m, 0.0) for m in mm]
    mmb = [m.astype(BF16) for m in mm]
    for lev in range(1, n_levels):
        xb = [xi.astype(BF16) for xi in x]
        y = [_dot(mmb[i] * lvl_ref[lev], xb[i]).astype(BF16) for i in range(len(units))]
        x = [x[i] - _dot(xb[i], y[i]) for i in range(len(units))]
    uwb_u = [_dot(x[i].astype(BF16), rhs[h][r:r + bl]).astype(BF16) for i, (h, r) in enumerate(units)]
    ou_u = [_dot(aqk[i], uwb_u[i]) for i in range(len(units))]
    per_head = lambda vals, h: jnp.concatenate([vals[i] for i, (hh, _) in enumerate(units) if hh == h], axis=0)
    uwb = [per_head(uwb_u, h) for h in heads]
    ou = [per_head(ou_u, h) for h in heads]
    qp = [(qg[h] - ou[h][:, GDN_D:]).astype(BF16) for h in heads]
    for c in range(lt // chunk):
        r0, r1 = c * chunk, (c + 1) * chunk
        for h in heads:
            glast = gcol[h][r1 - 1:r1, :]
            kg = (k[h][r0:r1] * jnp.exp(glast - gcol[h][r0:r1])).astype(BF16)
            ab = _dot_tn(uwb[h][r0:r1, :], kg)
            st = s_scr[h]
            stb = st.astype(BF16)
            o = _dot_nt(qp[h][r0:r1], stb) + ou[h][r0:r1, :GDN_D]
            s_scr[h] = st * jnp.exp(glast) + ab[:GDN_D] - _dot(stb, ab[GDN_D:].astype(BF16))
            gate = gate_src[r0:r1, h * GDN_D:(h + 1) * GDN_D].astype(F32)
            on = _rms(o, gn_ref[...]) * _silu(gate)
            if padded:
                if r0 < n_valid:
                    o_ref[0, :, h * GDN_D:(h + 1) * GDN_D] = on[0:n_valid].astype(o_ref.dtype)
            else:
                o_ref[r0:r1, h * GDN_D:(h + 1) * GDN_D] = on.astype(o_ref.dtype)

    @pl.when(j == pl.num_programs(1) - 1)
    def _():
        for h in range(GDN_H):
            sout_ref[0, h] = s_scr[h].T


def _chunk_level_codes(lt, chunk):
    i = jnp.arange(lt)[:, None]
    j = jnp.arange(lt)[None, :]
    code = jnp.full((lt, lt), NOT_SAME_CHUNK, F32)
    for lev in reversed(range(int(math.log2(chunk)))):
        code = jnp.where((i // (2 << lev)) == (j // (2 << lev)), float(lev), code)
    code = jnp.where(i > j, code, NOT_SAME_CHUNK)
    return jnp.where(i == j, -1.0, code).astype(F32)


def _gdn(qkv, gate, ba, cw, alog, dtb, gn, s0, cb, *, nb, seq, lt):
    padded = seq < lt
    n_valid = seq if padded else lt
    nt = 1 if padded else seq // lt
    chunk = GDN_CHUNK
    i = jnp.arange(lt)
    tri = ((i[:, None] // chunk == i[None, :] // chunk) & (i[:, None] >= i[None, :])).astype(BF16)
    bl = min(lt, 2 * chunk)
    lv = _chunk_level_codes(bl, chunk)
    n_levels = int(math.log2(chunk))
    lvl = jnp.stack([(lv == float(lev)).astype(BF16) for lev in range(n_levels)], axis=0)
    if padded:
        tile = lambda c: pl.BlockSpec((1, seq, c), lambda b, j: (b, 0, 0))
        out_o = jax.ShapeDtypeStruct((nb, seq, GDN_W), BF16)
        scratch = [pltpu.VMEM((lt, GDN_W), F32), pltpu.VMEM((lt, LANE), F32)]
    else:
        tile = lambda c: pl.BlockSpec((lt, c), lambda b, j: (b * nt + j, 0))
        out_o = jax.ShapeDtypeStruct((nb * seq, GDN_W), BF16)
        scratch = []
    return pl.pallas_call(
        functools.partial(_gdn_kernel, lt=lt, n_valid=n_valid, chunk=chunk),
        grid=(nb, nt),
        in_specs=[tile(3 * GDN_W), tile(GDN_W), tile(LANE),
                  _const_spec((GDN_CONV, 3 * GDN_W)), _const_spec((1, LANE)), _const_spec((1, LANE)),
                  _const_spec((1, GDN_D)),
                  pl.BlockSpec((1, GDN_H, GDN_D, GDN_D), lambda b, j: (b, 0, 0, 0)),
                  pl.BlockSpec((1, GDN_CONV - 1, 3 * GDN_W), lambda b, j: (b, 0, 0)),
                  _const_spec((lt, lt)), _const_spec((bl, bl)), _const_spec((n_levels, bl, bl))],
        out_specs=[tile(GDN_W), pl.BlockSpec((1, GDN_H, GDN_D, GDN_D), lambda b, j: (b, 0, 0, 0))],
        out_shape=[out_o, jax.ShapeDtypeStruct((nb, GDN_H, GDN_D, GDN_D), F32)],
        scratch_shapes=[pltpu.VMEM((lt + 8, 3 * GDN_W), F32), pltpu.VMEM((GDN_H, GDN_D, GDN_D), F32)] + scratch,
        compiler_params=_cparams("parallel", "arbitrary"),
        name="gdn",
    )(qkv, gate, ba, cw, alog, dtb, gn, s0, cb, tri, lv, lvl)


def _lambda(l1q, l1k, l2q, l2k, lam_init):
    return (jnp.exp(jnp.sum(l1q[...] * l1k[...], axis=-1, keepdims=True))
            - jnp.exp(jnp.sum(l2q[...] * l2k[...], axis=-1, keepdims=True)) + lam_init)


def _flash_kernel(qi_tab, ki_tab, q_ref, k_ref, v_ref, l1q, l1k, l2q, l2k, dn_ref, o_ref,
                  m_scr, acc_scr, *, tq, tk, lam_init):
    p = pl.program_id(1)
    qi, ki = qi_tab[p], ki_tab[p]

    @pl.when(ki == 0)
    def _():
        m_scr[...] = jnp.full(m_scr.shape, -jnp.inf, F32)
        acc_scr[...] = jnp.zeros(acc_scr.shape, F32)

    lane = lax.broadcasted_iota(jnp.int32, (1, DIFF_DV), 1)
    pairs = range(2 * DIFF_H)
    ones = jnp.ones((acc_scr.shape[1] - DIFF_DV, tk), BF16)

    def step(masked):
        kc, vt = [], []
        for h in range(DIFF_H):
            hs = slice(h * DIFF_DV, (h + 1) * DIFF_DV)
            kh = k_ref[:, hs].astype(BF16)
            vt.append(jnp.concatenate([v_ref[:, hs].T.astype(BF16), ones], axis=0))
            kc += [jnp.where((lane // DIFF_DH) == c, kh, jnp.zeros_like(kh)) for c in range(2)]
        st = [_dot_nt(kc[i], q_ref[:, (i // 2) * DIFF_DV:(i // 2 + 1) * DIFF_DV]) for i in pairs]
        if masked:
            keys = ki * tk + lax.broadcasted_iota(jnp.int32, (tk, tq), 0)
            queries = qi * tq + lax.broadcasted_iota(jnp.int32, (tk, tq), 1)
            visible = keys <= queries
            st = [jnp.where(visible, st[i], -jnp.inf) for i in pairs]
        m_old = [m_scr[i:i + 1, :] for i in pairs]
        m_new = [jnp.maximum(m_old[i], jnp.max(st[i], axis=0, keepdims=True)) for i in pairs]
        pt = [jnp.exp2(st[i] - m_new[i]).astype(BF16) for i in pairs]
        alpha = [jnp.exp2(m_old[i] - m_new[i]) for i in pairs]
        for i in pairs:
            acc_scr[i] = alpha[i] * acc_scr[i] + _dot(vt[i // 2], pt[i])
            m_scr[i:i + 1, :] = m_new[i]

    @pl.when(ki < qi)
    def _():
        step(False)

    @pl.when(ki == qi)
    def _():
        step(True)
        lam = _lambda(l1q, l1k, l2q, l2k, lam_init)
        for h in range(DIFF_H):
            a0, a1 = acc_scr[2 * h], acc_scr[2 * h + 1]
            ot = (a0[:DIFF_DV] / a0[DIFF_DV:DIFF_DV + 1]
                  - lam * (a1[:DIFF_DV] / a1[DIFF_DV:DIFF_DV + 1]))
            o_ref[:, h * DIFF_DV:(h + 1) * DIFF_DV] = (_rms(ot.T, dn_ref[...]) * (1.0 - lam_init)).astype(o_ref.dtype)


def _flash(q, k, v, lams, dn, *, nb, seq, tq, lam_init):
    nq = seq // tq
    pairs = [(a, b) for a in range(nq) for b in range(a + 1)]
    qi_tab = jnp.array([a for a, _ in pairs], jnp.int32)
    ki_tab = jnp.array([b for _, b in pairs], jnp.int32)
    qmap = lambda b, p, qt, kt: (b * nq + qt[p], 0)
    kmap = lambda b, p, qt, kt: (b * nq + kt[p], 0)
    cmap = lambda b, p, qt, kt: (0, 0)
    grid_spec = pltpu.PrefetchScalarGridSpec(
        num_scalar_prefetch=2,
        grid=(nb, len(pairs)),
        in_specs=[pl.BlockSpec((tq, DIFF_W), qmap), pl.BlockSpec((tq, DIFF_W), kmap),
                  pl.BlockSpec((tq, DIFF_W), kmap)]
                 + [pl.BlockSpec((1, DIFF_DH), cmap)] * 4 + [pl.BlockSpec((1, DIFF_DV), cmap)],
        out_specs=pl.BlockSpec((tq, DIFF_W), qmap),
        scratch_shapes=[pltpu.VMEM((2 * DIFF_H, tq), F32),
                        pltpu.VMEM((2 * DIFF_H, DIFF_DV + 16, tq), F32)],
    )
    return pl.pallas_call(
        functools.partial(_flash_kernel, tq=tq, tk=tq, lam_init=lam_init),
        grid_spec=grid_spec,
        out_shape=jax.ShapeDtypeStruct((nb * seq, DIFF_W), BF16),
        compiler_params=_cparams("parallel", "arbitrary"),
        name="diff_flash",
    )(qi_tab, ki_tab, q, k, v, *lams, dn)


def _decode_kernel(pt_ref, q_ref, qmask_ref, kn_ref, vn_ref, expand_ref, hmask_ref, l1q, l1k, l2q, l2k, dn_ref,
                   *rest, n_new, ppg, lam_init):
    kpages, vpages = rest[:ppg], rest[ppg:2 * ppg]
    o_ref = rest[2 * ppg]
    qb_scr, kpad_scr, vpad_scr, m_scr, l_scr, acc_scr = rest[2 * ppg + 1:]
    g = pl.program_id(1)
    nrow = 2 * DIFF_H * 8

    @pl.when(g == 0)
    def _():
        kpad_scr[...] = jnp.zeros(kpad_scr.shape, F32)
        kpad_scr[0:n_new, :] = q_ref[0].astype(F32)
        q8 = kpad_scr[0:8, :]
        qb_scr[...] = (jnp.concatenate([q8] * (2 * DIFF_H), axis=0) * qmask_ref[...]).astype(BF16)
        m_scr[...] = jnp.full(m_scr.shape, -jnp.inf, F32)
        l_scr[...] = jnp.zeros(l_scr.shape, F32)
        acc_scr[...] = jnp.zeros(acc_scr.shape, F32)

    def update(s, vals):
        m_old = m_scr[...]
        m_new = jnp.maximum(m_old, jnp.max(s, axis=-1, keepdims=True))
        alpha = jnp.exp2(m_old - m_new)
        pr = jnp.exp2(s - m_new)
        l_scr[...] = alpha * l_scr[...] + jnp.sum(pr, axis=-1, keepdims=True)
        prb = pr.astype(BF16)
        n = len(vals)
        stacked = prb if n == 1 else jnp.concatenate([prb[:, jj * PAGE:(jj + 1) * PAGE] for jj in range(n)], axis=0)
        spread = _dot(stacked, expand_ref[...])
        pes = [(spread[jj * nrow:(jj + 1) * nrow] * hmask_ref[...]).astype(BF16) for jj in range(n)]
        terms = [_dot(pes[jj], vals[jj]) for jj in range(n)]
        while len(terms) > 1:
            terms = [terms[i] + terms[i + 1] for i in range(0, len(terms) - 1, 2)] + terms[len(terms) & ~1:]
        acc_scr[...] = alpha * acc_scr[...] + terms[0]
        m_scr[...] = m_new

    qb = qb_scr[...]
    s = jnp.concatenate([_dot(qb, kp[...].astype(BF16)) for kp in kpages], axis=1)
    update(s, [vp[...].astype(BF16) for vp in vpages])

    @pl.when(g == pl.num_programs(1) - 1)
    def _():
        kpad_scr[...] = jnp.zeros(kpad_scr.shape, F32)
        kpad_scr[0:n_new, :] = kn_ref[0]
        vpad_scr[...] = jnp.zeros(vpad_scr.shape, F32)
        vpad_scr[0:n_new * DIFF_H, :] = vn_ref[0]
        tok = lax.broadcasted_iota(jnp.int32, (nrow, PAGE), 0) % 8
        key = lax.broadcasted_iota(jnp.int32, (nrow, PAGE), 1)
        ok = jnp.logical_and(key < n_new, key <= tok)
        update(jnp.where(ok, _dot_nt(qb, kpad_scr[...].astype(BF16)), -jnp.inf), [vpad_scr[...].astype(BF16)])
        lam = _lambda(l1q, l1k, l2q, l2k, lam_init)
        on = acc_scr[...] / l_scr[...]
        half = nrow // 2
        for h in range(DIFF_H):
            o = on[8 * h:8 * h + 8, :] - lam * on[half + 8 * h:half + 8 * h + 8, :]
            o_ref[0, :, h * DIFF_DV:(h + 1) * DIFF_DV] = (_rms(o, dn_ref[...]) * (1.0 - lam_init)).astype(o_ref.dtype)


def _decode(page_table, q, kn, vn, cache_k, cache_v, lams, dn, *, layer, ppg, lam_init):
    nb, n_new, _ = q.shape
    n_pages = page_table.shape[1]
    nrow = 2 * DIFF_H * 8
    r = jnp.arange(nrow)
    hc = 2 * ((r // 8) % DIFF_H) + r // (nrow // 2)
    qmask = (jnp.arange(DIFF_W)[None, :] // DIFF_DH == hc[:, None]).astype(F32)
    vrows = PAGE * DIFF_H
    jj = jnp.arange(vrows)
    expand = (jj[None, :] // DIFF_H == jnp.arange(PAGE)[:, None]).astype(BF16)
    hmask = (jj[None, :] % DIFF_H == ((r // 8) % DIFF_H)[:, None]).astype(F32)
    per_b = lambda n, c: pl.BlockSpec((1, n, c), lambda b, g, pt: (b, 0, 0))
    cmap = lambda b, g, pt: (0, 0)

    def page_spec(j, rows, cols):
        return pl.BlockSpec((None, None, rows, cols),
                            lambda b, g, pt: (layer, pt[b * n_pages + g * ppg + j], 0, 0))

    grid_spec = pltpu.PrefetchScalarGridSpec(
        num_scalar_prefetch=1,
        grid=(nb, n_pages // ppg),
        in_specs=[per_b(n_new, DIFF_W), pl.BlockSpec((nrow, DIFF_W), cmap), per_b(n_new, DIFF_W),
                  per_b(n_new * DIFF_H, DIFF_DV), pl.BlockSpec((PAGE, vrows), cmap),
                  pl.BlockSpec((nrow, vrows), cmap)]
                 + [pl.BlockSpec((1, DIFF_DH), cmap)] * 4 + [pl.BlockSpec((1, DIFF_DV), cmap)]
                 + [page_spec(j, DIFF_W, PAGE) for j in range(ppg)]
                 + [page_spec(j, vrows, DIFF_DV) for j in range(ppg)],
        out_specs=pl.BlockSpec((1, 8, DIFF_W), lambda b, g, pt: (b, 0, 0)),
        scratch_shapes=[pltpu.VMEM((nrow, DIFF_W), BF16), pltpu.VMEM((PAGE, DIFF_W), F32),
                        pltpu.VMEM((vrows, DIFF_DV), F32),
                        pltpu.VMEM((nrow, 1), F32), pltpu.VMEM((nrow, 1), F32),
                        pltpu.VMEM((nrow, DIFF_DV), F32)],
    )
    ck = cache_k.transpose(0, 1, 3, 4, 2).reshape(cache_k.shape[0], cache_k.shape[1], DIFF_W, PAGE)
    cv = cache_v.reshape(cache_v.shape[0], cache_v.shape[1], vrows, DIFF_DV)
    out = pl.pallas_call(
        functools.partial(_decode_kernel, n_new=n_new, ppg=ppg, lam_init=lam_init),
        grid_spec=grid_spec,
        out_shape=jax.ShapeDtypeStruct((nb, 8, DIFF_W), BF16),
        compiler_params=_cparams("parallel", "arbitrary"),
        name="diff_decode",
    )(page_table.reshape(-1), q, qmask, kn, vn.reshape(nb, n_new * DIFF_H, DIFF_DV), expand, hmask, *lams, dn,
      *([ck] * ppg), *([cv] * ppg))
    return out[:, :n_new]


def _head_rms(z, gain, width):
    outs = [_rms(z[:, a:a + width], gain) for a in range(0, z.shape[-1], width)]
    return jnp.concatenate(outs, axis=1)


def _mixout_kernel(x_ref, og_ref, od_ref, wo_ref, nc_ref, wq_ref, qn_ref, x1_ref, qc_ref):
    mix = jnp.concatenate([og_ref[...], od_ref[...]], axis=1)
    x1 = x_ref[...] + _dot(mix, wo_ref[...])
    x1_ref[...] = x1
    hc = _rms(x1, nc_ref[...]).astype(BF16)
    qc = _head_rms(_dot(hc, wq_ref[...]), qn_ref[...], CA_DH) * (CA_DH ** -0.5)
    qc_ref[...] = qc.astype(qc_ref.dtype)


def _mixout(x, og, od, wo, nc, wq, qn, *, tm, q_dtype, layer):
    m, d = x.shape
    row = lambda i: (i, 0)
    return pl.pallas_call(
        _mixout_kernel,
        grid=(m // tm,),
        in_specs=[pl.BlockSpec((tm, d), row), pl.BlockSpec((tm, GDN_W), row), pl.BlockSpec((tm, DIFF_W), row),
                  _weight_spec(wo.shape, layer), _const_spec((1, d)), _weight_spec(wq.shape, layer),
                  _const_spec((1, CA_DH))],
        out_specs=[pl.BlockSpec((tm, d), row), pl.BlockSpec((tm, CA_W), row)],
        out_shape=[jax.ShapeDtypeStruct((m, d), F32), jax.ShapeDtypeStruct((m, CA_W), q_dtype)],
        compiler_params=_cparams("parallel"),
        name="mixout",
    )(x, og, od, wo, nc, wq, qn)


def _memkv_kernel(m_ref, nm_ref, wk_ref, wv_ref, kn_ref, mk_ref, mv_ref):
    mn = _rms(m_ref[...], nm_ref[...]).astype(BF16)
    mk_ref[...] = _head_rms(_dot(mn, wk_ref[...]), kn_ref[...], CA_DH)
    mv_ref[...] = _dot(mn, wv_ref[...])


def _memkv(mem, nm, wk, wv, kn, *, tm, layer):
    m, d = mem.shape
    row = lambda i: (i, 0)
    return pl.pallas_call(
        _memkv_kernel,
        grid=(m // tm,),
        in_specs=[pl.BlockSpec((tm, d), row), _const_spec((1, d)), _weight_spec(wk.shape, layer),
                  _weight_spec(wv.shape, layer), _const_spec((1, CA_DH))],
        out_specs=[pl.BlockSpec((tm, CA_W), row)] * 2,
        out_shape=[jax.ShapeDtypeStruct((m, CA_W), F32)] * 2,
        compiler_params=_cparams("parallel"),
        name="memkv",
    )(mem, nm, wk, wv, kn)


def _cross_kernel(q_ref, mk_ref, mv_ref, o_ref, *pad_scr, n_valid):
    if pad_scr:
        pad_scr[0][...] = jnp.zeros(pad_scr[0].shape, F32)
        pad_scr[0][0:n_valid, :] = q_ref[0].astype(F32)
        q = pad_scr[0][...].astype(BF16)
    else:
        q = q_ref[...]
    heads_axis = len(mk_ref.shape) == 3 and mk_ref.shape[1] == CA_H
    for h in range(CA_H):
        hs = slice(h * CA_DH, (h + 1) * CA_DH)
        mk = mk_ref[:, h, :] if heads_axis else mk_ref[0, :, hs]
        mv = mv_ref[:, h, :] if heads_axis else mv_ref[0, :, hs]
        s = _dot_nt(q[:, hs], mk.astype(BF16))
        e = jnp.exp(s - jnp.max(s, axis=-1, keepdims=True))
        pr = e / jnp.sum(e, axis=-1, keepdims=True)
        oh = _dot(pr.astype(BF16), mv.astype(BF16))
        if pad_scr:
            o_ref[0, :, hs] = oh[0:n_valid].astype(o_ref.dtype)
        else:
            o_ref[:, hs] = oh.astype(o_ref.dtype)


def _cross(q, mk, mv, *, nb, seq, tq, layer=None):
    if layer is None:
        mem_spec = pl.BlockSpec((1, mk.shape[1], CA_W), lambda b, i: (b, 0, 0))
    else:
        mem_spec = pl.BlockSpec((None, None, mk.shape[2], CA_H, CA_DH), lambda b, i: (layer, b, 0, 0, 0))
    if seq < 8:
        nq = 1
        q_spec = pl.BlockSpec((1, seq, CA_W), lambda b, i: (b, 0, 0))
        out_shape = jax.ShapeDtypeStruct((nb, seq, CA_W), BF16)
        scratch = [pltpu.VMEM((8, CA_W), F32)]
    else:
        nq = seq // tq
        q_spec = pl.BlockSpec((tq, CA_W), lambda b, i: (b * nq + i, 0))
        out_shape = jax.ShapeDtypeStruct((nb * seq, CA_W), BF16)
        scratch = []
    return pl.pallas_call(
        functools.partial(_cross_kernel, n_valid=seq),
        grid=(nb, nq),
        in_specs=[q_spec, mem_spec, mem_spec],
        out_specs=q_spec,
        out_shape=out_shape,
        scratch_shapes=scratch,
        compiler_params=_cparams("parallel", "arbitrary"),
        name="cross_attn",
    )(q, mk, mv)


def _ffn_kernel(x_ref, oc_ref, wco_ref, nf_ref, wg_ref, wu_ref, cw_ref, wd_ref, hist_ref,
                y_ref, hout_ref, gbuf, acc, *, tm, stride, ncol):
    j = pl.program_id(1)
    d_ff = wg_ref.shape[1]
    hrows = (FFN_CONV - 1) * stride
    base = gbuf.shape[0] - tm

    @pl.when(j == 0)
    def _():
        gbuf[base - hrows:base, :] = hist_ref[0]

    @pl.when(j > 0)
    def _():
        gbuf[base - hrows:base, :] = gbuf[base + tm - hrows:base + tm, :]

    x2 = x_ref[...] + _dot(oc_ref[...], wco_ref[...])
    hf = _rms(x2, nf_ref[...]).astype(BF16)
    acc[...] = x2
    for c0 in range(0, d_ff, ncol):
        cs = slice(c0, c0 + ncol)
        gbuf[base:base + tm, cs] = _dot(hf, wg_ref[:, cs])
        w = cw_ref[:, cs]
        gt = gbuf[base - hrows:base - hrows + tm, cs] * w[0:1]
        for t in range(1, FFN_CONV):
            off = base - hrows + t * stride
            gt = gt + gbuf[off:off + tm, cs] * w[t:t + 1]
        act = (_silu(gt) * _dot(hf, wu_ref[:, cs])).astype(BF16)
        acc[...] += _dot(act, wd_ref[cs, :])
    y_ref[...] = acc[...]
    hout_ref[0] = gbuf[base + tm - hrows:base + tm, :]


def _ffn(x, oc, wco, nf, wg, wu, cw, wd, hist, *, ngroups, tm, stride, ncol, layer):
    m, d = x.shape
    d_ff = wg.shape[-1]
    nt = m // (ngroups * tm)
    hrows = (FFN_CONV - 1) * stride
    base = -(-hrows // 8) * 8
    row = lambda b, j: (b * nt + j, 0)
    grp = lambda b, j: (b, 0, 0)
    return pl.pallas_call(
        functools.partial(_ffn_kernel, tm=tm, stride=stride, ncol=ncol),
        grid=(ngroups, nt),
        in_specs=[pl.BlockSpec((tm, d), row), pl.BlockSpec((tm, CA_W), row), _weight_spec(wco.shape, layer),
                  _const_spec((1, d)), _weight_spec(wg.shape, layer), _weight_spec(wu.shape, layer),
                  _const_spec((FFN_CONV, d_ff)), _weight_spec(wd.shape, layer),
                  pl.BlockSpec((1, hrows, d_ff), grp)],
        out_specs=[pl.BlockSpec((tm, d), row), pl.BlockSpec((1, hrows, d_ff), grp)],
        out_shape=[jax.ShapeDtypeStruct((m, d), F32), jax.ShapeDtypeStruct((ngroups, hrows, d_ff), F32)],
        scratch_shapes=[pltpu.VMEM((base + tm, d_ff), F32), pltpu.VMEM((tm, d), F32)],
        compiler_params=_cparams("parallel", "arbitrary"),
        name="ffn",
    )(x, oc, wco, nf, wg, wu, cw, wd, hist)


def _rotary_tables(pos):
    half = DIFF_DH // 2
    inv = ROPE_THETA ** (-jnp.arange(half, dtype=F32) / half)
    ang = pos.astype(F32)[:, None] * inv[None, :]
    cos, sin, zero = jnp.cos(ang), jnp.sin(ang), jnp.zeros_like(ang)
    reps = (1, LANE // DIFF_DH)
    return (jnp.tile(jnp.concatenate([cos, cos], axis=1), reps),
            jnp.tile(jnp.concatenate([-sin, zero], axis=1), reps),
            jnp.tile(jnp.concatenate([zero, sin], axis=1), reps))


def kernel(x_prompt, x_sample, mem_prompt, cache_k, cache_v, page_table, state_gdn, state_gdn_conv, cache_mem_k, cache_mem_v, state_ffn_conv, norm_mix, w_in, conv_qkv, a_log, dt_bias, gdn_norm, qnorm_diff, knorm_diff, lam_q1, lam_k1, lam_q2, lam_k2, diff_norm, w_out, norm_cross, norm_mem, w_cq, w_ck, w_cv, qnorm_cross, knorm_cross, w_co, norm_ffn, w_gate, w_up, conv_ffn, w_down):
    depth = w_in.shape[0]
    bp, lp, d = x_prompt.shape
    bs, ls, _ = x_sample.shape
    mlen = mem_prompt.shape[1]
    d_ff = w_gate.shape[2]
    past_len = page_table.shape[1] * PAGE
    mp, ms = bp * lp, bs * ls

    cos_p, sa_p, sb_p = _rotary_tables(jnp.arange(lp, dtype=jnp.int32))
    cos_s, sa_s, sb_s = [jnp.tile(t, (bs, 1)) for t in
                         _rotary_tables(past_len + jnp.arange(ls, dtype=jnp.int32))]
    ii = jnp.arange(DIFF_W)
    gmat = jnp.where(ii[:, None] // DIFF_DH == ii[None, :] // DIFF_DH, 1.0 / DIFF_DH, 0.0).astype(BF16)
    row = lambda a: a.reshape(1, -1).astype(F32)
    lane_pad = lambda a, off: jnp.zeros((1, LANE), F32).at[0, off:off + a.shape[0]].set(a.astype(F32))
    c_qkv, c_gate = 3 * GDN_W, 4 * GDN_W
    c_ba = c_gate + 2 * GDN_H

    xp = x_prompt.reshape(mp, d)
    xs = x_sample.reshape(ms, d)
    memp = mem_prompt.reshape(bp * mlen, d)
    zeros_s0 = jnp.zeros((bp, GDN_H, GDN_D, GDN_D), F32)
    zeros_cb = jnp.zeros((bp, GDN_CONV - 1, 3 * GDN_W), F32)
    zeros_fb = jnp.zeros((bp, FFN_CONV - 1, d_ff), F32)

    wb = {name: w.astype(BF16) for name, w in
          dict(out=w_out, cq=w_cq, ck=w_ck, cv=w_cv, co=w_co, gate=w_gate, up=w_up, down=w_down).items()}

    p_out, s_out = [], []
    for l in range(depth):
        lam_init = 0.8 - 0.6 * math.exp(-0.3 * l)
        wl = w_in[l]
        w_in_l = jnp.concatenate([wl[:, :c_gate], wl[:, c_ba:], wl[:, c_gate:c_ba],
                                  jnp.zeros((d, LANE - 2 * GDN_H), F32)], axis=1).astype(BF16)
        qg = jnp.tile(row(qnorm_diff[l]), (1, 2 * DIFF_H))
        kg = jnp.tile(row(knorm_diff[l]), (1, 2 * DIFF_H))
        lams = (row(lam_q1[l]), row(lam_k1[l]), row(lam_q2[l]), row(lam_k2[l]))
        alog = lane_pad(a_log[l], GDN_H)
        dtb = lane_pad(dt_bias[l], GDN_H)
        inproj = functools.partial(_inproj, gain=row(norm_mix[l]), w=w_in_l, qg=qg, kg=kg, gmat=gmat)
        gdn = functools.partial(_gdn, cw=conv_qkv[l], alog=alog, dtb=dtb, gn=row(gdn_norm[l]))
        mixout = functools.partial(_mixout, wo=wb["out"], nc=row(norm_cross[l]), wq=wb["cq"],
                                   qn=row(qnorm_cross[l]), layer=l)
        ffn = functools.partial(_ffn, wco=wb["co"], nf=row(norm_ffn[l]), wg=wb["gate"], wu=wb["up"],
                                cw=conv_ffn[l], wd=wb["down"], layer=l)

        mk_p, mv_p = _memkv(memp, row(norm_mem[l]), wb["ck"], wb["cv"], row(knorm_cross[l]), tm=512, layer=l)
        qkv, gate, ba, q, k, v = inproj(xp, cos=cos_p, sa=sa_p, sb=sb_p, tm=1024, q_dtype=BF16)
        og, sp = gdn(qkv, gate, ba, s0=zeros_s0, cb=zeros_cb, nb=bp, seq=lp, lt=256)
        od = _flash(q, k, v, lams, row(diff_norm[l]), nb=bp, seq=lp, tq=512, lam_init=lam_init)
        x1, qc = mixout(xp, og, od, tm=512, q_dtype=BF16)
        oc = _cross(qc, mk_p.reshape(bp, mlen, CA_W), mv_p.reshape(bp, mlen, CA_W), nb=bp, seq=lp, tq=512)
        xp, fp = ffn(x1, oc, hist=zeros_fb, ngroups=bp, tm=512, stride=1, ncol=1408)
        p_out.append((k.reshape(bp, lp, 2 * DIFF_H, DIFF_DH), v.reshape(bp, lp, DIFF_H, DIFF_DV), sp,
                      qkv.reshape(bp, lp, c_qkv)[:, lp - (GDN_CONV - 1):],
                      mk_p.reshape(bp, mlen, CA_H, CA_DH), mv_p.reshape(bp, mlen, CA_H, CA_DH), fp))

        qkv, gate, ba, q, k, v = inproj(xs, cos=cos_s, sa=sa_s, sb=sb_s, tm=ms, q_dtype=F32)
        r3 = lambda a: a.reshape(bs, ls, a.shape[-1])
        og, ss = gdn(r3(qkv), r3(gate), r3(ba), s0=state_gdn[l], cb=state_gdn_conv[l], nb=bs, seq=ls,
                     lt=GDN_CHUNK)
        od = _decode(page_table, r3(q), r3(k), r3(v), cache_k, cache_v, lams, row(diff_norm[l]),
                     layer=l, ppg=16, lam_init=lam_init)
        x1, qc = mixout(xs, og.reshape(ms, GDN_W), od.reshape(ms, DIFF_W), tm=ms, q_dtype=F32)
        oc = _cross(r3(qc), cache_mem_k, cache_mem_v, nb=bs, seq=ls, tq=ls, layer=l)
        tmaj = lambda a: a.reshape(bs, ls, -1).transpose(1, 0, 2).reshape(ms, -1)
        hist = state_ffn_conv[l].transpose(1, 0, 2).reshape(1, (FFN_CONV - 1) * bs, d_ff)
        y_t, fs_t = ffn(tmaj(x1), tmaj(oc), hist=hist, ngroups=1, tm=ms, stride=bs, ncol=1408)
        xs = y_t.reshape(ls, bs, d).transpose(1, 0, 2).reshape(ms, d)
        fs = fs_t.reshape(FFN_CONV - 1, bs, d_ff).transpose(1, 0, 2)
        conv_s = jnp.concatenate([state_gdn_conv[l], r3(qkv)], axis=1)[:, ls:]
        s_out.append((k.reshape(bs, ls, 2 * DIFF_H, DIFF_DH), v.reshape(bs, ls, DIFF_H, DIFF_DV), ss, conv_s, fs))

    pk, pv, pg, pc, pmk, pmv, pf = [jnp.stack(t, axis=0) for t in zip(*p_out)]
    sk, sv, sg, sc, sf = [jnp.stack(t, axis=0) for t in zip(*s_out)]
    return (xp.reshape(bp, lp, d), xs.reshape(bs, ls, d), pk, pv, pg, pc, pmk, pmv, pf, sk, sv, sg, sc, sf)
```

```python
import functools
import math

import jax
import jax.numpy as jnp
from jax import lax
from jax.experimental import pallas as pl
from jax.experimental.pallas import tpu as pltpu

F32 = jnp.float32
BF16 = jnp.bfloat16

EPS = 1e-6
ROPE_THETA = 10000.0
GDN_H = 4
GDN_D = 128
GDN_W = GDN_H * GDN_D
GDN_CONV = 4
GDN_CHUNK = 64
DIFF_H = 4
DIFF_DH = 64
DIFF_DV = 128
DIFF_W = DIFF_H * DIFF_DV
CA_H = 4
CA_DH = 128
CA_W = CA_H * CA_DH
FFN_CONV = 3
PAGE = 128
LANE = 128
VMEM_LIMIT = 56 * 1024 * 1024
NOT_SAME_CHUNK = 99.0
LOG2E = math.log2(math.e)


def _cparams(*sem):
    return pltpu.CompilerParams(dimension_semantics=sem, vmem_limit_bytes=VMEM_LIMIT)


def _dot(a, b):
    return jnp.dot(a, b, preferred_element_type=F32)


def _dot_nt(a, b):
    return lax.dot_general(a, b, (((1,), (1,)), ((), ())), preferred_element_type=F32)


def _dot_tn(a, b):
    return lax.dot_general(a, b, (((0,), (0,)), ((), ())), preferred_element_type=F32)


def _sigmoid(x):
    return 1.0 / (1.0 + jnp.exp(-x))


def _silu(x):
    return x * _sigmoid(x)


def _softplus(x):
    return jnp.maximum(x, 0.0) + jnp.log1p(jnp.exp(-jnp.abs(x)))


def _rms(x, gain):
    return x * lax.rsqrt(jnp.mean(x * x, axis=-1, keepdims=True) + EPS) * gain


def _const_spec(shape):
    nd = len(shape)
    return pl.BlockSpec(shape, lambda *_: (0,) * nd)


def _weight_spec(shape, layer=None):
    if layer is None:
        nd = len(shape)
        return pl.BlockSpec(shape, lambda *_: (0,) * nd, pipeline_mode=pl.Buffered(1))
    nd = len(shape) - 1
    return pl.BlockSpec((None,) + tuple(shape[1:]), lambda *_: (layer,) + (0,) * nd, pipeline_mode=pl.Buffered(1))


def _inproj_kernel(x_ref, gain_ref, w_ref, qg_ref, kg_ref, cos_ref, sa_ref, sb_ref, gmat_ref,
                   qkv_ref, gate_ref, ba_ref, q_ref, *kv_refs):
    h = _rms(x_ref[...], gain_ref[...]).astype(BF16)
    c_qkv, c_gate = 3 * GDN_W, 4 * GDN_W
    c_q, c_k, c_v = c_gate + DIFF_W, c_gate + 2 * DIFF_W, c_gate + 3 * DIFF_W

    def proj(a, b):
        return _dot(h, w_ref[:, a:b])

    qkv_ref[...] = proj(0, c_qkv)
    gate_ref[...] = proj(c_qkv, c_gate).astype(gate_ref.dtype)
    v = proj(c_k, c_v)
    ba_ref[...] = proj(c_v, c_v + LANE)
    widen = lambda t: jnp.concatenate([t] * (DIFF_W // LANE), axis=1)
    cos, sa, sb = widen(cos_ref[...]), widen(sa_ref[...]), widen(sb_ref[...])
    half = DIFF_DH // 2

    def norm_rot(z, g):
        ms = _dot((z * z).astype(BF16), gmat_ref[...])
        y = z * lax.rsqrt(ms + EPS) * g
        return y * cos + pltpu.roll(y, DIFF_W - half, 1) * sa + pltpu.roll(y, half, 1) * sb

    q_ref[...] = (norm_rot(proj(c_gate, c_q), qg_ref[...]) * (LOG2E * DIFF_DH ** -0.5)).astype(q_ref.dtype)
    k = norm_rot(proj(c_q, c_k), kg_ref[...])
    if len(kv_refs) == 2:
        kv_refs[0][...] = k
        kv_refs[1][...] = v
    else:
        kt_ref, k16_ref, v2_ref, vt16_ref = kv_refs
        kt_ref[...] = k.T
        k16_ref[...] = k.astype(BF16)
        vt16_ref[...] = v.T.astype(BF16)
        for hd in range(DIFF_H):
            v2_ref[pl.ds(hd, v.shape[0], stride=DIFF_H), :] = v[:, hd * DIFF_DV:(hd + 1) * DIFF_DV]


def _inproj(x, gain, w, qg, kg, cos, sa, sb, gmat, *, tm, q_dtype, seq=None):
    m, d = x.shape
    npos = cos.shape[0] // tm
    nw = w.shape[1]
    row = lambda i: (i, 0)
    pos = lambda i: (i % npos, 0)
    if seq is None:
        kv_specs = [pl.BlockSpec((tm, DIFF_W), row)] * 2
        kv_shapes = [jax.ShapeDtypeStruct((m, DIFF_W), F32)] * 2
    else:
        nt, nb = seq // tm, m // seq
        cols = lambda i: (i // nt, 0, i % nt)
        kv_specs = [pl.BlockSpec((None, DIFF_W, tm), cols), pl.BlockSpec((tm, DIFF_W), row),
                    pl.BlockSpec((None, tm * DIFF_H, DIFF_DV), lambda i: (i // nt, i % nt, 0)),
                    pl.BlockSpec((None, DIFF_W, tm), cols)]
        kv_shapes = [jax.ShapeDtypeStruct((nb, DIFF_W, seq), F32), jax.ShapeDtypeStruct((m, DIFF_W), BF16),
                     jax.ShapeDtypeStruct((nb, seq * DIFF_H, DIFF_DV), F32),
                     jax.ShapeDtypeStruct((nb, DIFF_W, seq), BF16)]
    return pl.pallas_call(
        _inproj_kernel,
        grid=(m // tm,),
        in_specs=[pl.BlockSpec((tm, d), row), _const_spec((1, d)), _weight_spec((d, nw)),
                  _const_spec((1, DIFF_W)), _const_spec((1, DIFF_W)),
                  pl.BlockSpec((tm, LANE), pos), pl.BlockSpec((tm, LANE), pos),
                  pl.BlockSpec((tm, LANE), pos), _const_spec((DIFF_W, DIFF_W))],
        out_specs=[pl.BlockSpec((tm, 3 * GDN_W), row), pl.BlockSpec((tm, GDN_W), row),
                   pl.BlockSpec((tm, LANE), row), pl.BlockSpec((tm, DIFF_W), row)] + kv_specs,
        out_shape=[jax.ShapeDtypeStruct((m, 3 * GDN_W), F32), jax.ShapeDtypeStruct((m, GDN_W), BF16),
                   jax.ShapeDtypeStruct((m, LANE), F32), jax.ShapeDtypeStruct((m, DIFF_W), q_dtype)] + kv_shapes,
        compiler_params=_cparams("parallel"),
        name="inproj",
    )(x, gain, w, qg, kg, cos, sa, sb, gmat)


def _gdn_kernel(x_ref, gate_ref, ba_ref, cw_ref, alog_ref, dtb_ref, gn_ref, s0_ref, cb_ref, tri_ref, lv_ref,
                lvl_ref, o_ref, sout_ref, xbuf, s_scr, *pad_scr, lt, n_valid, chunk):
    j = pl.program_id(1)
    padded = n_valid < lt
    tail = GDN_CONV - 1

    @pl.when(j == 0)
    def _():
        for h in range(GDN_H):
            s_scr[h] = s0_ref[0, h].T
        xbuf[0:8, :] = jnp.zeros((8, 3 * GDN_W), F32)
        xbuf[8 - tail:8, :] = cb_ref[0]

    @pl.when(j > 0)
    def _():
        xbuf[0:8, :] = xbuf[lt:lt + 8, :]

    if padded:
        gate_scr, ba_scr = pad_scr
        xbuf[8:8 + lt, :] = jnp.zeros((lt, 3 * GDN_W), F32)
        xbuf[8:8 + n_valid, :] = x_ref[0]
        gate_scr[...] = jnp.zeros(gate_scr.shape, F32)
        gate_scr[0:n_valid, :] = gate_ref[0].astype(F32)
        ba_scr[...] = jnp.zeros(ba_scr.shape, F32)
        ba_scr[0:n_valid, :] = ba_ref[0]
        gate_src, ba = gate_scr, ba_scr[...]
    else:
        xbuf[8:8 + lt, :] = x_ref[...]
        gate_src, ba = gate_ref, ba_ref[...]

    def conv_act(c0):
        w = cw_ref[:, c0:c0 + GDN_D]
        y = xbuf[8 - tail:8 - tail + lt, c0:c0 + GDN_D] * w[0:1]
        for t in range(1, GDN_CONV):
            y = y + xbuf[8 - tail + t:8 - tail + t + lt, c0:c0 + GDN_D] * w[t:t + 1]
        return _silu(y)

    def l2n(z):
        return z * lax.rsqrt(jnp.sum(z * z, axis=-1, keepdims=True) + EPS)

    beta_all = _sigmoid(ba)
    g_all = -jnp.exp(alog_ref[...]) * _softplus(ba + dtb_ref[...])
    if padded:
        valid = lax.broadcasted_iota(jnp.int32, (lt, 1), 0) < n_valid
        beta_all = jnp.where(valid, beta_all, 0.0)
        g_all = jnp.where(valid, g_all, 0.0)
    g_hi = g_all.astype(BF16)
    g_lo = (g_all - g_hi.astype(F32)).astype(BF16)
    gc_all = _dot(tri_ref[...], g_hi) + _dot(tri_ref[...], g_lo)
    gc_rows = gc_all.T

    lv = lv_ref[...]
    bl = lv.shape[0]
    incl = lv < NOT_SAME_CHUNK
    strict = jnp.logical_and(incl, lv >= 0.0)
    eye = jnp.where(lv == -1.0, 1.0, 0.0)
    lev0 = lv == 0.0
    n_levels = int(math.log2(chunk))
    heads = range(GDN_H)
    units = [(h, r) for r in range(0, lt, bl) for h in heads]

    gcol = [gc_all[:, GDN_H + h:GDN_H + h + 1] for h in heads]
    bcol = [beta_all[:, h:h + 1] for h in heads]
    eg = [jnp.exp(gcol[h]) for h in heads]
    q = [l2n(conv_act(h * GDN_D)) * (GDN_D ** -0.5) for h in heads]
    k = [l2n(conv_act(GDN_W + h * GDN_D)) for h in heads]
    rhs = [jnp.concatenate([conv_act(2 * GDN_W + h * GDN_D) * bcol[h], k[h] * (bcol[h] * eg[h])],
                           axis=1).astype(BF16) for h in heads]
    qb = [q[h].astype(BF16) for h in heads]
    kb = [k[h].astype(BF16) for h in heads]
    qg = [q[h] * eg[h] for h in heads]
    mm, aqk = [], []
    for h, r in units:
        decay = jnp.where(incl, jnp.exp(gcol[h][r:r + bl] - gc_rows[GDN_H + h:GDN_H + h + 1, r:r + bl]), 0.0)
        mm.append(jnp.where(strict, bcol[h][r:r + bl] * _dot_nt(kb[h][r:r + bl], kb[h][r:r + bl]) * decay, 0.0))
        aqk.append((_dot_nt(qb[h][r:r + bl], kb[h][r:r + bl]) * decay).astype(BF16))
    x = [eye - jnp.where(lev0, m, 0.0) for m in mm]
    mmb = [m.astype(BF16) for m in mm]
    run_levels = min(n_levels, max(1, math.ceil(math.log2(n_valid)))) if padded else n_levels
    for lev in range(1, run_levels):
        xb = [xi.astype(BF16) for xi in x]
        y = [_dot(mmb[i] * lvl_ref[lev], xb[i]).astype(BF16) for i in range(len(units))]
        x = [x[i] - _dot(xb[i], y[i]) for i in range(len(units))]
    uwb_u = [_dot(x[i].astype(BF16), rhs[h][r:r + bl]).astype(BF16) for i, (h, r) in enumerate(units)]
    ou_u = [_dot(aqk[i], uwb_u[i]) for i in range(len(units))]
    per_head = lambda vals, h: jnp.concatenate([vals[i] for i, (hh, _) in enumerate(units) if hh == h], axis=0)
    uwb = [per_head(uwb_u, h) for h in heads]
    ou = [per_head(ou_u, h) for h in heads]
    qp = [(qg[h] - ou[h][:, GDN_D:]).astype(BF16) for h in heads]
    for c in range(lt // chunk):
        r0, r1 = c * chunk, (c + 1) * chunk
        for h in heads:
            glast = gcol[h][r1 - 1:r1, :]
            kg = (k[h][r0:r1] * jnp.exp(glast - gcol[h][r0:r1])).astype(BF16)
            ab = _dot_tn(uwb[h][r0:r1, :], kg)
            st = s_scr[h]
            stb = st.astype(BF16)
            o = _dot_nt(qp[h][r0:r1], stb) + ou[h][r0:r1, :GDN_D]
            s_scr[h] = st * jnp.exp(glast) + ab[:GDN_D] - _dot(stb, ab[GDN_D:].astype(BF16))
            gate = gate_src[r0:r1, h * GDN_D:(h + 1) * GDN_D].astype(F32)
            on = _rms(o, gn_ref[...]) * _silu(gate)
            if padded:
                if r0 < n_valid:
                    o_ref[0, :, h * GDN_D:(h + 1) * GDN_D] = on[0:n_valid].astype(o_ref.dtype)
            else:
                o_ref[r0:r1, h * GDN_D:(h + 1) * GDN_D] = on.astype(o_ref.dtype)

    @pl.when(j == pl.num_programs(1) - 1)
    def _():
        for h in range(GDN_H):
            sout_ref[0, h] = s_scr[h].T


def _chunk_level_codes(lt, chunk):
    i = jnp.arange(lt)[:, None]
    j = jnp.arange(lt)[None, :]
    code = jnp.full((lt, lt), NOT_SAME_CHUNK, F32)
    for lev in reversed(range(int(math.log2(chunk)))):
        code = jnp.where((i // (2 << lev)) == (j // (2 << lev)), float(lev), code)
    code = jnp.where(i > j, code, NOT_SAME_CHUNK)
    return jnp.where(i == j, -1.0, code).astype(F32)


def _gdn(qkv, gate, ba, cw, alog, dtb, gn, s0, cb, *, nb, seq, lt):
    padded = seq < lt
    n_valid = seq if padded else lt
    nt = 1 if padded else seq // lt
    chunk = GDN_CHUNK
    i = jnp.arange(lt)
    tri = ((i[:, None] // chunk == i[None, :] // chunk) & (i[:, None] >= i[None, :])).astype(BF16)
    bl = min(lt, 2 * chunk)
    lv = _chunk_level_codes(bl, chunk)
    n_levels = int(math.log2(chunk))
    lvl = jnp.stack([(lv == float(lev)).astype(BF16) for lev in range(n_levels)], axis=0)
    if padded:
        tile = lambda c: pl.BlockSpec((1, seq, c), lambda b, j: (b, 0, 0))
        out_o = jax.ShapeDtypeStruct((nb, seq, GDN_W), BF16)
        scratch = [pltpu.VMEM((lt, GDN_W), F32), pltpu.VMEM((lt, LANE), F32)]
    else:
        tile = lambda c: pl.BlockSpec((lt, c), lambda b, j: (b * nt + j, 0))
        out_o = jax.ShapeDtypeStruct((nb * seq, GDN_W), BF16)
        scratch = []
    return pl.pallas_call(
        functools.partial(_gdn_kernel, lt=lt, n_valid=n_valid, chunk=chunk),
        grid=(nb, nt),
        in_specs=[tile(3 * GDN_W), tile(GDN_W), tile(LANE),
                  _const_spec((GDN_CONV, 3 * GDN_W)), _const_spec((1, LANE)), _const_spec((1, LANE)),
                  _const_spec((1, GDN_D)),
                  pl.BlockSpec((1, GDN_H, GDN_D, GDN_D), lambda b, j: (b, 0, 0, 0)),
                  pl.BlockSpec((1, GDN_CONV - 1, 3 * GDN_W), lambda b, j: (b, 0, 0)),
                  _const_spec((lt, lt)), _const_spec((bl, bl)), _const_spec((n_levels, bl, bl))],
        out_specs=[tile(GDN_W), pl.BlockSpec((1, GDN_H, GDN_D, GDN_D), lambda b, j: (b, 0, 0, 0))],
        out_shape=[out_o, jax.ShapeDtypeStruct((nb, GDN_H, GDN_D, GDN_D), F32)],
        scratch_shapes=[pltpu.VMEM((lt + 8, 3 * GDN_W), F32), pltpu.VMEM((GDN_H, GDN_D, GDN_D), F32)] + scratch,
        compiler_params=_cparams("parallel", "arbitrary"),
        name="gdn",
    )(qkv, gate, ba, cw, alog, dtb, gn, s0, cb, tri, lv, lvl)


def _lambda(l1q, l1k, l2q, l2k, lam_init):
    return (jnp.exp(jnp.sum(l1q[...] * l1k[...], axis=-1, keepdims=True))
            - jnp.exp(jnp.sum(l2q[...] * l2k[...], axis=-1, keepdims=True)) + lam_init)


def _flash_kernel(qi_tab, ki_tab, q_ref, k_ref, vt_ref, l1q, l1k, l2q, l2k, dn_ref, o_ref,
                  m_scr, acc_scr, *, tq, tk, lam_init):
    p = pl.program_id(1)
    qi, ki = qi_tab[p], ki_tab[p]

    @pl.when(ki == 0)
    def _():
        m_scr[...] = jnp.full(m_scr.shape, -jnp.inf, F32)
        acc_scr[...] = jnp.zeros(acc_scr.shape, F32)

    lane = lax.broadcasted_iota(jnp.int32, (1, DIFF_DV), 1)
    pairs = range(2 * DIFF_H)
    ones = jnp.ones((acc_scr.shape[1] - DIFF_DV, tk), BF16)

    def step(masked):
        kc, vt = [], []
        for h in range(DIFF_H):
            hs = slice(h * DIFF_DV, (h + 1) * DIFF_DV)
            kh = k_ref[:, hs]
            vt.append(jnp.concatenate([vt_ref[hs, :], ones], axis=0))
            kc += [jnp.where((lane // DIFF_DH) == c, kh, jnp.zeros_like(kh)) for c in range(2)]
        st = [_dot_nt(kc[i], q_ref[:, (i // 2) * DIFF_DV:(i // 2 + 1) * DIFF_DV]) for i in pairs]
        if masked:
            keys = ki * tk + lax.broadcasted_iota(jnp.int32, (tk, tq), 0)
            queries = qi * tq + lax.broadcasted_iota(jnp.int32, (tk, tq), 1)
            visible = keys <= queries
            st = [jnp.where(visible, st[i], -jnp.inf) for i in pairs]
        m_old = [m_scr[i:i + 1, :] for i in pairs]
        m_new = [jnp.maximum(m_old[i], jnp.max(st[i], axis=0, keepdims=True)) for i in pairs]
        pt = [jnp.exp2(st[i] - m_new[i]).astype(BF16) for i in pairs]
        alpha = [jnp.exp2(m_old[i] - m_new[i]) for i in pairs]
        for i in pairs:
            acc_scr[i] = alpha[i] * acc_scr[i] + _dot(vt[i // 2], pt[i])
            m_scr[i:i + 1, :] = m_new[i]

    @pl.when(ki < qi)
    def _():
        step(False)

    @pl.when(ki == qi)
    def _():
        step(True)
        lam = _lambda(l1q, l1k, l2q, l2k, lam_init)
        for h in range(DIFF_H):
            a0, a1 = acc_scr[2 * h], acc_scr[2 * h + 1]
            ot = (a0[:DIFF_DV] / a0[DIFF_DV:DIFF_DV + 1]
                  - lam * (a1[:DIFF_DV] / a1[DIFF_DV:DIFF_DV + 1]))
            o_ref[:, h * DIFF_DV:(h + 1) * DIFF_DV] = (_rms(ot.T, dn_ref[...]) * (1.0 - lam_init)).astype(o_ref.dtype)


def _flash(q, k, v, lams, dn, *, nb, seq, tq, lam_init):
    nq = seq // tq
    pairs = [(a, b) for a in range(nq) for b in range(a + 1)]
    qi_tab = jnp.array([a for a, _ in pairs], jnp.int32)
    ki_tab = jnp.array([b for _, b in pairs], jnp.int32)
    qmap = lambda b, p, qt, kt: (b * nq + qt[p], 0)
    kmap = lambda b, p, qt, kt: (b * nq + kt[p], 0)
    cmap = lambda b, p, qt, kt: (0, 0)
    grid_spec = pltpu.PrefetchScalarGridSpec(
        num_scalar_prefetch=2,
        grid=(nb, len(pairs)),
        in_specs=[pl.BlockSpec((tq, DIFF_W), qmap), pl.BlockSpec((tq, DIFF_W), kmap),
                  pl.BlockSpec((None, DIFF_W, tq), lambda b, p, qt, kt: (b, 0, kt[p]))]
                 + [pl.BlockSpec((1, DIFF_DH), cmap)] * 4 + [pl.BlockSpec((1, DIFF_DV), cmap)],
        out_specs=pl.BlockSpec((tq, DIFF_W), qmap),
        scratch_shapes=[pltpu.VMEM((2 * DIFF_H, tq), F32),
                        pltpu.VMEM((2 * DIFF_H, DIFF_DV + 16, tq), F32)],
    )
    return pl.pallas_call(
        functools.partial(_flash_kernel, tq=tq, tk=tq, lam_init=lam_init),
        grid_spec=grid_spec,
        out_shape=jax.ShapeDtypeStruct((nb * seq, DIFF_W), BF16),
        compiler_params=_cparams("parallel", "arbitrary"),
        name="diff_flash",
    )(qi_tab, ki_tab, q, k, v, *lams, dn)


def _decode_kernel(pt_ref, q_ref, qmask_ref, kn_ref, vn_ref, expand_ref, hmask_ref, l1q, l1k, l2q, l2k, dn_ref,
                   *rest, n_new, ppg, lam_init):
    kpages, vpages = rest[:ppg], rest[ppg:2 * ppg]
    o_ref = rest[2 * ppg]
    qb_scr, kpad_scr, vpad_scr, m_scr, l_scr, acc_scr = rest[2 * ppg + 1:]
    g = pl.program_id(1)
    nrow = 2 * DIFF_H * 8

    @pl.when(g == 0)
    def _():
        kpad_scr[...] = jnp.zeros(kpad_scr.shape, F32)
        kpad_scr[0:n_new, :] = q_ref[0].astype(F32)
        q8 = kpad_scr[0:8, :]
        qb_scr[...] = (jnp.concatenate([q8] * (2 * DIFF_H), axis=0) * qmask_ref[...]).astype(BF16)
        m_scr[...] = jnp.full(m_scr.shape, -jnp.inf, F32)
        l_scr[...] = jnp.zeros(l_scr.shape, F32)
        acc_scr[...] = jnp.zeros(acc_scr.shape, F32)

    def update(s, vals):
        m_old = m_scr[...]
        m_new = jnp.maximum(m_old, jnp.max(s, axis=-1, keepdims=True))
        alpha = jnp.exp2(m_old - m_new)
        pr = jnp.exp2(s - m_new)
        l_scr[...] = alpha * l_scr[...] + jnp.sum(pr, axis=-1, keepdims=True)
        prb = pr.astype(BF16)
        n = len(vals)
        stacked = prb if n == 1 else jnp.concatenate([prb[:, jj * PAGE:(jj + 1) * PAGE] for jj in range(n)], axis=0)
        spread = _dot(stacked, expand_ref[...])
        pes = [(spread[jj * nrow:(jj + 1) * nrow] * hmask_ref[...]).astype(BF16) for jj in range(n)]
        terms = [_dot(pes[jj], vals[jj]) for jj in range(n)]
        while len(terms) > 1:
            terms = [terms[i] + terms[i + 1] for i in range(0, len(terms) - 1, 2)] + terms[len(terms) & ~1:]
        acc_scr[...] = alpha * acc_scr[...] + terms[0]
        m_scr[...] = m_new

    qb = qb_scr[...]
    s = jnp.concatenate([_dot(qb, kp[...].astype(BF16)) for kp in kpages], axis=1)
    update(s, [vp[...].astype(BF16) for vp in vpages])

    @pl.when(g == pl.num_programs(1) - 1)
    def _():
        kpad_scr[...] = jnp.zeros(kpad_scr.shape, F32)
        kpad_scr[0:n_new, :] = kn_ref[0]
        vpad_scr[...] = jnp.zeros(vpad_scr.shape, F32)
        vpad_scr[0:n_new * DIFF_H, :] = vn_ref[0]
        tok = lax.broadcasted_iota(jnp.int32, (nrow, PAGE), 0) % 8
        key = lax.broadcasted_iota(jnp.int32, (nrow, PAGE), 1)
        ok = jnp.logical_and(key < n_new, key <= tok)
        update(jnp.where(ok, _dot_nt(qb, kpad_scr[...].astype(BF16)), -jnp.inf), [vpad_scr[...].astype(BF16)])
        lam = _lambda(l1q, l1k, l2q, l2k, lam_init)
        on = acc_scr[...] / l_scr[...]
        half = nrow // 2
        for h in range(DIFF_H):
            o = on[8 * h:8 * h + 8, :] - lam * on[half + 8 * h:half + 8 * h + 8, :]
            o_ref[0, :, h * DIFF_DV:(h + 1) * DIFF_DV] = (_rms(o, dn_ref[...]) * (1.0 - lam_init)).astype(o_ref.dtype)


def _decode(page_table, q, kn, vn, cache_k, cache_v, lams, dn, *, layer, ppg, lam_init):
    nb, n_new, _ = q.shape
    n_pages = page_table.shape[1]
    nrow = 2 * DIFF_H * 8
    r = jnp.arange(nrow)
    hc = 2 * ((r // 8) % DIFF_H) + r // (nrow // 2)
    qmask = (jnp.arange(DIFF_W)[None, :] // DIFF_DH == hc[:, None]).astype(F32)
    vrows = PAGE * DIFF_H
    jj = jnp.arange(vrows)
    expand = (jj[None, :] // DIFF_H == jnp.arange(PAGE)[:, None]).astype(BF16)
    hmask = (jj[None, :] % DIFF_H == ((r // 8) % DIFF_H)[:, None]).astype(F32)
    per_b = lambda n, c: pl.BlockSpec((1, n, c), lambda b, g, pt: (b, 0, 0))
    cmap = lambda b, g, pt: (0, 0)

    def page_spec(j, rows, cols):
        return pl.BlockSpec((None, None, rows, cols),
                            lambda b, g, pt: (layer, pt[b * n_pages + g * ppg + j], 0, 0))

    grid_spec = pltpu.PrefetchScalarGridSpec(
        num_scalar_prefetch=1,
        grid=(nb, n_pages // ppg),
        in_specs=[per_b(n_new, DIFF_W), pl.BlockSpec((nrow, DIFF_W), cmap), per_b(n_new, DIFF_W),
                  per_b(n_new * DIFF_H, DIFF_DV), pl.BlockSpec((PAGE, vrows), cmap),
                  pl.BlockSpec((nrow, vrows), cmap)]
                 + [pl.BlockSpec((1, DIFF_DH), cmap)] * 4 + [pl.BlockSpec((1, DIFF_DV), cmap)]
                 + [page_spec(j, DIFF_W, PAGE) for j in range(ppg)]
                 + [page_spec(j, vrows, DIFF_DV) for j in range(ppg)],
        out_specs=pl.BlockSpec((1, 8, DIFF_W), lambda b, g, pt: (b, 0, 0)),
        scratch_shapes=[pltpu.VMEM((nrow, DIFF_W), BF16), pltpu.VMEM((PAGE, DIFF_W), F32),
                        pltpu.VMEM((vrows, DIFF_DV), F32),
                        pltpu.VMEM((nrow, 1), F32), pltpu.VMEM((nrow, 1), F32),
                        pltpu.VMEM((nrow, DIFF_DV), F32)],
    )
    ck = cache_k.transpose(0, 1, 3, 4, 2).reshape(cache_k.shape[0], cache_k.shape[1], DIFF_W, PAGE)
    cv = cache_v.reshape(cache_v.shape[0], cache_v.shape[1], vrows, DIFF_DV)
    out = pl.pallas_call(
        functools.partial(_decode_kernel, n_new=n_new, ppg=ppg, lam_init=lam_init),
        grid_spec=grid_spec,
        out_shape=jax.ShapeDtypeStruct((nb, 8, DIFF_W), BF16),
        compiler_params=_cparams("parallel", "arbitrary"),
        name="diff_decode",
    )(page_table.reshape(-1), q, qmask, kn, vn.reshape(nb, n_new * DIFF_H, DIFF_DV), expand, hmask, *lams, dn,
      *([ck] * ppg), *([cv] * ppg))
    return out[:, :n_new]


def _head_rms(z, gain, width):
    outs = [_rms(z[:, a:a + width], gain) for a in range(0, z.shape[-1], width)]
    return jnp.concatenate(outs, axis=1)


def _mixout_kernel(x_ref, og_ref, od_ref, wo_ref, nc_ref, wq_ref, qn_ref, x1_ref, qc_ref):
    mix = jnp.concatenate([og_ref[...], od_ref[...]], axis=1)
    x1 = x_ref[...] + _dot(mix, wo_ref[...])
    x1_ref[...] = x1
    hc = _rms(x1, nc_ref[...]).astype(BF16)
    qc = _head_rms(_dot(hc, wq_ref[...]), qn_ref[...], CA_DH) * (CA_DH ** -0.5)
    qc_ref[...] = qc.astype(qc_ref.dtype)


def _mixout(x, og, od, wo, nc, wq, qn, *, tm, q_dtype, layer):
    m, d = x.shape
    row = lambda i: (i, 0)
    return pl.pallas_call(
        _mixout_kernel,
        grid=(m // tm,),
        in_specs=[pl.BlockSpec((tm, d), row), pl.BlockSpec((tm, GDN_W), row), pl.BlockSpec((tm, DIFF_W), row),
                  _weight_spec(wo.shape, layer), _const_spec((1, d)), _weight_spec(wq.shape, layer),
                  _const_spec((1, CA_DH))],
        out_specs=[pl.BlockSpec((tm, d), row), pl.BlockSpec((tm, CA_W), row)],
        out_shape=[jax.ShapeDtypeStruct((m, d), F32), jax.ShapeDtypeStruct((m, CA_W), q_dtype)],
        compiler_params=_cparams("parallel"),
        name="mixout",
    )(x, og, od, wo, nc, wq, qn)


def _memkv_kernel(m_ref, nm_ref, wk_ref, wv_ref, kn_ref, mk_ref, mv_ref):
    mn = _rms(m_ref[...], nm_ref[...]).astype(BF16)
    mk_ref[...] = _head_rms(_dot(mn, wk_ref[...]), kn_ref[...], CA_DH)
    mv_ref[...] = _dot(mn, wv_ref[...])


def _memkv(mem, nm, wk, wv, kn, *, tm, layer):
    m, d = mem.shape
    row = lambda i: (i, 0)
    return pl.pallas_call(
        _memkv_kernel,
        grid=(m // tm,),
        in_specs=[pl.BlockSpec((tm, d), row), _const_spec((1, d)), _weight_spec(wk.shape, layer),
                  _weight_spec(wv.shape, layer), _const_spec((1, CA_DH))],
        out_specs=[pl.BlockSpec((tm, CA_W), row)] * 2,
        out_shape=[jax.ShapeDtypeStruct((m, CA_W), F32)] * 2,
        compiler_params=_cparams("parallel"),
        name="memkv",
    )(mem, nm, wk, wv, kn)


def _cross_kernel(q_ref, mk_ref, mv_ref, o_ref, *pad_scr, n_valid):
    decode = len(q_ref.shape) == 3
    for g in range(q_ref.shape[0] if decode else 1):
        if decode:
            pad_scr[0][...] = jnp.zeros(pad_scr[0].shape, F32)
            pad_scr[0][0:n_valid, :] = q_ref[g].astype(F32)
            q = pad_scr[0][...].astype(BF16)
        else:
            q = q_ref[...]
        for h in range(CA_H):
            hs = slice(h * CA_DH, (h + 1) * CA_DH)
            mk = mk_ref[g, :, h, :] if decode else mk_ref[0, :, hs]
            mv = mv_ref[g, :, h, :] if decode else mv_ref[0, :, hs]
            s = _dot_nt(q[:, hs], mk.astype(BF16))
            e = jnp.exp(s - jnp.max(s, axis=-1, keepdims=True))
            pr = e / jnp.sum(e, axis=-1, keepdims=True)
            oh = _dot(pr.astype(BF16), mv.astype(BF16))
            if decode:
                o_ref[g, :, hs] = oh[0:n_valid].astype(o_ref.dtype)
            else:
                o_ref[:, hs] = oh.astype(o_ref.dtype)


def _cross(q, mk, mv, *, nb, seq, tq, layer=None, group=1):
    if seq < 8:
        grid = (nb // group, 1)
        mem_spec = pl.BlockSpec((None, group, mk.shape[2], CA_H, CA_DH), lambda b, i: (layer, b, 0, 0, 0))
        q_spec = pl.BlockSpec((group, seq, CA_W), lambda b, i: (b, 0, 0))
        out_shape = jax.ShapeDtypeStruct((nb, seq, CA_W), BF16)
        scratch = [pltpu.VMEM((8, CA_W), F32)]
    else:
        nq = seq // tq
        grid = (nb, nq)
        mem_spec = pl.BlockSpec((1, mk.shape[1], CA_W), lambda b, i: (b, 0, 0))
        q_spec = pl.BlockSpec((tq, CA_W), lambda b, i: (b * nq + i, 0))
        out_shape = jax.ShapeDtypeStruct((nb * seq, CA_W), BF16)
        scratch = []
    return pl.pallas_call(
        functools.partial(_cross_kernel, n_valid=seq),
        grid=grid,
        in_specs=[q_spec, mem_spec, mem_spec],
        out_specs=q_spec,
        out_shape=out_shape,
        scratch_shapes=scratch,
        compiler_params=_cparams("parallel", "arbitrary"),
        name="cross_attn",
    )(q, mk, mv)


def _ffn_kernel(x_ref, oc_ref, wco_ref, nf_ref, wg_ref, wu_ref, cw_ref, wd_ref, hist_ref,
                y_ref, hout_ref, gbuf, acc, *, tm, stride, ncol):
    j = pl.program_id(1)
    d_ff = wg_ref.shape[1]
    hrows = (FFN_CONV - 1) * stride
    base = gbuf.shape[0] - tm

    @pl.when(j == 0)
    def _():
        gbuf[base - hrows:base, :] = hist_ref[0]

    @pl.when(j > 0)
    def _():
        gbuf[base - hrows:base, :] = gbuf[base + tm - hrows:base + tm, :]

    x2 = x_ref[...] + _dot(oc_ref[...], wco_ref[...])
    hf = _rms(x2, nf_ref[...]).astype(BF16)
    acc[...] = x2
    for c0 in range(0, d_ff, ncol):
        cs = slice(c0, c0 + ncol)
        gbuf[base:base + tm, cs] = _dot(hf, wg_ref[:, cs])
        w = cw_ref[:, cs]
        gt = gbuf[base - hrows:base - hrows + tm, cs] * w[0:1]
        for t in range(1, FFN_CONV):
            off = base - hrows + t * stride
            gt = gt + gbuf[off:off + tm, cs] * w[t:t + 1]
        act = (_silu(gt) * _dot(hf, wu_ref[:, cs])).astype(BF16)
        acc[...] += _dot(act, wd_ref[cs, :])
    y_ref[...] = acc[...]
    hout_ref[0] = gbuf[base + tm - hrows:base + tm, :]


def _ffn(x, oc, wco, nf, wg, wu, cw, wd, hist, *, ngroups, tm, stride, ncol, layer):
    m, d = x.shape
    d_ff = wg.shape[-1]
    nt = m // (ngroups * tm)
    hrows = (FFN_CONV - 1) * stride
    base = -(-hrows // 8) * 8
    row = lambda b, j: (b * nt + j, 0)
    grp = lambda b, j: (b, 0, 0)
    return pl.pallas_call(
        functools.partial(_ffn_kernel, tm=tm, stride=stride, ncol=ncol),
        grid=(ngroups, nt),
        in_specs=[pl.BlockSpec((tm, d), row), pl.BlockSpec((tm, CA_W), row), _weight_spec(wco.shape, layer),
                  _const_spec((1, d)), _weight_spec(wg.shape, layer), _weight_spec(wu.shape, layer),
                  _const_spec((FFN_CONV, d_ff)), _weight_spec(wd.shape, layer),
                  pl.BlockSpec((1, hrows, d_ff), grp)],
        out_specs=[pl.BlockSpec((tm, d), row), pl.BlockSpec((1, hrows, d_ff), grp)],
        out_shape=[jax.ShapeDtypeStruct((m, d), F32), jax.ShapeDtypeStruct((ngroups, hrows, d_ff), F32)],
        scratch_shapes=[pltpu.VMEM((base + tm, d_ff), F32), pltpu.VMEM((tm, d), F32)],
        compiler_params=_cparams("parallel", "arbitrary"),
        name="ffn",
    )(x, oc, wco, nf, wg, wu, cw, wd, hist)


def _rotary_tables(pos):
    half = DIFF_DH // 2
    inv = ROPE_THETA ** (-jnp.arange(half, dtype=F32) / half)
    ang = pos.astype(F32)[:, None] * inv[None, :]
    cos, sin, zero = jnp.cos(ang), jnp.sin(ang), jnp.zeros_like(ang)
    reps = (1, LANE // DIFF_DH)
    return (jnp.tile(jnp.concatenate([cos, cos], axis=1), reps),
            jnp.tile(jnp.concatenate([-sin, zero], axis=1), reps),
            jnp.tile(jnp.concatenate([zero, sin], axis=1), reps))


def kernel(x_prompt, x_sample, mem_prompt, cache_k, cache_v, page_table, state_gdn, state_gdn_conv, cache_mem_k, cache_mem_v, state_ffn_conv, norm_mix, w_in, conv_qkv, a_log, dt_bias, gdn_norm, qnorm_diff, knorm_diff, lam_q1, lam_k1, lam_q2, lam_k2, diff_norm, w_out, norm_cross, norm_mem, w_cq, w_ck, w_cv, qnorm_cross, knorm_cross, w_co, norm_ffn, w_gate, w_up, conv_ffn, w_down):
    depth = w_in.shape[0]
    bp, lp, d = x_prompt.shape
    bs, ls, _ = x_sample.shape
    mlen = mem_prompt.shape[1]
    d_ff = w_gate.shape[2]
    past_len = page_table.shape[1] * PAGE
    mp, ms = bp * lp, bs * ls

    cos_p, sa_p, sb_p = _rotary_tables(jnp.arange(lp, dtype=jnp.int32))
    cos_s, sa_s, sb_s = [jnp.tile(t, (bs, 1)) for t in
                         _rotary_tables(past_len + jnp.arange(ls, dtype=jnp.int32))]
    ii = jnp.arange(DIFF_W)
    gmat = jnp.where(ii[:, None] // DIFF_DH == ii[None, :] // DIFF_DH, 1.0 / DIFF_DH, 0.0).astype(BF16)
    row = lambda a: a.reshape(1, -1).astype(F32)
    lane_pad = lambda a, off: jnp.zeros((1, LANE), F32).at[0, off:off + a.shape[0]].set(a.astype(F32))
    c_qkv, c_gate = 3 * GDN_W, 4 * GDN_W
    c_ba = c_gate + 2 * GDN_H

    xp = x_prompt.reshape(mp, d)
    xs = x_sample.reshape(ms, d)
    memp = mem_prompt.reshape(bp * mlen, d)
    zeros_s0 = jnp.zeros((bp, GDN_H, GDN_D, GDN_D), F32)
    zeros_cb = jnp.zeros((bp, GDN_CONV - 1, 3 * GDN_W), F32)
    zeros_fb = jnp.zeros((bp, FFN_CONV - 1, d_ff), F32)

    wb = {name: w.astype(BF16) for name, w in
          dict(out=w_out, cq=w_cq, ck=w_ck, cv=w_cv, co=w_co, gate=w_gate, up=w_up, down=w_down).items()}

    p_out, s_out = [], []
    for l in range(depth):
        lam_init = 0.8 - 0.6 * math.exp(-0.3 * l)
        wl = w_in[l]
        w_in_l = jnp.concatenate([wl[:, :c_gate], wl[:, c_ba:], wl[:, c_gate:c_ba],
                                  jnp.zeros((d, LANE - 2 * GDN_H), F32)], axis=1).astype(BF16)
        qg = jnp.tile(row(qnorm_diff[l]), (1, 2 * DIFF_H))
        kg = jnp.tile(row(knorm_diff[l]), (1, 2 * DIFF_H))
        lams = (row(lam_q1[l]), row(lam_k1[l]), row(lam_q2[l]), row(lam_k2[l]))
        alog = lane_pad(a_log[l], GDN_H)
        dtb = lane_pad(dt_bias[l], GDN_H)
        inproj = functools.partial(_inproj, gain=row(norm_mix[l]), w=w_in_l, qg=qg, kg=kg, gmat=gmat)
        gdn = functools.partial(_gdn, cw=conv_qkv[l], alog=alog, dtb=dtb, gn=row(gdn_norm[l]))
        mixout = functools.partial(_mixout, wo=wb["out"], nc=row(norm_cross[l]), wq=wb["cq"],
                                   qn=row(qnorm_cross[l]), layer=l)
        ffn = functools.partial(_ffn, wco=wb["co"], nf=row(norm_ffn[l]), wg=wb["gate"], wu=wb["up"],
                                cw=conv_ffn[l], wd=wb["down"], layer=l)

        mk_p, mv_p = _memkv(memp, row(norm_mem[l]), wb["ck"], wb["cv"], row(knorm_cross[l]), tm=512, layer=l)
        qkv, gate, ba, q, kt, k16, v2, vt16 = inproj(xp, cos=cos_p, sa=sa_p, sb=sb_p, tm=1024, q_dtype=BF16,
                                                     seq=lp)
        og, sp = gdn(qkv, gate, ba, s0=zeros_s0, cb=zeros_cb, nb=bp, seq=lp, lt=256)
        od = _flash(q, k16, vt16, lams, row(diff_norm[l]), nb=bp, seq=lp, tq=512, lam_init=lam_init)
        k = kt.reshape(bp, 2 * DIFF_H, DIFF_DH, lp).transpose(0, 3, 1, 2)
        v = v2.reshape(bp, lp, DIFF_H, DIFF_DV)
        x1, qc = mixout(xp, og, od, tm=512, q_dtype=BF16)
        oc = _cross(qc, mk_p.reshape(bp, mlen, CA_W), mv_p.reshape(bp, mlen, CA_W), nb=bp, seq=lp, tq=512)
        xp, fp = ffn(x1, oc, hist=zeros_fb, ngroups=bp, tm=512, stride=1, ncol=1408)
        p_out.append((k, v, sp,
                      qkv.reshape(bp, lp, c_qkv)[:, lp - (GDN_CONV - 1):],
                      mk_p.reshape(bp, mlen, CA_H, CA_DH), mv_p.reshape(bp, mlen, CA_H, CA_DH), fp))

        qkv, gate, ba, q, k, v = inproj(xs, cos=cos_s, sa=sa_s, sb=sb_s, tm=ms, q_dtype=F32)
        r3 = lambda a: a.reshape(bs, ls, a.shape[-1])
        og, ss = gdn(r3(qkv), r3(gate), r3(ba), s0=state_gdn[l], cb=state_gdn_conv[l], nb=bs, seq=ls,
                     lt=GDN_CHUNK)
        od = _decode(page_table, r3(q), r3(k), r3(v), cache_k, cache_v, lams, row(diff_norm[l]),
                     layer=l, ppg=16, lam_init=lam_init)
        x1, qc = mixout(xs, og.reshape(ms, GDN_W), od.reshape(ms, DIFF_W), tm=ms, q_dtype=F32)
        oc = _cross(r3(qc), cache_mem_k, cache_mem_v, nb=bs, seq=ls, tq=ls, layer=l, group=math.gcd(bs, 4))
        tmaj = lambda a: a.reshape(bs, ls, -1).transpose(1, 0, 2).reshape(ms, -1)
        hist = state_ffn_conv[l].transpose(1, 0, 2).reshape(1, (FFN_CONV - 1) * bs, d_ff)
        y_t, fs_t = ffn(tmaj(x1), tmaj(oc), hist=hist, ngroups=1, tm=ms, stride=bs, ncol=1408)
        xs = y_t.reshape(ls, bs, d).transpose(1, 0, 2).reshape(ms, d)
        fs = fs_t.reshape(FFN_CONV - 1, bs, d_ff).transpose(1, 0, 2)
        conv_s = jnp.concatenate([state_gdn_conv[l], r3(qkv)], axis=1)[:, ls:]
        s_out.append((k.reshape(bs, ls, 2 * DIFF_H, DIFF_DH), v.reshape(bs, ls, DIFF_H, DIFF_DV), ss, conv_s, fs))

    pk, pv, pg, pc, pmk, pmv, pf = [jnp.stack(t, axis=0) for t in zip(*p_out)]
    sk, sv, sg, sc, sf = [jnp.stack(t, axis=0) for t in zip(*s_out)]
    return (xp.reshape(bp, lp, d), xs.reshape(bs, ls, d), pk, pv, pg, pc, pmk, pmv, pf, sk, sv, sg, sc, sf)
```

```python
import functools
import math

import jax
import jax.numpy as jnp
from jax import lax
from jax.experimental import pallas as pl
from jax.experimental.pallas import tpu as pltpu

F32 = jnp.float32
BF16 = jnp.bfloat16

EPS = 1e-6
ROPE_THETA = 10000.0
GDN_H = 4
GDN_D = 128
GDN_W = GDN_H * GDN_D
GDN_CONV = 4
GDN_CHUNK = 64
DIFF_H = 4
DIFF_DH = 64
DIFF_DV = 128
DIFF_W = DIFF_H * DIFF_DV
CA_H = 4
CA_DH = 128
CA_W = CA_H * CA_DH
FFN_CONV = 3
PAGE = 128
LANE = 128
VMEM_LIMIT = 56 * 1024 * 1024
NOT_SAME_CHUNK = 99.0
LOG2E = math.log2(math.e)


def _cparams(*sem):
    return pltpu.CompilerParams(dimension_semantics=sem, vmem_limit_bytes=VMEM_LIMIT)


def _dot(a, b):
    return jnp.dot(a, b, preferred_element_type=F32)


def _dot_nt(a, b):
    return lax.dot_general(a, b, (((1,), (1,)), ((), ())), preferred_element_type=F32)


def _dot_tn(a, b):
    return lax.dot_general(a, b, (((0,), (0,)), ((), ())), preferred_element_type=F32)


def _sigmoid(x):
    return 1.0 / (1.0 + jnp.exp(-x))


def _silu(x):
    return x * _sigmoid(x)


def _softplus(x):
    return jnp.maximum(x, 0.0) + jnp.log1p(jnp.exp(-jnp.abs(x)))


def _rms(x, gain):
    return x * lax.rsqrt(jnp.mean(x * x, axis=-1, keepdims=True) + EPS) * gain


def _const_spec(shape):
    nd = len(shape)
    return pl.BlockSpec(shape, lambda *_: (0,) * nd)


def _weight_spec(shape, layer=None):
    if layer is None:
        nd = len(shape)
        return pl.BlockSpec(shape, lambda *_: (0,) * nd, pipeline_mode=pl.Buffered(1))
    nd = len(shape) - 1
    return pl.BlockSpec((None,) + tuple(shape[1:]), lambda *_: (layer,) + (0,) * nd, pipeline_mode=pl.Buffered(1))


def _inproj_kernel(x_ref, gain_ref, w_ref, qg_ref, kg_ref, cos_ref, sa_ref, sb_ref, gmat_ref,
                   qkv_ref, gate_ref, ba_ref, q_ref, *kv_refs):
    h = _rms(x_ref[...], gain_ref[...]).astype(BF16)
    c_qkv, c_gate = 3 * GDN_W, 4 * GDN_W
    c_q, c_k, c_v = c_gate + DIFF_W, c_gate + 2 * DIFF_W, c_gate + 3 * DIFF_W

    def proj(a, b):
        return _dot(h, w_ref[:, a:b])

    qkv_ref[...] = proj(0, c_qkv)
    gate_ref[...] = proj(c_qkv, c_gate).astype(gate_ref.dtype)
    v = proj(c_k, c_v)
    ba_ref[...] = proj(c_v, c_v + LANE)
    widen = lambda t: jnp.concatenate([t] * (DIFF_W // LANE), axis=1)
    cos, sa, sb = widen(cos_ref[...]), widen(sa_ref[...]), widen(sb_ref[...])
    half = DIFF_DH // 2

    def norm_rot(z, g):
        ms = _dot((z * z).astype(BF16), gmat_ref[...])
        y = z * lax.rsqrt(ms + EPS) * g
        return y * cos + pltpu.roll(y, DIFF_W - half, 1) * sa + pltpu.roll(y, half, 1) * sb

    q_ref[...] = (norm_rot(proj(c_gate, c_q), qg_ref[...]) * (LOG2E * DIFF_DH ** -0.5)).astype(q_ref.dtype)
    k = norm_rot(proj(c_q, c_k), kg_ref[...])
    if len(kv_refs) == 2:
        kv_refs[0][...] = k
        kv_refs[1][...] = v
    else:
        kt_ref, k16_ref, v2_ref, vt16_ref = kv_refs
        kt_ref[...] = k.T
        k16_ref[...] = k.astype(BF16)
        vt16_ref[...] = v.T.astype(BF16)
        for hd in range(DIFF_H):
            v2_ref[pl.ds(hd, v.shape[0], stride=DIFF_H), :] = v[:, hd * DIFF_DV:(hd + 1) * DIFF_DV]


def _inproj(x, gain, w, qg, kg, cos, sa, sb, gmat, *, tm, q_dtype, seq=None):
    m, d = x.shape
    npos = cos.shape[0] // tm
    nw = w.shape[1]
    row = lambda i: (i, 0)
    pos = lambda i: (i % npos, 0)
    if seq is None:
        kv_specs = [pl.BlockSpec((tm, DIFF_W), row)] * 2
        kv_shapes = [jax.ShapeDtypeStruct((m, DIFF_W), F32)] * 2
    else:
        nt, nb = seq // tm, m // seq
        cols = lambda i: (i // nt, 0, i % nt)
        kv_specs = [pl.BlockSpec((None, DIFF_W, tm), cols), pl.BlockSpec((tm, DIFF_W), row),
                    pl.BlockSpec((None, tm * DIFF_H, DIFF_DV), lambda i: (i // nt, i % nt, 0)),
                    pl.BlockSpec((None, DIFF_W, tm), cols)]
        kv_shapes = [jax.ShapeDtypeStruct((nb, DIFF_W, seq), F32), jax.ShapeDtypeStruct((m, DIFF_W), BF16),
                     jax.ShapeDtypeStruct((nb, seq * DIFF_H, DIFF_DV), F32),
                     jax.ShapeDtypeStruct((nb, DIFF_W, seq), BF16)]
    return pl.pallas_call(
        _inproj_kernel,
        grid=(m // tm,),
        in_specs=[pl.BlockSpec((tm, d), row), _const_spec((1, d)), _weight_spec((d, nw)),
                  _const_spec((1, DIFF_W)), _const_spec((1, DIFF_W)),
                  pl.BlockSpec((tm, LANE), pos), pl.BlockSpec((tm, LANE), pos),
                  pl.BlockSpec((tm, LANE), pos), _const_spec((DIFF_W, DIFF_W))],
        out_specs=[pl.BlockSpec((tm, 3 * GDN_W), row), pl.BlockSpec((tm, GDN_W), row),
                   pl.BlockSpec((tm, LANE), row), pl.BlockSpec((tm, DIFF_W), row)] + kv_specs,
        out_shape=[jax.ShapeDtypeStruct((m, 3 * GDN_W), F32), jax.ShapeDtypeStruct((m, GDN_W), BF16),
                   jax.ShapeDtypeStruct((m, LANE), F32), jax.ShapeDtypeStruct((m, DIFF_W), q_dtype)] + kv_shapes,
        compiler_params=_cparams("parallel"),
        name="inproj",
    )(x, gain, w, qg, kg, cos, sa, sb, gmat)


def _gdn_kernel(x_ref, gate_ref, ba_ref, cw_ref, alog_ref, dtb_ref, gn_ref, s0_ref, cb_ref, tri_ref, lv_ref,
                lvl_ref, o_ref, sout_ref, xbuf, s_scr, *pad_scr, lt, n_valid, chunk):
    j = pl.program_id(1)
    padded = n_valid < lt
    tail = GDN_CONV - 1

    @pl.when(j == 0)
    def _():
        for h in range(GDN_H):
            s_scr[h] = s0_ref[0, h].T
        xbuf[0:8, :] = jnp.zeros((8, 3 * GDN_W), F32)
        xbuf[8 - tail:8, :] = cb_ref[0]

    @pl.when(j > 0)
    def _():
        xbuf[0:8, :] = xbuf[lt:lt + 8, :]

    if padded:
        gate_scr, ba_scr = pad_scr
        xbuf[8:8 + lt, :] = jnp.zeros((lt, 3 * GDN_W), F32)
        xbuf[8:8 + n_valid, :] = x_ref[0]
        gate_scr[...] = jnp.zeros(gate_scr.shape, F32)
        gate_scr[0:n_valid, :] = gate_ref[0].astype(F32)
        ba_scr[...] = jnp.zeros(ba_scr.shape, F32)
        ba_scr[0:n_valid, :] = ba_ref[0]
        gate_src, ba = gate_scr, ba_scr[...]
    else:
        xbuf[8:8 + lt, :] = x_ref[...]
        gate_src, ba = gate_ref, ba_ref[...]

    def conv_act(c0):
        w = cw_ref[:, c0:c0 + GDN_D]
        y = xbuf[8 - tail:8 - tail + lt, c0:c0 + GDN_D] * w[0:1]
        for t in range(1, GDN_CONV):
            y = y + xbuf[8 - tail + t:8 - tail + t + lt, c0:c0 + GDN_D] * w[t:t + 1]
        return _silu(y)

    def l2n(z):
        return z * lax.rsqrt(jnp.sum(z * z, axis=-1, keepdims=True) + EPS)

    beta_all = _sigmoid(ba)
    g_all = -jnp.exp(alog_ref[...]) * _softplus(ba + dtb_ref[...])
    if padded:
        valid = lax.broadcasted_iota(jnp.int32, (lt, 1), 0) < n_valid
        beta_all = jnp.where(valid, beta_all, 0.0)
        g_all = jnp.where(valid, g_all, 0.0)
    g_hi = g_all.astype(BF16)
    g_lo = (g_all - g_hi.astype(F32)).astype(BF16)
    gc_all = _dot(tri_ref[...], g_hi) + _dot(tri_ref[...], g_lo)
    gc_rows = gc_all.T

    lv = lv_ref[...]
    bl = lv.shape[0]
    incl = lv < NOT_SAME_CHUNK
    strict = jnp.logical_and(incl, lv >= 0.0)
    eye = jnp.where(lv == -1.0, 1.0, 0.0)
    lev0 = lv == 0.0
    n_levels = int(math.log2(chunk))
    heads = range(GDN_H)
    units = [(h, r) for r in range(0, lt, bl) for h in heads]

    gcol = [gc_all[:, GDN_H + h:GDN_H + h + 1] for h in heads]
    bcol = [beta_all[:, h:h + 1] for h in heads]
    eg = [jnp.exp(gcol[h]) for h in heads]
    q = [l2n(conv_act(h * GDN_D)) * (GDN_D ** -0.5) for h in heads]
    k = [l2n(conv_act(GDN_W + h * GDN_D)) for h in heads]
    rhs = [jnp.concatenate([conv_act(2 * GDN_W + h * GDN_D) * bcol[h], k[h] * (bcol[h] * eg[h])],
                           axis=1).astype(BF16) for h in heads]
    qb = [q[h].astype(BF16) for h in heads]
    kb = [k[h].astype(BF16) for h in heads]
    qg = [q[h] * eg[h] for h in heads]
    mm, aqk = [], []
    for h, r in units:
        decay = jnp.where(incl, jnp.exp(gcol[h][r:r + bl] - gc_rows[GDN_H + h:GDN_H + h + 1, r:r + bl]), 0.0)
        mm.append(jnp.where(strict, bcol[h][r:r + bl] * _dot_nt(kb[h][r:r + bl], kb[h][r:r + bl]) * decay, 0.0))
        aqk.append((_dot_nt(qb[h][r:r + bl], kb[h][r:r + bl]) * decay).astype(BF16))
    x = [eye - jnp.where(lev0, m, 0.0) for m in mm]
    mmb = [m.astype(BF16) for m in mm]
    run_levels = min(n_levels, max(1, math.ceil(math.log2(n_valid)))) if padded else n_levels
    for lev in range(1, run_levels):
        xb = [xi.astype(BF16) for xi in x]
        y = [_dot(mmb[i] * lvl_ref[lev], xb[i]).astype(BF16) for i in range(len(units))]
        x = [x[i] - _dot(xb[i], y[i]) for i in range(len(units))]
    uwb_u = [_dot(x[i].astype(BF16), rhs[h][r:r + bl]).astype(BF16) for i, (h, r) in enumerate(units)]
    ou_u = [_dot(aqk[i], uwb_u[i]) for i in range(len(units))]
    per_head = lambda vals, h: jnp.concatenate([vals[i] for i, (hh, _) in enumerate(units) if hh == h], axis=0)
    uwb = [per_head(uwb_u, h) for h in heads]
    ou = [per_head(ou_u, h) for h in heads]
    qp = [(qg[h] - ou[h][:, GDN_D:]).astype(BF16) for h in heads]
    for c in range(lt // chunk):
        r0, r1 = c * chunk, (c + 1) * chunk
        for h in heads:
            glast = gcol[h][r1 - 1:r1, :]
            kg = (k[h][r0:r1] * jnp.exp(glast - gcol[h][r0:r1])).astype(BF16)
            ab = _dot_tn(uwb[h][r0:r1, :], kg)
            st = s_scr[h]
            stb = st.astype(BF16)
            o = _dot_nt(qp[h][r0:r1], stb) + ou[h][r0:r1, :GDN_D]
            s_scr[h] = st * jnp.exp(glast) + ab[:GDN_D] - _dot(stb, ab[GDN_D:].astype(BF16))
            gate = gate_src[r0:r1, h * GDN_D:(h + 1) * GDN_D].astype(F32)
            on = _rms(o, gn_ref[...]) * _silu(gate)
            if padded:
                if r0 < n_valid:
                    o_ref[0, :, h * GDN_D:(h + 1) * GDN_D] = on[0:n_valid].astype(o_ref.dtype)
            else:
                o_ref[r0:r1, h * GDN_D:(h + 1) * GDN_D] = on.astype(o_ref.dtype)

    @pl.when(j == pl.num_programs(1) - 1)
    def _():
        for h in range(GDN_H):
            sout_ref[0, h] = s_scr[h].T


def _chunk_level_codes(lt, chunk):
    i = jnp.arange(lt)[:, None]
    j = jnp.arange(lt)[None, :]
    code = jnp.full((lt, lt), NOT_SAME_CHUNK, F32)
    for lev in reversed(range(int(math.log2(chunk)))):
        code = jnp.where((i // (2 << lev)) == (j // (2 << lev)), float(lev), code)
    code = jnp.where(i > j, code, NOT_SAME_CHUNK)
    return jnp.where(i == j, -1.0, code).astype(F32)


def _gdn(qkv, gate, ba, cw, alog, dtb, gn, s0, cb, *, nb, seq, lt):
    padded = seq < lt
    n_valid = seq if padded else lt
    nt = 1 if padded else seq // lt
    chunk = GDN_CHUNK
    i = jnp.arange(lt)
    tri = ((i[:, None] // chunk == i[None, :] // chunk) & (i[:, None] >= i[None, :])).astype(BF16)
    bl = min(lt, 2 * chunk)
    lv = _chunk_level_codes(bl, chunk)
    n_levels = int(math.log2(chunk))
    lvl = jnp.stack([(lv == float(lev)).astype(BF16) for lev in range(n_levels)], axis=0)
    if padded:
        tile = lambda c: pl.BlockSpec((1, seq, c), lambda b, j: (b, 0, 0))
        out_o = jax.ShapeDtypeStruct((nb, seq, GDN_W), BF16)
        scratch = [pltpu.VMEM((lt, GDN_W), F32), pltpu.VMEM((lt, LANE), F32)]
    else:
        tile = lambda c: pl.BlockSpec((lt, c), lambda b, j: (b * nt + j, 0))
        out_o = jax.ShapeDtypeStruct((nb * seq, GDN_W), BF16)
        scratch = []
    return pl.pallas_call(
        functools.partial(_gdn_kernel, lt=lt, n_valid=n_valid, chunk=chunk),
        grid=(nb, nt),
        in_specs=[tile(3 * GDN_W), tile(GDN_W), tile(LANE),
                  _const_spec((GDN_CONV, 3 * GDN_W)), _const_spec((1, LANE)), _const_spec((1, LANE)),
                  _const_spec((1, GDN_D)),
                  pl.BlockSpec((1, GDN_H, GDN_D, GDN_D), lambda b, j: (b, 0, 0, 0)),
                  pl.BlockSpec((1, GDN_CONV - 1, 3 * GDN_W), lambda b, j: (b, 0, 0)),
                  _const_spec((lt, lt)), _const_spec((bl, bl)), _const_spec((n_levels, bl, bl))],
        out_specs=[tile(GDN_W), pl.BlockSpec((1, GDN_H, GDN_D, GDN_D), lambda b, j: (b, 0, 0, 0))],
        out_shape=[out_o, jax.ShapeDtypeStruct((nb, GDN_H, GDN_D, GDN_D), F32)],
        scratch_shapes=[pltpu.VMEM((lt + 8, 3 * GDN_W), F32), pltpu.VMEM((GDN_H, GDN_D, GDN_D), F32)] + scratch,
        compiler_params=_cparams("parallel", "arbitrary"),
        name="gdn",
    )(qkv, gate, ba, cw, alog, dtb, gn, s0, cb, tri, lv, lvl)


def _lambda(l1q, l1k, l2q, l2k, lam_init):
    return (jnp.exp(jnp.sum(l1q[...] * l1k[...], axis=-1, keepdims=True))
            - jnp.exp(jnp.sum(l2q[...] * l2k[...], axis=-1, keepdims=True)) + lam_init)


def _flash_kernel(qi_tab, ki_tab, q_ref, k_ref, vt_ref, l1q, l1k, l2q, l2k, dn_ref, o_ref,
                  m_scr, acc_scr, *, tq, tk, lam_init):
    p = pl.program_id(1)
    qi, ki = qi_tab[p], ki_tab[p]

    @pl.when(ki == 0)
    def _():
        m_scr[...] = jnp.full(m_scr.shape, -jnp.inf, F32)
        acc_scr[...] = jnp.zeros(acc_scr.shape, F32)

    lane = lax.broadcasted_iota(jnp.int32, (1, DIFF_DV), 1)
    pairs = range(2 * DIFF_H)
    ones = jnp.ones((acc_scr.shape[1] - DIFF_DV, tk), BF16)

    def step(masked):
        kc, vt = [], []
        for h in range(DIFF_H):
            hs = slice(h * DIFF_DV, (h + 1) * DIFF_DV)
            kh = k_ref[:, hs]
            vt.append(jnp.concatenate([vt_ref[hs, :], ones], axis=0))
            kc += [jnp.where((lane // DIFF_DH) == c, kh, jnp.zeros_like(kh)) for c in range(2)]
        st = [_dot_nt(kc[i], q_ref[:, (i // 2) * DIFF_DV:(i // 2 + 1) * DIFF_DV]) for i in pairs]
        if masked:
            keys = ki * tk + lax.broadcasted_iota(jnp.int32, (tk, tq), 0)
            queries = qi * tq + lax.broadcasted_iota(jnp.int32, (tk, tq), 1)
            visible = keys <= queries
            st = [jnp.where(visible, st[i], -jnp.inf) for i in pairs]
        m_old = [m_scr[i:i + 1, :] for i in pairs]
        m_new = [jnp.maximum(m_old[i], jnp.max(st[i], axis=0, keepdims=True)) for i in pairs]
        pt = [jnp.exp2(st[i] - m_new[i]).astype(BF16) for i in pairs]
        alpha = [jnp.exp2(m_old[i] - m_new[i]) for i in pairs]
        for i in pairs:
            acc_scr[i] = alpha[i] * acc_scr[i] + _dot(vt[i // 2], pt[i])
            m_scr[i:i + 1, :] = m_new[i]

    @pl.when(ki < qi)
    def _():
        step(False)

    @pl.when(ki == qi)
    def _():
        step(True)
        lam = _lambda(l1q, l1k, l2q, l2k, lam_init)
        for h in range(DIFF_H):
            a0, a1 = acc_scr[2 * h], acc_scr[2 * h + 1]
            ot = (a0[:DIFF_DV] / a0[DIFF_DV:DIFF_DV + 1]
                  - lam * (a1[:DIFF_DV] / a1[DIFF_DV:DIFF_DV + 1]))
            o_ref[:, h * DIFF_DV:(h + 1) * DIFF_DV] = (_rms(ot.T, dn_ref[...]) * (1.0 - lam_init)).astype(o_ref.dtype)


def _flash(q, k, v, lams, dn, *, nb, seq, tq, lam_init):
    nq = seq // tq
    pairs = [(a, b) for a in range(nq) for b in range(a + 1)]
    qi_tab = jnp.array([a for a, _ in pairs], jnp.int32)
    ki_tab = jnp.array([b for _, b in pairs], jnp.int32)
    qmap = lambda b, p, qt, kt: (b * nq + qt[p], 0)
    kmap = lambda b, p, qt, kt: (b * nq + kt[p], 0)
    cmap = lambda b, p, qt, kt: (0, 0)
    grid_spec = pltpu.PrefetchScalarGridSpec(
        num_scalar_prefetch=2,
        grid=(nb, len(pairs)),
        in_specs=[pl.BlockSpec((tq, DIFF_W), qmap), pl.BlockSpec((tq, DIFF_W), kmap),
                  pl.BlockSpec((None, DIFF_W, tq), lambda b, p, qt, kt: (b, 0, kt[p]))]
                 + [pl.BlockSpec((1, DIFF_DH), cmap)] * 4 + [pl.BlockSpec((1, DIFF_DV), cmap)],
        out_specs=pl.BlockSpec((tq, DIFF_W), qmap),
        scratch_shapes=[pltpu.VMEM((2 * DIFF_H, tq), F32),
                        pltpu.VMEM((2 * DIFF_H, DIFF_DV + 16, tq), F32)],
    )
    return pl.pallas_call(
        functools.partial(_flash_kernel, tq=tq, tk=tq, lam_init=lam_init),
        grid_spec=grid_spec,
        out_shape=jax.ShapeDtypeStruct((nb * seq, DIFF_W), BF16),
        compiler_params=_cparams("parallel", "arbitrary"),
        name="diff_flash",
    )(qi_tab, ki_tab, q, k, v, *lams, dn)


def _decode_kernel(pt_ref, q_ref, qmask_ref, kn_ref, vn_ref, expand_ref, hmask_ref, l1q, l1k, l2q, l2k, dn_ref,
                   *rest, n_new, ppg, lam_init):
    kpages, vpages = rest[:ppg], rest[ppg:2 * ppg]
    o_ref = rest[2 * ppg]
    qb_scr, kpad_scr, vpad_scr, m_scr, l_scr, acc_scr = rest[2 * ppg + 1:]
    g = pl.program_id(1)
    nrow = 2 * DIFF_H * 8

    @pl.when(g == 0)
    def _():
        kpad_scr[...] = jnp.zeros(kpad_scr.shape, F32)
        kpad_scr[0:n_new, :] = q_ref[0].astype(F32)
        q8 = kpad_scr[0:8, :]
        qb_scr[...] = (jnp.concatenate([q8] * (2 * DIFF_H), axis=0) * qmask_ref[...]).astype(BF16)
        m_scr[...] = jnp.full(m_scr.shape, -jnp.inf, F32)
        l_scr[...] = jnp.zeros(l_scr.shape, F32)
        acc_scr[...] = jnp.zeros(acc_scr.shape, F32)

    def update(s, vals):
        m_old = m_scr[...]
        m_new = jnp.maximum(m_old, jnp.max(s, axis=-1, keepdims=True))
        alpha = jnp.exp2(m_old - m_new)
        pr = jnp.exp2(s - m_new)
        l_scr[...] = alpha * l_scr[...] + jnp.sum(pr, axis=-1, keepdims=True)
        prb = pr.astype(BF16)
        n = len(vals)
        stacked = prb if n == 1 else jnp.concatenate([prb[:, jj * PAGE:(jj + 1) * PAGE] for jj in range(n)], axis=0)
        spread = _dot(stacked, expand_ref[...])
        pes = [(spread[jj * nrow:(jj + 1) * nrow] * hmask_ref[...]).astype(BF16) for jj in range(n)]
        terms = [_dot(pes[jj], vals[jj]) for jj in range(n)]
        while len(terms) > 1:
            terms = [terms[i] + terms[i + 1] for i in range(0, len(terms) - 1, 2)] + terms[len(terms) & ~1:]
        acc_scr[...] = alpha * acc_scr[...] + terms[0]
        m_scr[...] = m_new

    qb = qb_scr[...]
    s = jnp.concatenate([_dot(qb, kp[...].astype(BF16)) for kp in kpages], axis=1)
    update(s, [vp[...].astype(BF16) for vp in vpages])

    @pl.when(g == pl.num_programs(1) - 1)
    def _():
        kpad_scr[...] = jnp.zeros(kpad_scr.shape, F32)
        kpad_scr[0:n_new, :] = kn_ref[0]
        vpad_scr[...] = jnp.zeros(vpad_scr.shape, F32)
        vpad_scr[0:n_new * DIFF_H, :] = vn_ref[0]
        tok = lax.broadcasted_iota(jnp.int32, (nrow, PAGE), 0) % 8
        key = lax.broadcasted_iota(jnp.int32, (nrow, PAGE), 1)
        ok = jnp.logical_and(key < n_new, key <= tok)
        update(jnp.where(ok, _dot_nt(qb, kpad_scr[...].astype(BF16)), -jnp.inf), [vpad_scr[...].astype(BF16)])
        lam = _lambda(l1q, l1k, l2q, l2k, lam_init)
        on = acc_scr[...] / l_scr[...]
        half = nrow // 2
        for h in range(DIFF_H):
            o = on[8 * h:8 * h + 8, :] - lam * on[half + 8 * h:half + 8 * h + 8, :]
            o_ref[0, :, h * DIFF_DV:(h + 1) * DIFF_DV] = (_rms(o, dn_ref[...]) * (1.0 - lam_init)).astype(o_ref.dtype)


def _decode(page_table, q, kn, vn, cache_k, cache_v, lams, dn, *, layer, ppg, lam_init):
    nb, n_new, _ = q.shape
    n_pages = page_table.shape[1]
    nrow = 2 * DIFF_H * 8
    r = jnp.arange(nrow)
    hc = 2 * ((r // 8) % DIFF_H) + r // (nrow // 2)
    qmask = (jnp.arange(DIFF_W)[None, :] // DIFF_DH == hc[:, None]).astype(F32)
    vrows = PAGE * DIFF_H
    jj = jnp.arange(vrows)
    expand = (jj[None, :] // DIFF_H == jnp.arange(PAGE)[:, None]).astype(BF16)
    hmask = (jj[None, :] % DIFF_H == ((r // 8) % DIFF_H)[:, None]).astype(F32)
    per_b = lambda n, c: pl.BlockSpec((1, n, c), lambda b, g, pt: (b, 0, 0))
    cmap = lambda b, g, pt: (0, 0)

    def page_spec(j, rows, cols):
        return pl.BlockSpec((None, None, rows, cols),
                            lambda b, g, pt: (layer, pt[b * n_pages + g * ppg + j], 0, 0))

    grid_spec = pltpu.PrefetchScalarGridSpec(
        num_scalar_prefetch=1,
        grid=(nb, n_pages // ppg),
        in_specs=[per_b(n_new, DIFF_W), pl.BlockSpec((nrow, DIFF_W), cmap), per_b(n_new, DIFF_W),
                  per_b(n_new * DIFF_H, DIFF_DV), pl.BlockSpec((PAGE, vrows), cmap),
                  pl.BlockSpec((nrow, vrows), cmap)]
                 + [pl.BlockSpec((1, DIFF_DH), cmap)] * 4 + [pl.BlockSpec((1, DIFF_DV), cmap)]
                 + [page_spec(j, DIFF_W, PAGE) for j in range(ppg)]
                 + [page_spec(j, vrows, DIFF_DV) for j in range(ppg)],
        out_specs=pl.BlockSpec((1, 8, DIFF_W), lambda b, g, pt: (b, 0, 0)),
        scratch_shapes=[pltpu.VMEM((nrow, DIFF_W), BF16), pltpu.VMEM((PAGE, DIFF_W), F32),
                        pltpu.VMEM((vrows, DIFF_DV), F32),
                        pltpu.VMEM((nrow, 1), F32), pltpu.VMEM((nrow, 1), F32),
                        pltpu.VMEM((nrow, DIFF_DV), F32)],
    )
    ck = cache_k.transpose(0, 1, 3, 4, 2).reshape(cache_k.shape[0], cache_k.shape[1], DIFF_W, PAGE)
    cv = cache_v.reshape(cache_v.shape[0], cache_v.shape[1], vrows, DIFF_DV)
    out = pl.pallas_call(
        functools.partial(_decode_kernel, n_new=n_new, ppg=ppg, lam_init=lam_init),
        grid_spec=grid_spec,
        out_shape=jax.ShapeDtypeStruct((nb, 8, DIFF_W), BF16),
        compiler_params=_cparams("parallel", "arbitrary"),
        name="diff_decode",
    )(page_table.reshape(-1), q, qmask, kn, vn.reshape(nb, n_new * DIFF_H, DIFF_DV), expand, hmask, *lams, dn,
      *([ck] * ppg), *([cv] * ppg))
    return out[:, :n_new]


def _head_rms(z, gain, width):
    outs = [_rms(z[:, a:a + width], gain) for a in range(0, z.shape[-1], width)]
    return jnp.concatenate(outs, axis=1)


def _mixout_kernel(x_ref, og_ref, od_ref, wo_ref, nc_ref, wq_ref, qn_ref, x1_ref, qc_ref):
    mix = jnp.concatenate([og_ref[...], od_ref[...]], axis=1)
    x1 = x_ref[...] + _dot(mix, wo_ref[...])
    x1_ref[...] = x1
    hc = _rms(x1, nc_ref[...]).astype(BF16)
    qc = _head_rms(_dot(hc, wq_ref[...]), qn_ref[...], CA_DH) * (CA_DH ** -0.5)
    qc_ref[...] = qc.astype(qc_ref.dtype)


def _mixout(x, og, od, wo, nc, wq, qn, *, tm, q_dtype, layer):
    m, d = x.shape
    row = lambda i: (i, 0)
    return pl.pallas_call(
        _mixout_kernel,
        grid=(m // tm,),
        in_specs=[pl.BlockSpec((tm, d), row), pl.BlockSpec((tm, GDN_W), row), pl.BlockSpec((tm, DIFF_W), row),
                  _weight_spec(wo.shape, layer), _const_spec((1, d)), _weight_spec(wq.shape, layer),
                  _const_spec((1, CA_DH))],
        out_specs=[pl.BlockSpec((tm, d), row), pl.BlockSpec((tm, CA_W), row)],
        out_shape=[jax.ShapeDtypeStruct((m, d), F32), jax.ShapeDtypeStruct((m, CA_W), q_dtype)],
        compiler_params=_cparams("parallel"),
        name="mixout",
    )(x, og, od, wo, nc, wq, qn)


def _memkv_kernel(m_ref, nm_ref, wk_ref, wv_ref, kn_ref, mk_ref, mv_ref):
    mn = _rms(m_ref[...], nm_ref[...]).astype(BF16)
    mk_ref[...] = _head_rms(_dot(mn, wk_ref[...]), kn_ref[...], CA_DH)
    mv_ref[...] = _dot(mn, wv_ref[...])


def _memkv(mem, nm, wk, wv, kn, *, tm, layer):
    m, d = mem.shape
    row = lambda i: (i, 0)
    return pl.pallas_call(
        _memkv_kernel,
        grid=(m // tm,),
        in_specs=[pl.BlockSpec((tm, d), row), _const_spec((1, d)), _weight_spec(wk.shape, layer),
                  _weight_spec(wv.shape, layer), _const_spec((1, CA_DH))],
        out_specs=[pl.BlockSpec((tm, CA_W), row)] * 2,
        out_shape=[jax.ShapeDtypeStruct((m, CA_W), F32)] * 2,
        compiler_params=_cparams("parallel"),
        name="memkv",
    )(mem, nm, wk, wv, kn)


def _memory_attend(qs, ks, vs):
    s = [_dot_nt(q, k) for q, k in zip(qs, ks)]
    e = [jnp.exp(si - jnp.max(si, axis=-1, keepdims=True)) for si in s]
    pr = [(ei / jnp.sum(ei, axis=-1, keepdims=True)).astype(BF16) for ei in e]
    return [_dot(p, v) for p, v in zip(pr, vs)]


def _head_cols(h):
    return slice(h * CA_DH, (h + 1) * CA_DH)


def _cross_kernel(q_ref, mk_ref, mv_ref, o_ref, pad_scr, *, n_valid):
    groups = range(q_ref.shape[0])
    pad_scr[...] = jnp.zeros(pad_scr.shape, F32)
    for g in groups:
        pad_scr[g, 0:n_valid, :] = q_ref[g].astype(F32)
    q = [pad_scr[g].astype(BF16) for g in groups]
    units = [(g, h) for g in groups for h in range(CA_H)]
    outs = _memory_attend([q[g][:, _head_cols(h)] for g, h in units],
                          [mk_ref[g, :, h, :].astype(BF16) for g, h in units],
                          [mv_ref[g, :, h, :].astype(BF16) for g, h in units])
    for (g, h), oh in zip(units, outs):
        o_ref[g, :, _head_cols(h)] = oh[0:n_valid].astype(o_ref.dtype)


def _cross(q, mk, mv, *, layer, group):
    nb, seq, _ = q.shape
    mem_spec = pl.BlockSpec((None, group, mk.shape[2], CA_H, CA_DH), lambda b: (layer, b, 0, 0, 0))
    q_spec = pl.BlockSpec((group, seq, CA_W), lambda b: (b, 0, 0))
    return pl.pallas_call(
        functools.partial(_cross_kernel, n_valid=seq),
        grid=(nb // group,),
        in_specs=[q_spec, mem_spec, mem_spec],
        out_specs=q_spec,
        out_shape=jax.ShapeDtypeStruct((nb, seq, CA_W), BF16),
        scratch_shapes=[pltpu.VMEM((group, 8, CA_W), F32)],
        compiler_params=_cparams("parallel"),
        name="cross_attn",
    )(q, mk, mv)


def _ffn_kernel(x_ref, oc_ref, *rest, tm, stride, ncol, attend):
    if attend:
        mk_ref, mv_ref = rest[:2]
        rest = rest[2:]
    wco_ref, nf_ref, wg_ref, wu_ref, cw_ref, wd_ref, hist_ref, y_ref, hout_ref, gbuf, acc = rest
    j = pl.program_id(1)
    d_ff = wg_ref.shape[1]
    hrows = (FFN_CONV - 1) * stride
    base = gbuf.shape[0] - tm

    @pl.when(j == 0)
    def _():
        gbuf[base - hrows:base, :] = hist_ref[0]

    @pl.when(j > 0)
    def _():
        gbuf[base - hrows:base, :] = gbuf[base + tm - hrows:base + tm, :]

    if attend:
        q = oc_ref[...]
        heads = range(CA_H)
        oc = jnp.concatenate(_memory_attend([q[:, _head_cols(h)] for h in heads],
                                            [mk_ref[0, :, _head_cols(h)].astype(BF16) for h in heads],
                                            [mv_ref[0, :, _head_cols(h)].astype(BF16) for h in heads]),
                             axis=1).astype(BF16)
    else:
        oc = oc_ref[...]
    x2 = x_ref[...] + _dot(oc, wco_ref[...])
    hf = _rms(x2, nf_ref[...]).astype(BF16)
    acc[...] = x2
    for c0 in range(0, d_ff, ncol):
        cs = slice(c0, c0 + ncol)
        gbuf[base:base + tm, cs] = _dot(hf, wg_ref[:, cs])
        w = cw_ref[:, cs]
        gt = gbuf[base - hrows:base - hrows + tm, cs] * w[0:1]
        for t in range(1, FFN_CONV):
            off = base - hrows + t * stride
            gt = gt + gbuf[off:off + tm, cs] * w[t:t + 1]
        act = (_silu(gt) * _dot(hf, wu_ref[:, cs])).astype(BF16)
        acc[...] += _dot(act, wd_ref[cs, :])
    y_ref[...] = acc[...]
    hout_ref[0] = gbuf[base + tm - hrows:base + tm, :]


def _ffn(x, oc, wco, nf, wg, wu, cw, wd, hist, *, ngroups, tm, stride, ncol, layer, mem=None):
    m, d = x.shape
    d_ff = wg.shape[-1]
    nt = m // (ngroups * tm)
    hrows = (FFN_CONV - 1) * stride
    base = -(-hrows // 8) * 8
    row = lambda b, j: (b * nt + j, 0)
    grp = lambda b, j: (b, 0, 0)
    mem = tuple(mem or ())
    return pl.pallas_call(
        functools.partial(_ffn_kernel, tm=tm, stride=stride, ncol=ncol, attend=bool(mem)),
        grid=(ngroups, nt),
        in_specs=[pl.BlockSpec((tm, d), row), pl.BlockSpec((tm, CA_W), row)]
                 + [pl.BlockSpec((1,) + a.shape[1:], grp) for a in mem]
                 + [_weight_spec(wco.shape, layer),
                  _const_spec((1, d)), _weight_spec(wg.shape, layer), _weight_spec(wu.shape, layer),
                  _const_spec((FFN_CONV, d_ff)), _weight_spec(wd.shape, layer),
                  pl.BlockSpec((1, hrows, d_ff), grp)],
        out_specs=[pl.BlockSpec((tm, d), row), pl.BlockSpec((1, hrows, d_ff), grp)],
        out_shape=[jax.ShapeDtypeStruct((m, d), F32), jax.ShapeDtypeStruct((ngroups, hrows, d_ff), F32)],
        scratch_shapes=[pltpu.VMEM((base + tm, d_ff), F32), pltpu.VMEM((tm, d), F32)],
        compiler_params=_cparams("parallel", "arbitrary"),
        name="ffn",
    )(x, oc, *mem, wco, nf, wg, wu, cw, wd, hist)


def _rotary_tables(pos):
    half = DIFF_DH // 2
    inv = ROPE_THETA ** (-jnp.arange(half, dtype=F32) / half)
    ang = pos.astype(F32)[:, None] * inv[None, :]
    cos, sin, zero = jnp.cos(ang), jnp.sin(ang), jnp.zeros_like(ang)
    reps = (1, LANE // DIFF_DH)
    return (jnp.tile(jnp.concatenate([cos, cos], axis=1), reps),
            jnp.tile(jnp.concatenate([-sin, zero], axis=1), reps),
            jnp.tile(jnp.concatenate([zero, sin], axis=1), reps))


def kernel(x_prompt, x_sample, mem_prompt, cache_k, cache_v, page_table, state_gdn, state_gdn_conv, cache_mem_k, cache_mem_v, state_ffn_conv, norm_mix, w_in, conv_qkv, a_log, dt_bias, gdn_norm, qnorm_diff, knorm_diff, lam_q1, lam_k1, lam_q2, lam_k2, diff_norm, w_out, norm_cross, norm_mem, w_cq, w_ck, w_cv, qnorm_cross, knorm_cross, w_co, norm_ffn, w_gate, w_up, conv_ffn, w_down):
    depth = w_in.shape[0]
    bp, lp, d = x_prompt.shape
    bs, ls, _ = x_sample.shape
    mlen = mem_prompt.shape[1]
    d_ff = w_gate.shape[2]
    past_len = page_table.shape[1] * PAGE
    mp, ms = bp * lp, bs * ls

    cos_p, sa_p, sb_p = _rotary_tables(jnp.arange(lp, dtype=jnp.int32))
    cos_s, sa_s, sb_s = [jnp.tile(t, (bs, 1)) for t in
                         _rotary_tables(past_len + jnp.arange(ls, dtype=jnp.int32))]
    ii = jnp.arange(DIFF_W)
    gmat = jnp.where(ii[:, None] // DIFF_DH == ii[None, :] // DIFF_DH, 1.0 / DIFF_DH, 0.0).astype(BF16)
    row = lambda a: a.reshape(1, -1).astype(F32)
    lane_pad = lambda a, off: jnp.zeros((1, LANE), F32).at[0, off:off + a.shape[0]].set(a.astype(F32))
    c_qkv, c_gate = 3 * GDN_W, 4 * GDN_W
    c_ba = c_gate + 2 * GDN_H

    xp = x_prompt.reshape(mp, d)
    xs = x_sample.reshape(ms, d)
    memp = mem_prompt.reshape(bp * mlen, d)
    zeros_s0 = jnp.zeros((bp, GDN_H, GDN_D, GDN_D), F32)
    zeros_cb = jnp.zeros((bp, GDN_CONV - 1, 3 * GDN_W), F32)
    zeros_fb = jnp.zeros((bp, FFN_CONV - 1, d_ff), F32)

    wb = {name: w.astype(BF16) for name, w in
          dict(out=w_out, cq=w_cq, ck=w_ck, cv=w_cv, co=w_co, gate=w_gate, up=w_up, down=w_down).items()}

    p_out, s_out = [], []
    for l in range(depth):
        lam_init = 0.8 - 0.6 * math.exp(-0.3 * l)
        wl = w_in[l]
        w_in_l = jnp.concatenate([wl[:, :c_gate], wl[:, c_ba:], wl[:, c_gate:c_ba],
                                  jnp.zeros((d, LANE - 2 * GDN_H), F32)], axis=1).astype(BF16)
        qg = jnp.tile(row(qnorm_diff[l]), (1, 2 * DIFF_H))
        kg = jnp.tile(row(knorm_diff[l]), (1, 2 * DIFF_H))
        lams = (row(lam_q1[l]), row(lam_k1[l]), row(lam_q2[l]), row(lam_k2[l]))
        alog = lane_pad(a_log[l], GDN_H)
        dtb = lane_pad(dt_bias[l], GDN_H)
        inproj = functools.partial(_inproj, gain=row(norm_mix[l]), w=w_in_l, qg=qg, kg=kg, gmat=gmat)
        gdn = functools.partial(_gdn, cw=conv_qkv[l], alog=alog, dtb=dtb, gn=row(gdn_norm[l]))
        mixout = functools.partial(_mixout, wo=wb["out"], nc=row(norm_cross[l]), wq=wb["cq"],
                                   qn=row(qnorm_cross[l]), layer=l)
        ffn = functools.partial(_ffn, wco=wb["co"], nf=row(norm_ffn[l]), wg=wb["gate"], wu=wb["up"],
                                cw=conv_ffn[l], wd=wb["down"], layer=l)

        mk_p, mv_p = _memkv(memp, row(norm_mem[l]), wb["ck"], wb["cv"], row(knorm_cross[l]), tm=512, layer=l)
        qkv, gate, ba, q, kt, k16, v2, vt16 = inproj(xp, cos=cos_p, sa=sa_p, sb=sb_p, tm=1024, q_dtype=BF16,
                                                     seq=lp)
        og, sp = gdn(qkv, gate, ba, s0=zeros_s0, cb=zeros_cb, nb=bp, seq=lp, lt=256)
        od = _flash(q, k16, vt16, lams, row(diff_norm[l]), nb=bp, seq=lp, tq=512, lam_init=lam_init)
        k = kt.reshape(bp, 2 * DIFF_H, DIFF_DH, lp).transpose(0, 3, 1, 2)
        v = v2.reshape(bp, lp, DIFF_H, DIFF_DV)
        x1, qc = mixout(xp, og, od, tm=1024, q_dtype=BF16)
        xp, fp = ffn(x1, qc, hist=zeros_fb, ngroups=bp, tm=512, stride=1, ncol=1408,
                     mem=(mk_p.reshape(bp, mlen, CA_W), mv_p.reshape(bp, mlen, CA_W)))
        p_out.append((k, v, sp,
                      qkv.reshape(bp, lp, c_qkv)[:, lp - (GDN_CONV - 1):],
                      mk_p.reshape(bp, mlen, CA_H, CA_DH), mv_p.reshape(bp, mlen, CA_H, CA_DH), fp))

        qkv, gate, ba, q, k, v = inproj(xs, cos=cos_s, sa=sa_s, sb=sb_s, tm=ms, q_dtype=F32)
        r3 = lambda a: a.reshape(bs, ls, a.shape[-1])
        og, ss = gdn(r3(qkv), r3(gate), r3(ba), s0=state_gdn[l], cb=state_gdn_conv[l], nb=bs, seq=ls,
                     lt=GDN_CHUNK)
        od = _decode(page_table, r3(q), r3(k), r3(v), cache_k, cache_v, lams, row(diff_norm[l]),
                     layer=l, ppg=math.gcd(page_table.shape[1], 32), lam_init=lam_init)
        x1, qc = mixout(xs, og.reshape(ms, GDN_W), od.reshape(ms, DIFF_W), tm=ms, q_dtype=F32)
        oc = _cross(r3(qc), cache_mem_k, cache_mem_v, layer=l, group=math.gcd(bs, 4))
        tmaj = lambda a: a.reshape(bs, ls, -1).transpose(1, 0, 2).reshape(ms, -1)
        hist = state_ffn_conv[l].transpose(1, 0, 2).reshape(1, (FFN_CONV - 1) * bs, d_ff)
        y_t, fs_t = ffn(tmaj(x1), tmaj(oc), hist=hist, ngroups=1, tm=ms, stride=bs, ncol=1408)
        xs = y_t.reshape(ls, bs, d).transpose(1, 0, 2).reshape(ms, d)
        fs = fs_t.reshape(FFN_CONV - 1, bs, d_ff).transpose(1, 0, 2)
        conv_s = jnp.concatenate([state_gdn_conv[l], r3(qkv)], axis=1)[:, ls:]
        s_out.append((k.reshape(bs, ls, 2 * DIFF_H, DIFF_DH), v.reshape(bs, ls, DIFF_H, DIFF_DV), ss, conv_s, fs))

    pk, pv, pg, pc, pmk, pmv, pf = [jnp.stack(t, axis=0) for t in zip(*p_out)]
    sk, sv, sg, sc, sf = [jnp.stack(t, axis=0) for t in zip(*s_out)]
    return (xp.reshape(bp, lp, d), xs.reshape(bs, ls, d), pk, pv, pg, pc, pmk, pmv, pf, sk, sv, sg, sc, sf)
```

```python
import functools
import math

import jax
import jax.numpy as jnp
from jax import lax
from jax.experimental import pallas as pl
from jax.experimental.pallas import tpu as pltpu

F32 = jnp.float32
BF16 = jnp.bfloat16

EPS = 1e-6
ROPE_THETA = 10000.0
GDN_H = 4
GDN_D = 128
GDN_W = GDN_H * GDN_D
GDN_CONV = 4
GDN_CHUNK = 64
DIFF_H = 4
DIFF_DH = 64
DIFF_DV = 128
DIFF_W = DIFF_H * DIFF_DV
CA_H = 4
CA_DH = 128
CA_W = CA_H * CA_DH
FFN_CONV = 3
PAGE = 128
LANE = 128
VMEM_LIMIT = 56 * 1024 * 1024
NOT_SAME_CHUNK = 99.0
LOG2E = math.log2(math.e)


def _cparams(*sem):
    return pltpu.CompilerParams(dimension_semantics=sem, vmem_limit_bytes=VMEM_LIMIT)


def _dot(a, b):
    return jnp.dot(a, b, preferred_element_type=F32)


def _dot_nt(a, b):
    return lax.dot_general(a, b, (((1,), (1,)), ((), ())), preferred_element_type=F32)


def _dot_tn(a, b):
    return lax.dot_general(a, b, (((0,), (0,)), ((), ())), preferred_element_type=F32)


def _sigmoid(x):
    return 1.0 / (1.0 + jnp.exp(-x))


def _silu(x):
    return x * _sigmoid(x)


def _softplus(x):
    return jnp.maximum(x, 0.0) + jnp.log1p(jnp.exp(-jnp.abs(x)))


def _rms(x, gain):
    return x * lax.rsqrt(jnp.mean(x * x, axis=-1, keepdims=True) + EPS) * gain


def _const_spec(shape):
    nd = len(shape)
    return pl.BlockSpec(shape, lambda *_: (0,) * nd)


def _layer_spec(a, layer):
    return pl.BlockSpec((None,) + tuple(a.shape[1:]), lambda *_: (layer, 0, 0))


def _weight_spec(shape, layer=None):
    if layer is None:
        nd = len(shape)
        return pl.BlockSpec(shape, lambda *_: (0,) * nd, pipeline_mode=pl.Buffered(1))
    nd = len(shape) - 1
    return pl.BlockSpec((None,) + tuple(shape[1:]), lambda *_: (layer,) + (0,) * nd, pipeline_mode=pl.Buffered(1))


def _inproj_kernel(x_ref, gain_ref, w_ref, qg_ref, kg_ref, cos_ref, sa_ref, sb_ref, gmat_ref, *refs, n_prev):
    prev_refs, (qkv_ref, gate_ref, ba_ref, q_ref, *kv_refs) = refs[:2 * (n_prev > 0)], refs[2 * (n_prev > 0):]
    h = _rms(x_ref[...], gain_ref[...]).astype(BF16)
    c_qkv, c_gate = 3 * GDN_W, 4 * GDN_W
    c_q, c_k, c_v = c_gate + DIFF_W, c_gate + 2 * DIFF_W, c_gate + 3 * DIFF_W

    def proj(a, b):
        return _dot(h, w_ref[:, a:b])

    qkv_ref[...] = proj(0, c_qkv)
    gate_ref[...] = proj(c_qkv, c_gate).astype(gate_ref.dtype)
    v = proj(c_k, c_v)
    ba_ref[...] = proj(c_v, c_v + LANE)
    widen = lambda t: jnp.concatenate([t] * (DIFF_W // LANE), axis=1)
    cos, sa, sb = widen(cos_ref[...]), widen(sa_ref[...]), widen(sb_ref[...])
    half = DIFF_DH // 2

    def norm_rot(z, g):
        ms = _dot((z * z).astype(BF16), gmat_ref[...])
        y = z * lax.rsqrt(ms + EPS) * g
        return y * cos + pltpu.roll(y, DIFF_W - half, 1) * sa + pltpu.roll(y, half, 1) * sb

    q_ref[...] = (norm_rot(proj(c_gate, c_q), qg_ref[...]) * (LOG2E * DIFF_DH ** -0.5)).astype(q_ref.dtype)
    k = norm_rot(proj(c_q, c_k), kg_ref[...])
    if len(kv_refs) == 2:
        kv_refs[0][...] = k
        kv_refs[1][...] = v
    else:
        kt_ref, k16_ref, v2_ref, vt16_ref = kv_refs
        if n_prev:
            kt_ref[0:n_prev] = prev_refs[0][...]
            v2_ref[0:n_prev] = prev_refs[1][...]
        kt_ref[n_prev] = k.T
        k16_ref[...] = k.astype(BF16)
        vt16_ref[...] = v.T.astype(BF16)
        for hd in range(DIFF_H):
            v2_ref[n_prev, pl.ds(hd, v.shape[0], stride=DIFF_H), :] = v[:, hd * DIFF_DV:(hd + 1) * DIFF_DV]


def _inproj(x, gain, w, qg, kg, cos, sa, sb, gmat, *, tm, q_dtype, layer, seq=None, prev=()):
    m, d = x.shape
    npos = cos.shape[0] // tm
    row = lambda i: (i, 0)
    pos = lambda i: (i % npos, 0)
    n_prev = prev[0].shape[0] if prev else 0
    if seq is None:
        kv_specs = [pl.BlockSpec((tm, DIFF_W), row)] * 2
        kv_shapes = [jax.ShapeDtypeStruct((m, DIFF_W), F32)] * 2
        prev_specs = []
    else:
        nt, nb = seq // tm, m // seq
        kcols = lambda i: (0, i // nt, 0, i % nt)
        vrows = lambda i: (0, i // nt, i % nt, 0)
        kt_spec = lambda n: pl.BlockSpec((n, None, DIFF_W, tm), kcols)
        v2_spec = lambda n: pl.BlockSpec((n, None, tm * DIFF_H, DIFF_DV), vrows)
        kv_specs = [kt_spec(n_prev + 1), pl.BlockSpec((tm, DIFF_W), row), v2_spec(n_prev + 1),
                    pl.BlockSpec((None, DIFF_W, tm), lambda i: (i // nt, 0, i % nt))]
        kv_shapes = [jax.ShapeDtypeStruct((n_prev + 1, nb, DIFF_W, seq), F32),
                     jax.ShapeDtypeStruct((m, DIFF_W), BF16),
                     jax.ShapeDtypeStruct((n_prev + 1, nb, seq * DIFF_H, DIFF_DV), F32),
                     jax.ShapeDtypeStruct((nb, DIFF_W, seq), BF16)]
        prev_specs = [kt_spec(n_prev), v2_spec(n_prev)] if prev else []
    return pl.pallas_call(
        functools.partial(_inproj_kernel, n_prev=n_prev),
        grid=(m // tm,),
        in_specs=[pl.BlockSpec((tm, d), row), _layer_spec(gain, layer), _weight_spec(w.shape, layer),
                  _layer_spec(qg, layer), _layer_spec(kg, layer),
                  pl.BlockSpec((tm, LANE), pos), pl.BlockSpec((tm, LANE), pos),
                  pl.BlockSpec((tm, LANE), pos), _const_spec((DIFF_W, DIFF_W))] + prev_specs,
        out_specs=[pl.BlockSpec((tm, 3 * GDN_W), row), pl.BlockSpec((tm, GDN_W), row),
                   pl.BlockSpec((tm, LANE), row), pl.BlockSpec((tm, DIFF_W), row)] + kv_specs,
        out_shape=[jax.ShapeDtypeStruct((m, 3 * GDN_W), F32), jax.ShapeDtypeStruct((m, GDN_W), BF16),
                   jax.ShapeDtypeStruct((m, LANE), F32), jax.ShapeDtypeStruct((m, DIFF_W), q_dtype)] + kv_shapes,
        compiler_params=_cparams("parallel"),
        name="inproj",
    )(x, gain, w, qg, kg, cos, sa, sb, gmat, *prev)


def _gdn_kernel(x_ref, gate_ref, ba_ref, cw_ref, alog_ref, dtb_ref, gn_ref, s0_ref, cb_ref, tri_ref, lv_ref,
                lvl_ref, o_ref, sout_ref, xbuf, s_scr, *pad_scr, lt, n_valid, chunk):
    j = pl.program_id(1)
    padded = n_valid < lt
    tail = GDN_CONV - 1

    @pl.when(j == 0)
    def _():
        for h in range(GDN_H):
            s_scr[h] = s0_ref[0, h].T
        xbuf[0:8, :] = jnp.zeros((8, 3 * GDN_W), F32)
        xbuf[8 - tail:8, :] = cb_ref[0]

    @pl.when(j > 0)
    def _():
        xbuf[0:8, :] = xbuf[lt:lt + 8, :]

    if padded:
        gate_scr, ba_scr = pad_scr
        xbuf[8:8 + lt, :] = jnp.zeros((lt, 3 * GDN_W), F32)
        xbuf[8:8 + n_valid, :] = x_ref[0]
        gate_scr[...] = jnp.zeros(gate_scr.shape, F32)
        gate_scr[0:n_valid, :] = gate_ref[0].astype(F32)
        ba_scr[...] = jnp.zeros(ba_scr.shape, F32)
        ba_scr[0:n_valid, :] = ba_ref[0]
        gate_src, ba = gate_scr, ba_scr[...]
    else:
        xbuf[8:8 + lt, :] = x_ref[...]
        gate_src, ba = gate_ref, ba_ref[...]

    def conv_act(c0):
        w = cw_ref[:, c0:c0 + GDN_D]
        y = xbuf[8 - tail:8 - tail + lt, c0:c0 + GDN_D] * w[0:1]
        for t in range(1, GDN_CONV):
            y = y + xbuf[8 - tail + t:8 - tail + t + lt, c0:c0 + GDN_D] * w[t:t + 1]
        return _silu(y)

    def l2n(z):
        return z * lax.rsqrt(jnp.sum(z * z, axis=-1, keepdims=True) + EPS)

    beta_all = _sigmoid(ba)
    g_all = -jnp.exp(alog_ref[...]) * _softplus(ba + dtb_ref[...])
    if padded:
        valid = lax.broadcasted_iota(jnp.int32, (lt, 1), 0) < n_valid
        beta_all = jnp.where(valid, beta_all, 0.0)
        g_all = jnp.where(valid, g_all, 0.0)
    g_hi = g_all.astype(BF16)
    g_lo = (g_all - g_hi.astype(F32)).astype(BF16)
    gc_all = _dot(tri_ref[...], g_hi) + _dot(tri_ref[...], g_lo)
    gc_rows = gc_all.T

    lv = lv_ref[...]
    bl = lv.shape[0]
    incl = lv < NOT_SAME_CHUNK
    strict = jnp.logical_and(incl, lv >= 0.0)
    eye = jnp.where(lv == -1.0, 1.0, 0.0)
    lev0 = lv == 0.0
    n_levels = int(math.log2(chunk))
    run_levels = min(n_levels, max(1, math.ceil(math.log2(n_valid)))) if padded else n_levels
    heads = range(GDN_H)
    units = [(h, r) for r in range(0, lt, bl) for h in heads]

    gcol = [gc_all[:, GDN_H + h:GDN_H + h + 1] for h in heads]
    bcol = [beta_all[:, h:h + 1] for h in heads]
    eg = [jnp.exp(gcol[h]) for h in heads]
    q = [l2n(conv_act(h * GDN_D)) * (GDN_D ** -0.5) for h in heads]
    k = [l2n(conv_act(GDN_W + h * GDN_D)) for h in heads]
    rhs = [jnp.concatenate([conv_act(2 * GDN_W + h * GDN_D) * bcol[h], k[h] * (bcol[h] * eg[h])],
                           axis=1).astype(BF16) for h in heads]
    qb = [q[h].astype(BF16) for h in heads]
    kb = [k[h].astype(BF16) for h in heads]
    qg = [q[h] * eg[h] for h in heads]
    mm, aqk = [], []
    for h, r in units:
        decay = jnp.where(incl, jnp.exp(gcol[h][r:r + bl] - gc_rows[GDN_H + h:GDN_H + h + 1, r:r + bl]), 0.0)
        mm.append(jnp.where(strict, bcol[h][r:r + bl] * _dot_nt(kb[h][r:r + bl], kb[h][r:r + bl]) * decay, 0.0))
        aqk.append((_dot_nt(qb[h][r:r + bl], kb[h][r:r + bl]) * decay).astype(BF16))
    x = [eye - jnp.where(lev0, m, 0.0) for m in mm]
    mmb = [m.astype(BF16) for m in mm]
    for lev in range(1, run_levels):
        xb = [xi.astype(BF16) for xi in x]
        y = [_dot(mmb[i] * lvl_ref[lev], xb[i]).astype(BF16) for i in range(len(units))]
        x = [x[i] - _dot(xb[i], y[i]) for i in range(len(units))]
    uwb_u = [_dot(x[i].astype(BF16), rhs[h][r:r + bl]).astype(BF16) for i, (h, r) in enumerate(units)]
    ou_u = [_dot(aqk[i], uwb_u[i]) for i in range(len(units))]
    per_head = lambda vals, h: jnp.concatenate([vals[i] for i, (hh, _) in enumerate(units) if hh == h], axis=0)
    uwb = [per_head(uwb_u, h) for h in heads]
    ou = [per_head(ou_u, h) for h in heads]
    qp = [(qg[h] - ou[h][:, GDN_D:]).astype(BF16) for h in heads]
    for c in range(lt // chunk):
        r0, r1 = c * chunk, (c + 1) * chunk
        for h in heads:
            glast = gcol[h][r1 - 1:r1, :]
            kg = (k[h][r0:r1] * jnp.exp(glast - gcol[h][r0:r1])).astype(BF16)
            ab = _dot_tn(uwb[h][r0:r1, :], kg)
            st = s_scr[h]
            stb = st.astype(BF16)
            o = _dot_nt(qp[h][r0:r1], stb) + ou[h][r0:r1, :GDN_D]
            s_scr[h] = st * jnp.exp(glast) + ab[:GDN_D] - _dot(stb, ab[GDN_D:].astype(BF16))
            gate = gate_src[r0:r1, h * GDN_D:(h + 1) * GDN_D].astype(F32)
            on = _rms(o, gn_ref[...]) * _silu(gate)
            if padded:
                if r0 < n_valid:
                    o_ref[0, :, h * GDN_D:(h + 1) * GDN_D] = on[0:n_valid].astype(o_ref.dtype)
            else:
                o_ref[r0:r1, h * GDN_D:(h + 1) * GDN_D] = on.astype(o_ref.dtype)

    @pl.when(j == pl.num_programs(1) - 1)
    def _():
        for h in range(GDN_H):
            sout_ref[0, h] = s_scr[h].T


def _chunk_level_codes(lt, chunk):
    i = jnp.arange(lt)[:, None]
    j = jnp.arange(lt)[None, :]
    code = jnp.full((lt, lt), NOT_SAME_CHUNK, F32)
    for lev in reversed(range(int(math.log2(chunk)))):
        code = jnp.where((i // (2 << lev)) == (j // (2 << lev)), float(lev), code)
    code = jnp.where(i > j, code, NOT_SAME_CHUNK)
    return jnp.where(i == j, -1.0, code).astype(F32)


def _gdn(qkv, gate, ba, cw, alog, dtb, gn, s0, cb, *, nb, seq, lt, layer):
    padded = seq < lt
    n_valid = seq if padded else lt
    nt = 1 if padded else seq // lt
    chunk = GDN_CHUNK
    i = jnp.arange(lt)
    tri = ((i[:, None] // chunk == i[None, :] // chunk) & (i[:, None] >= i[None, :])).astype(BF16)
    bl = min(lt, 2 * chunk)
    lv = _chunk_level_codes(bl, chunk)
    n_levels = int(math.log2(chunk))
    lvl = jnp.stack([(lv == float(lev)).astype(BF16) for lev in range(n_levels)], axis=0)
    if padded:
        tile = lambda c: pl.BlockSpec((1, seq, c), lambda b, j: (b, 0, 0))
        out_o = jax.ShapeDtypeStruct((nb, seq, GDN_W), BF16)
        scratch = [pltpu.VMEM((lt, GDN_W), F32), pltpu.VMEM((lt, LANE), F32)]
    else:
        tile = lambda c: pl.BlockSpec((lt, c), lambda b, j: (b * nt + j, 0))
        out_o = jax.ShapeDtypeStruct((nb * seq, GDN_W), BF16)
        scratch = []
    return pl.pallas_call(
        functools.partial(_gdn_kernel, lt=lt, n_valid=n_valid, chunk=chunk),
        grid=(nb, nt),
        in_specs=[tile(3 * GDN_W), tile(GDN_W), tile(LANE),
                  _layer_spec(cw, layer), _layer_spec(alog, layer), _layer_spec(dtb, layer),
                  _layer_spec(gn, layer),
                  pl.BlockSpec((1, GDN_H, GDN_D, GDN_D), lambda b, j: (b, 0, 0, 0)),
                  pl.BlockSpec((1, GDN_CONV - 1, 3 * GDN_W), lambda b, j: (b, 0, 0)),
                  _const_spec((lt, lt)), _const_spec((bl, bl)), _const_spec((n_levels, bl, bl))],
        out_specs=[tile(GDN_W), pl.BlockSpec((1, GDN_H, GDN_D, GDN_D), lambda b, j: (b, 0, 0, 0))],
        out_shape=[out_o, jax.ShapeDtypeStruct((nb, GDN_H, GDN_D, GDN_D), F32)],
        scratch_shapes=[pltpu.VMEM((lt + 8, 3 * GDN_W), F32), pltpu.VMEM((GDN_H, GDN_D, GDN_D), F32)] + scratch,
        compiler_params=_cparams("parallel", "arbitrary"),
        name="gdn",
    )(qkv, gate, ba, cw, alog, dtb, gn, s0, cb, tri, lv, lvl)


def _lambda(l1q, l1k, l2q, l2k, lam_init):
    return (jnp.exp(jnp.sum(l1q[...] * l1k[...], axis=-1, keepdims=True))
            - jnp.exp(jnp.sum(l2q[...] * l2k[...], axis=-1, keepdims=True)) + lam_init)


def _flash_kernel(qi_tab, ki_tab, q_ref, k_ref, vt_ref, l1q, l1k, l2q, l2k, dn_ref, o_ref,
                  m_scr, acc_scr, *, tq, tk, lam_init):
    p = pl.program_id(1)
    qi, ki = qi_tab[p], ki_tab[p]

    @pl.when(ki == 0)
    def _():
        m_scr[...] = jnp.full(m_scr.shape, -jnp.inf, F32)
        acc_scr[...] = jnp.zeros(acc_scr.shape, F32)

    lane = lax.broadcasted_iota(jnp.int32, (1, DIFF_DV), 1)
    pairs = range(2 * DIFF_H)
    ones = jnp.ones((acc_scr.shape[1] - DIFF_DV, tk), BF16)

    def step(masked):
        kc, vt = [], []
        for h in range(DIFF_H):
            hs = slice(h * DIFF_DV, (h + 1) * DIFF_DV)
            kh = k_ref[:, hs]
            vt.append(jnp.concatenate([vt_ref[hs, :], ones], axis=0))
            kc += [jnp.where((lane // DIFF_DH) == c, kh, jnp.zeros_like(kh)) for c in range(2)]
        st = [_dot_nt(kc[i], q_ref[:, (i // 2) * DIFF_DV:(i // 2 + 1) * DIFF_DV]) for i in pairs]
        if masked:
            keys = ki * tk + lax.broadcasted_iota(jnp.int32, (tk, tq), 0)
            queries = qi * tq + lax.broadcasted_iota(jnp.int32, (tk, tq), 1)
            visible = keys <= queries
            st = [jnp.where(visible, st[i], -jnp.inf) for i in pairs]
        m_old = [m_scr[i:i + 1, :] for i in pairs]
        m_new = [jnp.maximum(m_old[i], jnp.max(st[i], axis=0, keepdims=True)) for i in pairs]
        pt = [jnp.exp2(st[i] - m_new[i]).astype(BF16) for i in pairs]
        alpha = [jnp.exp2(m_old[i] - m_new[i]) for i in pairs]
        for i in pairs:
            acc_scr[i] = alpha[i] * acc_scr[i] + _dot(vt[i // 2], pt[i])
            m_scr[i:i + 1, :] = m_new[i]

    @pl.when(ki < qi)
    def _():
        step(False)

    @pl.when(ki == qi)
    def _():
        step(True)
        lam = _lambda(l1q, l1k, l2q, l2k, lam_init)
        for h in range(DIFF_H):
            a0, a1 = acc_scr[2 * h], acc_scr[2 * h + 1]
            ot = (a0[:DIFF_DV] / a0[DIFF_DV:DIFF_DV + 1]
                  - lam * (a1[:DIFF_DV] / a1[DIFF_DV:DIFF_DV + 1]))
            o_ref[:, h * DIFF_DV:(h + 1) * DIFF_DV] = (_rms(ot.T, dn_ref[...]) * (1.0 - lam_init)).astype(o_ref.dtype)


def _flash(q, k, v, lams, dn, *, nb, seq, tq, lam_init, layer):
    nq = seq // tq
    pairs = [(a, b) for a in range(nq) for b in range(a + 1)]
    qi_tab = jnp.array([a for a, _ in pairs], jnp.int32)
    ki_tab = jnp.array([b for _, b in pairs], jnp.int32)
    qmap = lambda b, p, qt, kt: (b * nq + qt[p], 0)
    kmap = lambda b, p, qt, kt: (b * nq + kt[p], 0)
    cmap = lambda b, p, qt, kt: (0, 0)
    grid_spec = pltpu.PrefetchScalarGridSpec(
        num_scalar_prefetch=2,
        grid=(nb, len(pairs)),
        in_specs=[pl.BlockSpec((tq, DIFF_W), qmap), pl.BlockSpec((tq, DIFF_W), kmap),
                  pl.BlockSpec((None, DIFF_W, tq), lambda b, p, qt, kt: (b, 0, kt[p]))]
                 + [_layer_spec(a, layer) for a in (*lams, dn)],
        out_specs=pl.BlockSpec((tq, DIFF_W), qmap),
        scratch_shapes=[pltpu.VMEM((2 * DIFF_H, tq), F32),
                        pltpu.VMEM((2 * DIFF_H, DIFF_DV + 16, tq), F32)],
    )
    return pl.pallas_call(
        functools.partial(_flash_kernel, tq=tq, tk=tq, lam_init=lam_init),
        grid_spec=grid_spec,
        out_shape=jax.ShapeDtypeStruct((nb * seq, DIFF_W), BF16),
        compiler_params=_cparams("parallel", "arbitrary"),
        name="diff_flash",
    )(qi_tab, ki_tab, q, k, v, *lams, dn)


def _decode_kernel(pt_ref, q_ref, qmask_ref, kn_ref, vn_ref, expand_ref, hmask_ref, l1q, l1k, l2q, l2k, dn_ref,
                   *rest, n_new, ppg, lam_init):
    kpages, vpages = rest[:ppg], rest[ppg:2 * ppg]
    o_ref = rest[2 * ppg]
    qb_scr, kpad_scr, vpad_scr, m_scr, l_scr, acc_scr = rest[2 * ppg + 1:]
    g = pl.program_id(1)
    nrow = 2 * DIFF_H * 8

    @pl.when(g == 0)
    def _():
        kpad_scr[...] = jnp.zeros(kpad_scr.shape, F32)
        kpad_scr[0:n_new, :] = q_ref[0].astype(F32)
        q8 = kpad_scr[0:8, :]
        qb_scr[...] = (jnp.concatenate([q8] * (2 * DIFF_H), axis=0) * qmask_ref[...]).astype(BF16)
        m_scr[...] = jnp.full(m_scr.shape, -jnp.inf, F32)
        l_scr[...] = jnp.zeros(l_scr.shape, F32)
        acc_scr[...] = jnp.zeros(acc_scr.shape, F32)

    def update(s, vals):
        m_old = m_scr[...]
        m_new = jnp.maximum(m_old, jnp.max(s, axis=-1, keepdims=True))
        alpha = jnp.exp2(m_old - m_new)
        pr = jnp.exp2(s - m_new)
        l_scr[...] = alpha * l_scr[...] + jnp.sum(pr, axis=-1, keepdims=True)
        prb = pr.astype(BF16)
        n = len(vals)
        stacked = prb if n == 1 else jnp.concatenate([prb[:, jj * PAGE:(jj + 1) * PAGE] for jj in range(n)], axis=0)
        spread = _dot(stacked, expand_ref[...])
        pes = [(spread[jj * nrow:(jj + 1) * nrow] * hmask_ref[...]).astype(BF16) for jj in range(n)]
        terms = [_dot(pes[jj], vals[jj]) for jj in range(n)]
        while len(terms) > 1:
            terms = [terms[i] + terms[i + 1] for i in range(0, len(terms) - 1, 2)] + terms[len(terms) & ~1:]
        acc_scr[...] = alpha * acc_scr[...] + terms[0]
        m_scr[...] = m_new

    qb = qb_scr[...]
    s = jnp.concatenate([_dot(qb, kp[...].astype(BF16)) for kp in kpages], axis=1)
    update(s, [vp[...].astype(BF16) for vp in vpages])

    @pl.when(g == pl.num_programs(1) - 1)
    def _():
        kpad_scr[...] = jnp.zeros(kpad_scr.shape, F32)
        kpad_scr[0:n_new, :] = kn_ref[0]
        vpad_scr[...] = jnp.zeros(vpad_scr.shape, F32)
        vpad_scr[0:n_new * DIFF_H, :] = vn_ref[0]
        tok = lax.broadcasted_iota(jnp.int32, (nrow, PAGE), 0) % 8
        key = lax.broadcasted_iota(jnp.int32, (nrow, PAGE), 1)
        ok = jnp.logical_and(key < n_new, key <= tok)
        update(jnp.where(ok, _dot_nt(qb, kpad_scr[...].astype(BF16)), -jnp.inf), [vpad_scr[...].astype(BF16)])
        lam = _lambda(l1q, l1k, l2q, l2k, lam_init)
        on = acc_scr[...] / l_scr[...]
        half = nrow // 2
        for h in range(DIFF_H):
            o = on[8 * h:8 * h + 8, :] - lam * on[half + 8 * h:half + 8 * h + 8, :]
            o_ref[0, :, h * DIFF_DV:(h + 1) * DIFF_DV] = (_rms(o, dn_ref[...]) * (1.0 - lam_init)).astype(o_ref.dtype)


def _decode(page_table, q, kn, vn, cache_k, cache_v, lams, dn, *, layer, ppg, lam_init):
    nb, n_new, _ = q.shape
    n_pages = page_table.shape[1]
    nrow = 2 * DIFF_H * 8
    r = jnp.arange(nrow)
    hc = 2 * ((r // 8) % DIFF_H) + r // (nrow // 2)
    qmask = (jnp.arange(DIFF_W)[None, :] // DIFF_DH == hc[:, None]).astype(F32)
    vrows = PAGE * DIFF_H
    jj = jnp.arange(vrows)
    expand = (jj[None, :] // DIFF_H == jnp.arange(PAGE)[:, None]).astype(BF16)
    hmask = (jj[None, :] % DIFF_H == ((r // 8) % DIFF_H)[:, None]).astype(F32)
    per_b = lambda n, c: pl.BlockSpec((1, n, c), lambda b, g, pt: (b, 0, 0))
    cmap = lambda b, g, pt: (0, 0)

    def page_spec(j, rows, cols):
        return pl.BlockSpec((None, None, rows, cols),
                            lambda b, g, pt: (layer, pt[b * n_pages + g * ppg + j], 0, 0))

    grid_spec = pltpu.PrefetchScalarGridSpec(
        num_scalar_prefetch=1,
        grid=(nb, n_pages // ppg),
        in_specs=[per_b(n_new, DIFF_W), pl.BlockSpec((nrow, DIFF_W), cmap), per_b(n_new, DIFF_W),
                  per_b(n_new * DIFF_H, DIFF_DV), pl.BlockSpec((PAGE, vrows), cmap),
                  pl.BlockSpec((nrow, vrows), cmap)]
                 + [_layer_spec(a, layer) for a in (*lams, dn)]
                 + [page_spec(j, DIFF_W, PAGE) for j in range(ppg)]
                 + [page_spec(j, vrows, DIFF_DV) for j in range(ppg)],
        out_specs=pl.BlockSpec((1, 8, DIFF_W), lambda b, g, pt: (b, 0, 0)),
        scratch_shapes=[pltpu.VMEM((nrow, DIFF_W), BF16), pltpu.VMEM((PAGE, DIFF_W), F32),
                        pltpu.VMEM((vrows, DIFF_DV), F32),
                        pltpu.VMEM((nrow, 1), F32), pltpu.VMEM((nrow, 1), F32),
                        pltpu.VMEM((nrow, DIFF_DV), F32)],
    )
    ck = cache_k.transpose(0, 1, 3, 4, 2).reshape(cache_k.shape[0], cache_k.shape[1], DIFF_W, PAGE)
    cv = cache_v.reshape(cache_v.shape[0], cache_v.shape[1], vrows, DIFF_DV)
    out = pl.pallas_call(
        functools.partial(_decode_kernel, n_new=n_new, ppg=ppg, lam_init=lam_init),
        grid_spec=grid_spec,
        out_shape=jax.ShapeDtypeStruct((nb, 8, DIFF_W), BF16),
        compiler_params=_cparams("parallel", "arbitrary"),
        name="diff_decode",
    )(page_table.reshape(-1), q, qmask, kn, vn.reshape(nb, n_new * DIFF_H, DIFF_DV), expand, hmask, *lams, dn,
      *([ck] * ppg), *([cv] * ppg))
    return out[:, :n_new]


def _head_rms(z, gain, width):
    outs = [_rms(z[:, a:a + width], gain) for a in range(0, z.shape[-1], width)]
    return jnp.concatenate(outs, axis=1)


def _mixout_kernel(x_ref, og_ref, od_ref, wo_ref, nc_ref, wq_ref, qn_ref, x1_ref, qc_ref):
    mix = jnp.concatenate([og_ref[...], od_ref[...]], axis=1)
    x1 = x_ref[...] + _dot(mix, wo_ref[...])
    x1_ref[...] = x1
    hc = _rms(x1, nc_ref[...]).astype(BF16)
    qc = _head_rms(_dot(hc, wq_ref[...]), qn_ref[...], CA_DH) * (CA_DH ** -0.5)
    qc_ref[...] = qc.astype(qc_ref.dtype)


def _mixout(x, og, od, wo, nc, wq, qn, *, tm, q_dtype, layer):
    m, d = x.shape
    row = lambda i: (i, 0)
    return pl.pallas_call(
        _mixout_kernel,
        grid=(m // tm,),
        in_specs=[pl.BlockSpec((tm, d), row), pl.BlockSpec((tm, GDN_W), row), pl.BlockSpec((tm, DIFF_W), row),
                  _weight_spec(wo.shape, layer), _layer_spec(nc, layer), _weight_spec(wq.shape, layer),
                  _layer_spec(qn, layer)],
        out_specs=[pl.BlockSpec((tm, d), row), pl.BlockSpec((tm, CA_W), row)],
        out_shape=[jax.ShapeDtypeStruct((m, d), F32), jax.ShapeDtypeStruct((m, CA_W), q_dtype)],
        compiler_params=_cparams("parallel"),
        name="mixout",
    )(x, og, od, wo, nc, wq, qn)


def _memkv_kernel(m_ref, nm_ref, wk_ref, wv_ref, kn_ref, mk_ref, mv_ref):
    mn = _rms(m_ref[...], nm_ref[...]).astype(BF16)
    mk_ref[...] = _head_rms(_dot(mn, wk_ref[...]), kn_ref[...], CA_DH)
    mv_ref[...] = _dot(mn, wv_ref[...])


def _memkv(mem, nm, wk, wv, kn, *, tm, layer):
    m, d = mem.shape
    row = lambda i: (i, 0)
    return pl.pallas_call(
        _memkv_kernel,
        grid=(m // tm,),
        in_specs=[pl.BlockSpec((tm, d), row), _layer_spec(nm, layer), _weight_spec(wk.shape, layer),
                  _weight_spec(wv.shape, layer), _layer_spec(kn, layer)],
        out_specs=[pl.BlockSpec((tm, CA_W), row)] * 2,
        out_shape=[jax.ShapeDtypeStruct((m, CA_W), F32)] * 2,
        compiler_params=_cparams("parallel"),
        name="memkv",
    )(mem, nm, wk, wv, kn)


def _memory_attend(qs, ks, vs):
    s = [_dot_nt(q, k) for q, k in zip(qs, ks)]
    e = [jnp.exp(si - jnp.max(si, axis=-1, keepdims=True)) for si in s]
    pr = [(ei / jnp.sum(ei, axis=-1, keepdims=True)).astype(BF16) for ei in e]
    return [_dot(p, v) for p, v in zip(pr, vs)]


def _head_cols(h):
    return slice(h * CA_DH, (h + 1) * CA_DH)


def _cross_kernel(q_ref, mk_ref, mv_ref, o_ref, pad_scr, *, n_valid):
    groups = range(q_ref.shape[0])
    pad_scr[...] = jnp.zeros(pad_scr.shape, F32)
    for g in groups:
        pad_scr[g, 0:n_valid, :] = q_ref[g].astype(F32)
    q = [pad_scr[g].astype(BF16) for g in groups]
    units = [(g, h) for g in groups for h in range(CA_H)]
    outs = _memory_attend([q[g][:, _head_cols(h)] for g, h in units],
                          [mk_ref[g, :, h, :].astype(BF16) for g, h in units],
                          [mv_ref[g, :, h, :].astype(BF16) for g, h in units])
    for (g, h), oh in zip(units, outs):
        o_ref[g, :, _head_cols(h)] = oh[0:n_valid].astype(o_ref.dtype)


def _cross(q, mk, mv, *, layer, group):
    nb, seq, _ = q.shape
    mem_spec = pl.BlockSpec((None, group, mk.shape[2], CA_H, CA_DH), lambda b: (layer, b, 0, 0, 0))
    q_spec = pl.BlockSpec((group, seq, CA_W), lambda b: (b, 0, 0))
    return pl.pallas_call(
        functools.partial(_cross_kernel, n_valid=seq),
        grid=(nb // group,),
        in_specs=[q_spec, mem_spec, mem_spec],
        out_specs=q_spec,
        out_shape=jax.ShapeDtypeStruct((nb, seq, CA_W), BF16),
        scratch_shapes=[pltpu.VMEM((group, 8, CA_W), F32)],
        compiler_params=_cparams("parallel"),
        name="cross_attn",
    )(q, mk, mv)


def _ffn_kernel(x_ref, oc_ref, *rest, tm, stride, ncol, attend):
    if attend:
        mk_ref, mv_ref = rest[:2]
        rest = rest[2:]
    wco_ref, nf_ref, wg_ref, wu_ref, cw_ref, wd_ref, hist_ref, y_ref, hout_ref, hbuf, gbuf, acc = rest
    j = pl.program_id(1)
    d_ff = wg_ref.shape[1]
    hrows = (FFN_CONV - 1) * stride
    base = hbuf.shape[0]

    @pl.when(j == 0)
    def _():
        hbuf[base - hrows:base, :] = hist_ref[0]

    if attend:
        q = oc_ref[...]
        heads = range(CA_H)
        oc = jnp.concatenate(_memory_attend([q[:, _head_cols(h)] for h in heads],
                                            [mk_ref[0, :, _head_cols(h)].astype(BF16) for h in heads],
                                            [mv_ref[0, :, _head_cols(h)].astype(BF16) for h in heads]),
                             axis=1).astype(BF16)
    else:
        oc = oc_ref[...]
    x2 = x_ref[...] + _dot(oc, wco_ref[...])
    hf = _rms(x2, nf_ref[...]).astype(BF16)
    acc[...] = x2
    for c0 in range(0, d_ff, ncol):
        cs = slice(c0, c0 + ncol)
        gbuf[base - hrows:base, :] = hbuf[base - hrows:base, cs]
        gbuf[base:base + tm, :] = _dot(hf, wg_ref[:, cs])
        hbuf[base - hrows:base, cs] = gbuf[base + tm - hrows:base + tm, :]
        w = cw_ref[:, cs]
        gt = gbuf[base - hrows:base - hrows + tm, :] * w[0:1]
        for t in range(1, FFN_CONV):
            off = base - hrows + t * stride
            gt = gt + gbuf[off:off + tm, :] * w[t:t + 1]
        act = (_silu(gt) * _dot(hf, wu_ref[:, cs])).astype(BF16)
        acc[...] += _dot(act, wd_ref[cs, :])
    y_ref[...] = acc[...]
    hout_ref[0] = hbuf[base - hrows:base, :]


def _ffn(x, oc, wco, nf, wg, wu, cw, wd, hist, *, ngroups, tm, stride, ncol, layer, mem=None):
    m, d = x.shape
    d_ff = wg.shape[-1]
    nt = m // (ngroups * tm)
    hrows = (FFN_CONV - 1) * stride
    base = -(-hrows // 8) * 8
    row = lambda b, j: (b * nt + j, 0)
    grp = lambda b, j: (b, 0, 0)
    mem = tuple(mem or ())
    return pl.pallas_call(
        functools.partial(_ffn_kernel, tm=tm, stride=stride, ncol=ncol, attend=bool(mem)),
        grid=(ngroups, nt),
        in_specs=[pl.BlockSpec((tm, d), row), pl.BlockSpec((tm, CA_W), row)]
                 + [pl.BlockSpec((1,) + a.shape[1:], grp) for a in mem]
                 + [_weight_spec(wco.shape, layer),
                  _layer_spec(nf, layer), _weight_spec(wg.shape, layer), _weight_spec(wu.shape, layer),
                  _layer_spec(cw, layer), _weight_spec(wd.shape, layer),
                  pl.BlockSpec((1, hrows, d_ff), grp)],
        out_specs=[pl.BlockSpec((tm, d), row), pl.BlockSpec((1, hrows, d_ff), grp)],
        out_shape=[jax.ShapeDtypeStruct((m, d), F32), jax.ShapeDtypeStruct((ngroups, hrows, d_ff), F32)],
        scratch_shapes=[pltpu.VMEM((base, d_ff), F32), pltpu.VMEM((base + tm, ncol), F32),
                        pltpu.VMEM((tm, d), F32)],
        compiler_params=_cparams("parallel", "arbitrary"),
        name="ffn",
    )(x, oc, *mem, wco, nf, wg, wu, cw, wd, hist)


def _rotary_tables(pos):
    half = DIFF_DH // 2
    inv = ROPE_THETA ** (-jnp.arange(half, dtype=F32) / half)
    ang = pos.astype(F32)[:, None] * inv[None, :]
    cos, sin, zero = jnp.cos(ang), jnp.sin(ang), jnp.zeros_like(ang)
    reps = (1, LANE // DIFF_DH)
    return (jnp.tile(jnp.concatenate([cos, cos], axis=1), reps),
            jnp.tile(jnp.concatenate([-sin, zero], axis=1), reps),
            jnp.tile(jnp.concatenate([zero, sin], axis=1), reps))


def kernel(x_prompt, x_sample, mem_prompt, cache_k, cache_v, page_table, state_gdn, state_gdn_conv, cache_mem_k, cache_mem_v, state_ffn_conv, norm_mix, w_in, conv_qkv, a_log, dt_bias, gdn_norm, qnorm_diff, knorm_diff, lam_q1, lam_k1, lam_q2, lam_k2, diff_norm, w_out, norm_cross, norm_mem, w_cq, w_ck, w_cv, qnorm_cross, knorm_cross, w_co, norm_ffn, w_gate, w_up, conv_ffn, w_down):
    depth = w_in.shape[0]
    bp, lp, d = x_prompt.shape
    bs, ls, _ = x_sample.shape
    mlen = mem_prompt.shape[1]
    d_ff = w_gate.shape[2]
    past_len = page_table.shape[1] * PAGE
    mp, ms = bp * lp, bs * ls

    cos_p, sa_p, sb_p = _rotary_tables(jnp.arange(lp, dtype=jnp.int32))
    cos_s, sa_s, sb_s = [jnp.tile(t, (bs, 1)) for t in
                         _rotary_tables(past_len + jnp.arange(ls, dtype=jnp.int32))]
    ii = jnp.arange(DIFF_W)
    gmat = jnp.where(ii[:, None] // DIFF_DH == ii[None, :] // DIFF_DH, 1.0 / DIFF_DH, 0.0).astype(BF16)
    c_qkv, c_gate = 3 * GDN_W, 4 * GDN_W
    c_ba = c_gate + 2 * GDN_H

    xp = x_prompt.reshape(mp, d)
    xs = x_sample.reshape(ms, d)
    memp = mem_prompt.reshape(bp * mlen, d)
    zeros_s0 = jnp.zeros((bp, GDN_H, GDN_D, GDN_D), F32)
    zeros_cb = jnp.zeros((bp, GDN_CONV - 1, 3 * GDN_W), F32)
    zeros_fb = jnp.zeros((bp, FFN_CONV - 1, d_ff), F32)

    wb = {name: w.astype(BF16) for name, w in
          dict(out=w_out, cq=w_cq, ck=w_ck, cv=w_cv, co=w_co, gate=w_gate, up=w_up, down=w_down).items()}

    wb["in"] = jnp.concatenate([w_in[:, :, :c_gate], w_in[:, :, c_ba:], w_in[:, :, c_gate:c_ba],
                                jnp.zeros((depth, d, LANE - 2 * GDN_H), F32)], axis=2).astype(BF16)
    rows = lambda a: a.astype(F32)[:, None, :]
    lane_pad = lambda a: jnp.pad(rows(a), ((0, 0), (0, 0), (GDN_H, LANE - 2 * GDN_H)))
    head_tile = lambda a: jnp.tile(rows(a), (1, 1, 2 * DIFF_H))
    lams = (rows(lam_q1), rows(lam_k1), rows(lam_q2), rows(lam_k2))
    dn = rows(diff_norm)

    p_out, s_out, kv_stack = [], [], ()
    for l in range(depth):
        lam_init = 0.8 - 0.6 * math.exp(-0.3 * l)
        inproj = functools.partial(_inproj, gain=rows(norm_mix), w=wb["in"], qg=head_tile(qnorm_diff),
                                   kg=head_tile(knorm_diff), gmat=gmat, layer=l)
        gdn = functools.partial(_gdn, cw=conv_qkv, alog=lane_pad(a_log), dtb=lane_pad(dt_bias),
                                gn=rows(gdn_norm), layer=l)
        mixout = functools.partial(_mixout, wo=wb["out"], nc=rows(norm_cross), wq=wb["cq"],
                                   qn=rows(qnorm_cross), layer=l)
        ffn = functools.partial(_ffn, wco=wb["co"], nf=rows(norm_ffn), wg=wb["gate"], wu=wb["up"],
                                cw=conv_ffn, wd=wb["down"], layer=l)

        mk_p, mv_p = _memkv(memp, rows(norm_mem), wb["ck"], wb["cv"], rows(knorm_cross), tm=512, layer=l)
        qkv, gate, ba, q, kt, k16, v2, vt16 = inproj(xp, cos=cos_p, sa=sa_p, sb=sb_p, q_dtype=BF16, seq=lp,
                                                     tm=512 if kv_stack else 1024, prev=kv_stack)
        kv_stack = (kt, v2)
        og, sp = gdn(qkv, gate, ba, s0=zeros_s0, cb=zeros_cb, nb=bp, seq=lp, lt=256)
        od = _flash(q, k16, vt16, lams, dn, nb=bp, seq=lp, tq=512, lam_init=lam_init, layer=l)
        x1, qc = mixout(xp, og, od, tm=1024, q_dtype=BF16)
        xp, fp = ffn(x1, qc, hist=zeros_fb, ngroups=bp, tm=512, stride=1, ncol=1408,
                     mem=(mk_p.reshape(bp, mlen, CA_W), mv_p.reshape(bp, mlen, CA_W)))
        p_out.append((sp, qkv.reshape(bp, lp, c_qkv)[:, lp - (GDN_CONV - 1):],
                      mk_p.reshape(bp, mlen, CA_H, CA_DH), mv_p.reshape(bp, mlen, CA_H, CA_DH), fp))

        qkv, gate, ba, q, k, v = inproj(xs, cos=cos_s, sa=sa_s, sb=sb_s, tm=ms, q_dtype=F32)
        r3 = lambda a: a.reshape(bs, ls, a.shape[-1])
        og, ss = gdn(r3(qkv), r3(gate), r3(ba), s0=state_gdn[l], cb=state_gdn_conv[l], nb=bs, seq=ls,
                     lt=GDN_CHUNK)
        od = _decode(page_table, r3(q), r3(k), r3(v), cache_k, cache_v, lams, dn,
                     layer=l, ppg=math.gcd(page_table.shape[1], 32), lam_init=lam_init)
        x1, qc = mixout(xs, og.reshape(ms, GDN_W), od.reshape(ms, DIFF_W), tm=ms, q_dtype=F32)
        oc = _cross(r3(qc), cache_mem_k, cache_mem_v, layer=l, group=math.gcd(bs, 4))
        tmaj = lambda a: a.reshape(bs, ls, -1).transpose(1, 0, 2).reshape(ms, -1)
        hist = state_ffn_conv[l].transpose(1, 0, 2).reshape(1, (FFN_CONV - 1) * bs, d_ff)
        y_t, fs_t = ffn(tmaj(x1), tmaj(oc), hist=hist, ngroups=1, tm=ms, stride=bs, ncol=1408)
        xs = y_t.reshape(ls, bs, d).transpose(1, 0, 2).reshape(ms, d)
        fs = fs_t.reshape(FFN_CONV - 1, bs, d_ff).transpose(1, 0, 2)
        conv_s = jnp.concatenate([state_gdn_conv[l], r3(qkv)], axis=1)[:, ls:]
        s_out.append((k.reshape(bs, ls, 2 * DIFF_H, DIFF_DH), v.reshape(bs, ls, DIFF_H, DIFF_DV), ss, conv_s, fs))

    pg, pc, pmk, pmv, pf = [jnp.stack(t, axis=0) for t in zip(*p_out)]
    pk = kv_stack[0].reshape(depth, bp, 2 * DIFF_H, DIFF_DH, lp).transpose(0, 1, 4, 2, 3)
    pv = kv_stack[1].reshape(depth, bp, lp, DIFF_H, DIFF_DV)
    sk, sv, sg, sc, sf = [jnp.stack(t, axis=0) for t in zip(*s_out)]
    return (xp.reshape(bp, lp, d), xs.reshape(bs, ls, d), pk, pv, pg, pc, pmk, pmv, pf, sk, sv, sg, sc, sf)
```

```python
import functools
import math

import jax
import jax.numpy as jnp
from jax import lax
from jax.experimental import pallas as pl
from jax.experimental.pallas import tpu as pltpu

F32 = jnp.float32
BF16 = jnp.bfloat16

EPS = 1e-6
ROPE_THETA = 10000.0
GDN_H = 4
GDN_D = 128
GDN_W = GDN_H * GDN_D
GDN_CONV = 4
GDN_CHUNK = 64
DIFF_H = 4
DIFF_DH = 64
DIFF_DV = 128
DIFF_W = DIFF_H * DIFF_DV
CA_H = 4
CA_DH = 128
CA_W = CA_H * CA_DH
FFN_CONV = 3
PAGE = 128
LANE = 128
VMEM_LIMIT = 56 * 1024 * 1024
NOT_SAME_CHUNK = 99.0
LOG2E = math.log2(math.e)


def _cparams(*sem):
    return pltpu.CompilerParams(dimension_semantics=sem, vmem_limit_bytes=VMEM_LIMIT)


def _dot(a, b):
    return jnp.dot(a, b, preferred_element_type=F32)


def _dot_nt(a, b):
    return lax.dot_general(a, b, (((1,), (1,)), ((), ())), preferred_element_type=F32)


def _dot_tn(a, b):
    return lax.dot_general(a, b, (((0,), (0,)), ((), ())), preferred_element_type=F32)


def _sigmoid(x):
    return 1.0 / (1.0 + jnp.exp(-x))


def _silu(x):
    return x * _sigmoid(x)


def _softplus(x):
    return jnp.maximum(x, 0.0) + jnp.log1p(jnp.exp(-jnp.abs(x)))


def _rms(x, gain):
    return x * lax.rsqrt(jnp.mean(x * x, axis=-1, keepdims=True) + EPS) * gain


def _const_spec(shape):
    nd = len(shape)
    return pl.BlockSpec(shape, lambda *_: (0,) * nd)


def _layer_spec(a, layer):
    return pl.BlockSpec((None,) + tuple(a.shape[1:]), lambda *_: (layer, 0, 0))


def _weight_spec(shape, layer=None):
    if layer is None:
        nd = len(shape)
        return pl.BlockSpec(shape, lambda *_: (0,) * nd, pipeline_mode=pl.Buffered(1))
    nd = len(shape) - 1
    return pl.BlockSpec((None,) + tuple(shape[1:]), lambda *_: (layer,) + (0,) * nd, pipeline_mode=pl.Buffered(1))


def _inproj_kernel(x_ref, gain_ref, w_ref, qg_ref, kg_ref, cos_ref, sa_ref, sb_ref, gmat_ref, *refs, n_prev):
    prev_refs, (qkv_ref, gate_ref, ba_ref, q_ref, *kv_refs) = refs[:2 * (n_prev > 0)], refs[2 * (n_prev > 0):]
    h = _rms(x_ref[...], gain_ref[...]).astype(BF16)
    c_qkv, c_gate = 3 * GDN_W, 4 * GDN_W
    c_q, c_k, c_v = c_gate + DIFF_W, c_gate + 2 * DIFF_W, c_gate + 3 * DIFF_W

    def proj(a, b):
        return _dot_nt(h, w_ref[a:b, :])

    qkv_ref[...] = proj(0, c_qkv)
    gate_ref[...] = proj(c_qkv, c_gate).astype(gate_ref.dtype)
    v = proj(c_k, c_v)
    ba_ref[...] = proj(c_v, c_v + LANE)
    widen = lambda t: jnp.concatenate([t] * (DIFF_W // LANE), axis=1)
    cos, sa, sb = widen(cos_ref[...]), widen(sa_ref[...]), widen(sb_ref[...])
    half = DIFF_DH // 2

    def norm_rot(z, g):
        ms = _dot((z * z).astype(BF16), gmat_ref[...])
        y = z * lax.rsqrt(ms + EPS) * g
        return y * cos + pltpu.roll(y, DIFF_W - half, 1) * sa + pltpu.roll(y, half, 1) * sb

    q_ref[...] = (norm_rot(proj(c_gate, c_q), qg_ref[...]) * (LOG2E * DIFF_DH ** -0.5)).astype(q_ref.dtype)
    k = norm_rot(proj(c_q, c_k), kg_ref[...])
    if len(kv_refs) == 2:
        kv_refs[0][...] = k
        kv_refs[1][...] = v
    else:
        kt_ref, k16_ref, v2_ref, vt16_ref = kv_refs
        if n_prev:
            kt_ref[0:n_prev] = prev_refs[0][...]
            v2_ref[0:n_prev] = prev_refs[1][...]
        kt_ref[n_prev] = k.T
        k16_ref[...] = k.astype(BF16)
        vt16_ref[...] = v.T.astype(BF16)
        for hd in range(DIFF_H):
            v2_ref[n_prev, pl.ds(hd, v.shape[0], stride=DIFF_H), :] = v[:, hd * DIFF_DV:(hd + 1) * DIFF_DV]


def _inproj(x, gain, w, qg, kg, cos, sa, sb, gmat, *, tm, q_dtype, layer, seq=None, prev=()):
    m, d = x.shape
    npos = cos.shape[0] // tm
    row = lambda i: (i, 0)
    pos = lambda i: (i % npos, 0)
    n_prev = prev[0].shape[0] if prev else 0
    if seq is None:
        kv_specs = [pl.BlockSpec((tm, DIFF_W), row)] * 2
        kv_shapes = [jax.ShapeDtypeStruct((m, DIFF_W), F32)] * 2
        prev_specs = []
    else:
        nt, nb = seq // tm, m // seq
        kcols = lambda i: (0, i // nt, 0, i % nt)
        vrows = lambda i: (0, i // nt, i % nt, 0)
        kt_spec = lambda n: pl.BlockSpec((n, None, DIFF_W, tm), kcols)
        v2_spec = lambda n: pl.BlockSpec((n, None, tm * DIFF_H, DIFF_DV), vrows)
        kv_specs = [kt_spec(n_prev + 1), pl.BlockSpec((tm, DIFF_W), row), v2_spec(n_prev + 1),
                    pl.BlockSpec((None, DIFF_W, tm), lambda i: (i // nt, 0, i % nt))]
        kv_shapes = [jax.ShapeDtypeStruct((n_prev + 1, nb, DIFF_W, seq), F32),
                     jax.ShapeDtypeStruct((m, DIFF_W), BF16),
                     jax.ShapeDtypeStruct((n_prev + 1, nb, seq * DIFF_H, DIFF_DV), F32),
                     jax.ShapeDtypeStruct((nb, DIFF_W, seq), BF16)]
        prev_specs = [kt_spec(n_prev), v2_spec(n_prev)] if prev else []
    return pl.pallas_call(
        functools.partial(_inproj_kernel, n_prev=n_prev),
        grid=(m // tm,),
        in_specs=[pl.BlockSpec((tm, d), row), _layer_spec(gain, layer), _weight_spec(w.shape, layer),
                  _layer_spec(qg, layer), _layer_spec(kg, layer),
                  pl.BlockSpec((tm, LANE), pos), pl.BlockSpec((tm, LANE), pos),
                  pl.BlockSpec((tm, LANE), pos), _const_spec((DIFF_W, DIFF_W))] + prev_specs,
        out_specs=[pl.BlockSpec((tm, 3 * GDN_W), row), pl.BlockSpec((tm, GDN_W), row),
                   pl.BlockSpec((tm, LANE), row), pl.BlockSpec((tm, DIFF_W), row)] + kv_specs,
        out_shape=[jax.ShapeDtypeStruct((m, 3 * GDN_W), F32), jax.ShapeDtypeStruct((m, GDN_W), BF16),
                   jax.ShapeDtypeStruct((m, LANE), F32), jax.ShapeDtypeStruct((m, DIFF_W), q_dtype)] + kv_shapes,
        compiler_params=_cparams("parallel"),
        name="inproj",
    )(x, gain, w, qg, kg, cos, sa, sb, gmat, *prev)


def _gdn_kernel(x_ref, gate_ref, ba_ref, cw_ref, alog_ref, dtb_ref, gn_ref, s0_ref, cb_ref, tri_ref, lv_ref,
                lvl_ref, o_ref, sout_ref, xbuf, s_scr, *pad_scr, lt, n_valid, chunk):
    j = pl.program_id(1)
    padded = n_valid < lt
    tail = GDN_CONV - 1

    @pl.when(j == 0)
    def _():
        for h in range(GDN_H):
            s_scr[h] = s0_ref[0, h].T
        xbuf[0:8, :] = jnp.zeros((8, 3 * GDN_W), F32)
        xbuf[8 - tail:8, :] = cb_ref[0]

    @pl.when(j > 0)
    def _():
        xbuf[0:8, :] = xbuf[lt:lt + 8, :]

    if padded:
        gate_scr, ba_scr = pad_scr
        xbuf[8:8 + lt, :] = jnp.zeros((lt, 3 * GDN_W), F32)
        xbuf[8:8 + n_valid, :] = x_ref[0]
        gate_scr[...] = jnp.zeros(gate_scr.shape, F32)
        gate_scr[0:n_valid, :] = gate_ref[0].astype(F32)
        ba_scr[...] = jnp.zeros(ba_scr.shape, F32)
        ba_scr[0:n_valid, :] = ba_ref[0]
        gate_src, ba = gate_scr, ba_scr[...]
    else:
        xbuf[8:8 + lt, :] = x_ref[...]
        gate_src, ba = gate_ref, ba_ref[...]

    def conv_act(c0):
        w = cw_ref[:, c0:c0 + GDN_D]
        y = xbuf[8 - tail:8 - tail + lt, c0:c0 + GDN_D] * w[0:1]
        for t in range(1, GDN_CONV):
            y = y + xbuf[8 - tail + t:8 - tail + t + lt, c0:c0 + GDN_D] * w[t:t + 1]
        return _silu(y)

    def l2n(z):
        return z * lax.rsqrt(jnp.sum(z * z, axis=-1, keepdims=True) + EPS)

    beta_all = _sigmoid(ba)
    g_all = -jnp.exp(alog_ref[...]) * _softplus(ba + dtb_ref[...])
    if padded:
        valid = lax.broadcasted_iota(jnp.int32, (lt, 1), 0) < n_valid
        beta_all = jnp.where(valid, beta_all, 0.0)
        g_all = jnp.where(valid, g_all, 0.0)
    g_hi = g_all.astype(BF16)
    g_lo = (g_all - g_hi.astype(F32)).astype(BF16)
    gc_all = _dot(tri_ref[...], g_hi) + _dot(tri_ref[...], g_lo)
    gc_rows = gc_all.T

    lv = lv_ref[...]
    bl = lv.shape[0]
    incl = lv < NOT_SAME_CHUNK
    strict = jnp.logical_and(incl, lv >= 0.0)
    eye = jnp.where(lv == -1.0, 1.0, 0.0)
    lev0 = lv == 0.0
    n_levels = int(math.log2(chunk))
    run_levels = min(n_levels, max(1, math.ceil(math.log2(n_valid)))) if padded else n_levels
    heads = range(GDN_H)
    units = [(h, r) for r in range(0, lt, bl) for h in heads]

    gcol = [gc_all[:, GDN_H + h:GDN_H + h + 1] for h in heads]
    bcol = [beta_all[:, h:h + 1] for h in heads]
    eg = [jnp.exp(gcol[h]) for h in heads]
    q = [l2n(conv_act(h * GDN_D)) * (GDN_D ** -0.5) for h in heads]
    k = [l2n(conv_act(GDN_W + h * GDN_D)) for h in heads]
    rhs = [jnp.concatenate([conv_act(2 * GDN_W + h * GDN_D) * bcol[h], k[h] * (bcol[h] * eg[h])],
                           axis=1).astype(BF16) for h in heads]
    qb = [q[h].astype(BF16) for h in heads]
    kb = [k[h].astype(BF16) for h in heads]
    qg = [q[h] * eg[h] for h in heads]
    mm, aqk = [], []
    for h, r in units:
        decay = jnp.where(incl, jnp.exp(gcol[h][r:r + bl] - gc_rows[GDN_H + h:GDN_H + h + 1, r:r + bl]), 0.0)
        mm.append(jnp.where(strict, bcol[h][r:r + bl] * _dot_nt(kb[h][r:r + bl], kb[h][r:r + bl]) * decay, 0.0))
        aqk.append((_dot_nt(qb[h][r:r + bl], kb[h][r:r + bl]) * decay).astype(BF16))
    x = [eye - jnp.where(lev0, m, 0.0) for m in mm]
    mmb = [m.astype(BF16) for m in mm]
    for lev in range(1, run_levels):
        xb = [xi.astype(BF16) for xi in x]
        y = [_dot(mmb[i] * lvl_ref[lev], xb[i]).astype(BF16) for i in range(len(units))]
        x = [x[i] - _dot(xb[i], y[i]) for i in range(len(units))]
    uwb_u = [_dot(x[i].astype(BF16), rhs[h][r:r + bl]).astype(BF16) for i, (h, r) in enumerate(units)]
    ou_u = [_dot(aqk[i], uwb_u[i]) for i in range(len(units))]
    per_head = lambda vals, h: jnp.concatenate([vals[i] for i, (hh, _) in enumerate(units) if hh == h], axis=0)
    uwb = [per_head(uwb_u, h) for h in heads]
    ou = [per_head(ou_u, h) for h in heads]
    qp = [(qg[h] - ou[h][:, GDN_D:]).astype(BF16) for h in heads]
    for c in range(lt // chunk):
        r0, r1 = c * chunk, (c + 1) * chunk
        for h in heads:
            glast = gcol[h][r1 - 1:r1, :]
            kg = (k[h][r0:r1] * jnp.exp(glast - gcol[h][r0:r1])).astype(BF16)
            ab = _dot_tn(uwb[h][r0:r1, :], kg)
            st = s_scr[h]
            stb = st.astype(BF16)
            o = _dot_nt(qp[h][r0:r1], stb) + ou[h][r0:r1, :GDN_D]
            s_scr[h] = st * jnp.exp(glast) + ab[:GDN_D] - _dot(stb, ab[GDN_D:].astype(BF16))
            gate = gate_src[r0:r1, h * GDN_D:(h + 1) * GDN_D].astype(F32)
            on = _rms(o, gn_ref[...]) * _silu(gate)
            if padded:
                if r0 < n_valid:
                    o_ref[0, :, h * GDN_D:(h + 1) * GDN_D] = on[0:n_valid].astype(o_ref.dtype)
            else:
                o_ref[r0:r1, h * GDN_D:(h + 1) * GDN_D] = on.astype(o_ref.dtype)

    @pl.when(j == pl.num_programs(1) - 1)
    def _():
        for h in range(GDN_H):
            sout_ref[0, h] = s_scr[h].T


def _chunk_level_codes(lt, chunk):
    i = jnp.arange(lt)[:, None]
    j = jnp.arange(lt)[None, :]
    code = jnp.full((lt, lt), NOT_SAME_CHUNK, F32)
    for lev in reversed(range(int(math.log2(chunk)))):
        code = jnp.where((i // (2 << lev)) == (j // (2 << lev)), float(lev), code)
    code = jnp.where(i > j, code, NOT_SAME_CHUNK)
    return jnp.where(i == j, -1.0, code).astype(F32)


def _gdn(qkv, gate, ba, cw, alog, dtb, gn, s0, cb, *, nb, seq, lt, layer):
    padded = seq < lt
    n_valid = seq if padded else lt
    nt = 1 if padded else seq // lt
    chunk = GDN_CHUNK
    i = jnp.arange(lt)
    tri = ((i[:, None] // chunk == i[None, :] // chunk) & (i[:, None] >= i[None, :])).astype(BF16)
    bl = min(lt, 2 * chunk)
    lv = _chunk_level_codes(bl, chunk)
    n_levels = int(math.log2(chunk))
    lvl = jnp.stack([(lv == float(lev)).astype(BF16) for lev in range(n_levels)], axis=0)
    if padded:
        tile = lambda c: pl.BlockSpec((1, seq, c), lambda b, j: (b, 0, 0))
        out_o = jax.ShapeDtypeStruct((nb, seq, GDN_W), BF16)
        scratch = [pltpu.VMEM((lt, GDN_W), F32), pltpu.VMEM((lt, LANE), F32)]
    else:
        tile = lambda c: pl.BlockSpec((lt, c), lambda b, j: (b * nt + j, 0))
        out_o = jax.ShapeDtypeStruct((nb * seq, GDN_W), BF16)
        scratch = []
    return pl.pallas_call(
        functools.partial(_gdn_kernel, lt=lt, n_valid=n_valid, chunk=chunk),
        grid=(nb, nt),
        in_specs=[tile(3 * GDN_W), tile(GDN_W), tile(LANE),
                  _layer_spec(cw, layer), _layer_spec(alog, layer), _layer_spec(dtb, layer),
                  _layer_spec(gn, layer),
                  pl.BlockSpec((1, GDN_H, GDN_D, GDN_D), lambda b, j: (b, 0, 0, 0)),
                  pl.BlockSpec((1, GDN_CONV - 1, 3 * GDN_W), lambda b, j: (b, 0, 0)),
                  _const_spec((lt, lt)), _const_spec((bl, bl)), _const_spec((n_levels, bl, bl))],
        out_specs=[tile(GDN_W), pl.BlockSpec((1, GDN_H, GDN_D, GDN_D), lambda b, j: (b, 0, 0, 0))],
        out_shape=[out_o, jax.ShapeDtypeStruct((nb, GDN_H, GDN_D, GDN_D), F32)],
        scratch_shapes=[pltpu.VMEM((lt + 8, 3 * GDN_W), F32), pltpu.VMEM((GDN_H, GDN_D, GDN_D), F32)] + scratch,
        compiler_params=_cparams("parallel", "arbitrary"),
        name="gdn",
    )(qkv, gate, ba, cw, alog, dtb, gn, s0, cb, tri, lv, lvl)


def _lambda(l1q, l1k, l2q, l2k, lam_init):
    return (jnp.exp(jnp.sum(l1q[...] * l1k[...], axis=-1, keepdims=True))
            - jnp.exp(jnp.sum(l2q[...] * l2k[...], axis=-1, keepdims=True)) + lam_init)


def _flash_kernel(qi_tab, ki_tab, q_ref, k_ref, vt_ref, l1q, l1k, l2q, l2k, dn_ref, o_ref,
                  m_scr, acc_scr, *, tq, tk, lam_init):
    p = pl.program_id(1)
    qi, ki = qi_tab[p], ki_tab[p]

    @pl.when(ki == 0)
    def _():
        m_scr[...] = jnp.full(m_scr.shape, -jnp.inf, F32)
        acc_scr[...] = jnp.zeros(acc_scr.shape, F32)

    lane = lax.broadcasted_iota(jnp.int32, (1, DIFF_DV), 1)
    pairs = range(2 * DIFF_H)
    ones = jnp.ones((acc_scr.shape[1] - DIFF_DV, tk), BF16)

    def step(masked):
        kc, vt = [], []
        for h in range(DIFF_H):
            hs = slice(h * DIFF_DV, (h + 1) * DIFF_DV)
            kh = k_ref[:, hs]
            vt.append(jnp.concatenate([vt_ref[hs, :], ones], axis=0))
            kc += [jnp.where((lane // DIFF_DH) == c, kh, jnp.zeros_like(kh)) for c in range(2)]
        st = [_dot_nt(kc[i], q_ref[:, (i // 2) * DIFF_DV:(i // 2 + 1) * DIFF_DV]) for i in pairs]
        if masked:
            keys = ki * tk + lax.broadcasted_iota(jnp.int32, (tk, tq), 0)
            queries = qi * tq + lax.broadcasted_iota(jnp.int32, (tk, tq), 1)
            visible = keys <= queries
            st = [jnp.where(visible, st[i], -jnp.inf) for i in pairs]
        m_old = [m_scr[i:i + 1, :] for i in pairs]
        m_new = [jnp.maximum(m_old[i], jnp.max(st[i], axis=0, keepdims=True)) for i in pairs]
        pt = [jnp.exp2(st[i] - m_new[i]).astype(BF16) for i in pairs]
        alpha = [jnp.exp2(m_old[i] - m_new[i]) for i in pairs]
        for i in pairs:
            acc_scr[i] = alpha[i] * acc_scr[i] + _dot(vt[i // 2], pt[i])
            m_scr[i:i + 1, :] = m_new[i]

    @pl.when(ki < qi)
    def _():
        step(False)

    @pl.when(ki == qi)
    def _():
        step(True)
        lam = _lambda(l1q, l1k, l2q, l2k, lam_init)
        for h in range(DIFF_H):
            a0, a1 = acc_scr[2 * h], acc_scr[2 * h + 1]
            ot = (a0[:DIFF_DV] / a0[DIFF_DV:DIFF_DV + 1]
                  - lam * (a1[:DIFF_DV] / a1[DIFF_DV:DIFF_DV + 1]))
            o_ref[:, h * DIFF_DV:(h + 1) * DIFF_DV] = (_rms(ot.T, dn_ref[...]) * (1.0 - lam_init)).astype(o_ref.dtype)


def _flash(q, k, v, lams, dn, *, nb, seq, tq, lam_init, layer):
    nq = seq // tq
    pairs = [(a, b) for a in range(nq) for b in range(a + 1)]
    qi_tab = jnp.array([a for a, _ in pairs], jnp.int32)
    ki_tab = jnp.array([b for _, b in pairs], jnp.int32)
    qmap = lambda b, p, qt, kt: (b * nq + qt[p], 0)
    kmap = lambda b, p, qt, kt: (b * nq + kt[p], 0)
    cmap = lambda b, p, qt, kt: (0, 0)
    grid_spec = pltpu.PrefetchScalarGridSpec(
        num_scalar_prefetch=2,
        grid=(nb, len(pairs)),
        in_specs=[pl.BlockSpec((tq, DIFF_W), qmap), pl.BlockSpec((tq, DIFF_W), kmap),
                  pl.BlockSpec((None, DIFF_W, tq), lambda b, p, qt, kt: (b, 0, kt[p]))]
                 + [_layer_spec(a, layer) for a in (*lams, dn)],
        out_specs=pl.BlockSpec((tq, DIFF_W), qmap),
        scratch_shapes=[pltpu.VMEM((2 * DIFF_H, tq), F32),
                        pltpu.VMEM((2 * DIFF_H, DIFF_DV + 16, tq), F32)],
    )
    return pl.pallas_call(
        functools.partial(_flash_kernel, tq=tq, tk=tq, lam_init=lam_init),
        grid_spec=grid_spec,
        out_shape=jax.ShapeDtypeStruct((nb * seq, DIFF_W), BF16),
        compiler_params=_cparams("parallel", "arbitrary"),
        name="diff_flash",
    )(qi_tab, ki_tab, q, k, v, *lams, dn)


def _decode_kernel(pt_ref, q_ref, qmask_ref, kn_ref, vn_ref, expand_ref, hmask_ref, l1q, l1k, l2q, l2k, dn_ref,
                   *rest, n_new, ppg, lam_init):
    kpages, vpages = rest[:ppg], rest[ppg:2 * ppg]
    o_ref = rest[2 * ppg]
    qb_scr, kpad_scr, vpad_scr, m_scr, l_scr, acc_scr = rest[2 * ppg + 1:]
    g = pl.program_id(1)
    nrow = 2 * DIFF_H * 8

    @pl.when(g == 0)
    def _():
        kpad_scr[...] = jnp.zeros(kpad_scr.shape, F32)
        kpad_scr[0:n_new, :] = q_ref[0].astype(F32)
        q8 = kpad_scr[0:8, :]
        qb_scr[...] = (jnp.concatenate([q8] * (2 * DIFF_H), axis=0) * qmask_ref[...]).astype(BF16)
        m_scr[...] = jnp.full(m_scr.shape, -jnp.inf, F32)
        l_scr[...] = jnp.zeros(l_scr.shape, F32)
        acc_scr[...] = jnp.zeros(acc_scr.shape, F32)

    def update(s, vals):
        m_old = m_scr[...]
        m_new = jnp.maximum(m_old, jnp.max(s, axis=-1, keepdims=True))
        alpha = jnp.exp2(m_old - m_new)
        pr = jnp.exp2(s - m_new)
        l_scr[...] = alpha * l_scr[...] + jnp.sum(pr, axis=-1, keepdims=True)
        prb = pr.astype(BF16)
        n = len(vals)
        stacked = prb if n == 1 else jnp.concatenate([prb[:, jj * PAGE:(jj + 1) * PAGE] for jj in range(n)], axis=0)
        spread = _dot(stacked, expand_ref[...])
        pes = [(spread[jj * nrow:(jj + 1) * nrow] * hmask_ref[...]).astype(BF16) for jj in range(n)]
        terms = [_dot(pes[jj], vals[jj]) for jj in range(n)]
        while len(terms) > 1:
            terms = [terms[i] + terms[i + 1] for i in range(0, len(terms) - 1, 2)] + terms[len(terms) & ~1:]
        acc_scr[...] = alpha * acc_scr[...] + terms[0]
        m_scr[...] = m_new

    qb = qb_scr[...]
    s = jnp.concatenate([_dot(qb, kp[...].astype(BF16)) for kp in kpages], axis=1)
    update(s, [vp[...].astype(BF16) for vp in vpages])

    @pl.when(g == pl.num_programs(1) - 1)
    def _():
        kpad_scr[...] = jnp.zeros(kpad_scr.shape, F32)
        kpad_scr[0:n_new, :] = kn_ref[0]
        vpad_scr[...] = jnp.zeros(vpad_scr.shape, F32)
        vpad_scr[0:n_new * DIFF_H, :] = vn_ref[0]
        tok = lax.broadcasted_iota(jnp.int32, (nrow, PAGE), 0) % 8
        key = lax.broadcasted_iota(jnp.int32, (nrow, PAGE), 1)
        ok = jnp.logical_and(key < n_new, key <= tok)
        update(jnp.where(ok, _dot_nt(qb, kpad_scr[...].astype(BF16)), -jnp.inf), [vpad_scr[...].astype(BF16)])
        lam = _lambda(l1q, l1k, l2q, l2k, lam_init)
        on = acc_scr[...] / l_scr[...]
        half = nrow // 2
        for h in range(DIFF_H):
            o = on[8 * h:8 * h + 8, :] - lam * on[half + 8 * h:half + 8 * h + 8, :]
            o_ref[0, :, h * DIFF_DV:(h + 1) * DIFF_DV] = (_rms(o, dn_ref[...]) * (1.0 - lam_init)).astype(o_ref.dtype)


def _decode(page_table, q, kn, vn, cache_k, cache_v, lams, dn, *, layer, ppg, lam_init):
    nb, n_new, _ = q.shape
    n_pages = page_table.shape[1]
    nrow = 2 * DIFF_H * 8
    r = jnp.arange(nrow)
    hc = 2 * ((r // 8) % DIFF_H) + r // (nrow // 2)
    qmask = (jnp.arange(DIFF_W)[None, :] // DIFF_DH == hc[:, None]).astype(F32)
    vrows = PAGE * DIFF_H
    jj = jnp.arange(vrows)
    expand = (jj[None, :] // DIFF_H == jnp.arange(PAGE)[:, None]).astype(BF16)
    hmask = (jj[None, :] % DIFF_H == ((r // 8) % DIFF_H)[:, None]).astype(F32)
    per_b = lambda n, c: pl.BlockSpec((1, n, c), lambda b, g, pt: (b, 0, 0))
    cmap = lambda b, g, pt: (0, 0)

    def page_spec(j, rows, cols):
        return pl.BlockSpec((None, None, rows, cols),
                            lambda b, g, pt: (layer, pt[b * n_pages + g * ppg + j], 0, 0))

    grid_spec = pltpu.PrefetchScalarGridSpec(
        num_scalar_prefetch=1,
        grid=(nb, n_pages // ppg),
        in_specs=[per_b(n_new, DIFF_W), pl.BlockSpec((nrow, DIFF_W), cmap), per_b(n_new, DIFF_W),
                  per_b(n_new * DIFF_H, DIFF_DV), pl.BlockSpec((PAGE, vrows), cmap),
                  pl.BlockSpec((nrow, vrows), cmap)]
                 + [_layer_spec(a, layer) for a in (*lams, dn)]
                 + [page_spec(j, DIFF_W, PAGE) for j in range(ppg)]
                 + [page_spec(j, vrows, DIFF_DV) for j in range(ppg)],
        out_specs=pl.BlockSpec((1, 8, DIFF_W), lambda b, g, pt: (b, 0, 0)),
        scratch_shapes=[pltpu.VMEM((nrow, DIFF_W), BF16), pltpu.VMEM((PAGE, DIFF_W), F32),
                        pltpu.VMEM((vrows, DIFF_DV), F32),
                        pltpu.VMEM((nrow, 1), F32), pltpu.VMEM((nrow, 1), F32),
                        pltpu.VMEM((nrow, DIFF_DV), F32)],
    )
    ck = cache_k.transpose(0, 1, 3, 4, 2).reshape(cache_k.shape[0], cache_k.shape[1], DIFF_W, PAGE)
    cv = cache_v.reshape(cache_v.shape[0], cache_v.shape[1], vrows, DIFF_DV)
    out = pl.pallas_call(
        functools.partial(_decode_kernel, n_new=n_new, ppg=ppg, lam_init=lam_init),
        grid_spec=grid_spec,
        out_shape=jax.ShapeDtypeStruct((nb, 8, DIFF_W), BF16),
        compiler_params=_cparams("parallel", "arbitrary"),
        name="diff_decode",
    )(page_table.reshape(-1), q, qmask, kn, vn.reshape(nb, n_new * DIFF_H, DIFF_DV), expand, hmask, *lams, dn,
      *([ck] * ppg), *([cv] * ppg))
    return out[:, :n_new]


def _head_rms(z, gain, width):
    outs = [_rms(z[:, a:a + width], gain) for a in range(0, z.shape[-1], width)]
    return jnp.concatenate(outs, axis=1)


def _mixout_kernel(x_ref, og_ref, od_ref, wo_ref, nc_ref, wq_ref, qn_ref, x1_ref, qc_ref):
    mix = jnp.concatenate([og_ref[...], od_ref[...]], axis=1)
    x1 = x_ref[...] + _dot(mix, wo_ref[...])
    x1_ref[...] = x1
    hc = _rms(x1, nc_ref[...]).astype(BF16)
    qc = _head_rms(_dot(hc, wq_ref[...]), qn_ref[...], CA_DH) * (CA_DH ** -0.5)
    qc_ref[...] = qc.astype(qc_ref.dtype)


def _mixout(x, og, od, wo, nc, wq, qn, *, tm, q_dtype, layer):
    m, d = x.shape
    row = lambda i: (i, 0)
    return pl.pallas_call(
        _mixout_kernel,
        grid=(m // tm,),
        in_specs=[pl.BlockSpec((tm, d), row), pl.BlockSpec((tm, GDN_W), row), pl.BlockSpec((tm, DIFF_W), row),
                  _weight_spec(wo.shape, layer), _layer_spec(nc, layer), _weight_spec(wq.shape, layer),
                  _layer_spec(qn, layer)],
        out_specs=[pl.BlockSpec((tm, d), row), pl.BlockSpec((tm, CA_W), row)],
        out_shape=[jax.ShapeDtypeStruct((m, d), F32), jax.ShapeDtypeStruct((m, CA_W), q_dtype)],
        compiler_params=_cparams("parallel"),
        name="mixout",
    )(x, og, od, wo, nc, wq, qn)


def _memkv_kernel(m_ref, nm_ref, wk_ref, wv_ref, kn_ref, mk_ref, mv_ref):
    mn = _rms(m_ref[...], nm_ref[...]).astype(BF16)
    mk_ref[...] = _head_rms(_dot(mn, wk_ref[...]), kn_ref[...], CA_DH)
    mv_ref[...] = _dot(mn, wv_ref[...])


def _memkv(mem, nm, wk, wv, kn, *, tm, layer):
    m, d = mem.shape
    row = lambda i: (i, 0)
    return pl.pallas_call(
        _memkv_kernel,
        grid=(m // tm,),
        in_specs=[pl.BlockSpec((tm, d), row), _layer_spec(nm, layer), _weight_spec(wk.shape, layer),
                  _weight_spec(wv.shape, layer), _layer_spec(kn, layer)],
        out_specs=[pl.BlockSpec((tm, CA_W), row)] * 2,
        out_shape=[jax.ShapeDtypeStruct((m, CA_W), F32)] * 2,
        compiler_params=_cparams("parallel"),
        name="memkv",
    )(mem, nm, wk, wv, kn)


def _memory_attend(qs, ks, vs):
    s = [_dot_nt(q, k) for q, k in zip(qs, ks)]
    e = [jnp.exp(si - jnp.max(si, axis=-1, keepdims=True)) for si in s]
    pr = [(ei / jnp.sum(ei, axis=-1, keepdims=True)).astype(BF16) for ei in e]
    return [_dot(p, v) for p, v in zip(pr, vs)]


def _head_cols(h):
    return slice(h * CA_DH, (h + 1) * CA_DH)


def _cross_kernel(q_ref, mk_ref, mv_ref, o_ref, pad_scr, *, n_valid):
    groups = range(q_ref.shape[0])
    pad_scr[...] = jnp.zeros(pad_scr.shape, F32)
    for g in groups:
        pad_scr[g, 0:n_valid, :] = q_ref[g].astype(F32)
    q = [pad_scr[g].astype(BF16) for g in groups]
    units = [(g, h) for g in groups for h in range(CA_H)]
    outs = _memory_attend([q[g][:, _head_cols(h)] for g, h in units],
                          [mk_ref[g, :, h, :].astype(BF16) for g, h in units],
                          [mv_ref[g, :, h, :].astype(BF16) for g, h in units])
    for (g, h), oh in zip(units, outs):
        o_ref[g, :, _head_cols(h)] = oh[0:n_valid].astype(o_ref.dtype)


def _cross(q, mk, mv, *, layer, group):
    nb, seq, _ = q.shape
    mem_spec = pl.BlockSpec((None, group, mk.shape[2], CA_H, CA_DH), lambda b: (layer, b, 0, 0, 0))
    q_spec = pl.BlockSpec((group, seq, CA_W), lambda b: (b, 0, 0))
    return pl.pallas_call(
        functools.partial(_cross_kernel, n_valid=seq),
        grid=(nb // group,),
        in_specs=[q_spec, mem_spec, mem_spec],
        out_specs=q_spec,
        out_shape=jax.ShapeDtypeStruct((nb, seq, CA_W), BF16),
        scratch_shapes=[pltpu.VMEM((group, 8, CA_W), F32)],
        compiler_params=_cparams("parallel"),
        name="cross_attn",
    )(q, mk, mv)


def _ffn_kernel(x_ref, oc_ref, *rest, tm, stride, ncol, attend):
    if attend:
        mk_ref, mv_ref = rest[:2]
        rest = rest[2:]
    wco_ref, nf_ref, wg_ref, wu_ref, cw_ref, wd_ref, hist_ref, y_ref, hout_ref, hbuf, gbuf, acc = rest
    j = pl.program_id(1)
    d_ff = wg_ref.shape[1]
    hrows = (FFN_CONV - 1) * stride
    base = hbuf.shape[0]

    @pl.when(j == 0)
    def _():
        hbuf[base - hrows:base, :] = hist_ref[0]

    if attend:
        q = oc_ref[...]
        heads = range(CA_H)
        oc = jnp.concatenate(_memory_attend([q[:, _head_cols(h)] for h in heads],
                                            [mk_ref[0, :, _head_cols(h)].astype(BF16) for h in heads],
                                            [mv_ref[0, :, _head_cols(h)].astype(BF16) for h in heads]),
                             axis=1).astype(BF16)
    else:
        oc = oc_ref[...]
    x2 = x_ref[...] + _dot(oc, wco_ref[...])
    hf = _rms(x2, nf_ref[...]).astype(BF16)
    acc[...] = x2
    for c0 in range(0, d_ff, ncol):
        cs = slice(c0, c0 + ncol)
        gbuf[base - hrows:base, :] = hbuf[base - hrows:base, cs]
        gbuf[base:base + tm, :] = _dot(hf, wg_ref[:, cs])
        hbuf[base - hrows:base, cs] = gbuf[base + tm - hrows:base + tm, :]
        w = cw_ref[:, cs]
        gt = gbuf[base - hrows:base - hrows + tm, :] * w[0:1]
        for t in range(1, FFN_CONV):
            off = base - hrows + t * stride
            gt = gt + gbuf[off:off + tm, :] * w[t:t + 1]
        act = (_silu(gt) * _dot(hf, wu_ref[:, cs])).astype(BF16)
        acc[...] += _dot(act, wd_ref[cs, :])
    y_ref[...] = acc[...]
    hout_ref[0] = hbuf[base - hrows:base, :]


def _ffn(x, oc, wco, nf, wg, wu, cw, wd, hist, *, ngroups, tm, stride, ncol, layer, mem=None):
    m, d = x.shape
    d_ff = wg.shape[-1]
    nt = m // (ngroups * tm)
    hrows = (FFN_CONV - 1) * stride
    base = -(-hrows // 8) * 8
    row = lambda b, j: (b * nt + j, 0)
    grp = lambda b, j: (b, 0, 0)
    mem = tuple(mem or ())
    return pl.pallas_call(
        functools.partial(_ffn_kernel, tm=tm, stride=stride, ncol=ncol, attend=bool(mem)),
        grid=(ngroups, nt),
        in_specs=[pl.BlockSpec((tm, d), row), pl.BlockSpec((tm, CA_W), row)]
                 + [pl.BlockSpec((1,) + a.shape[1:], grp) for a in mem]
                 + [_weight_spec(wco.shape, layer),
                  _layer_spec(nf, layer), _weight_spec(wg.shape, layer), _weight_spec(wu.shape, layer),
                  _layer_spec(cw, layer), _weight_spec(wd.shape, layer),
                  pl.BlockSpec((1, hrows, d_ff), grp)],
        out_specs=[pl.BlockSpec((tm, d), row), pl.BlockSpec((1, hrows, d_ff), grp)],
        out_shape=[jax.ShapeDtypeStruct((m, d), F32), jax.ShapeDtypeStruct((ngroups, hrows, d_ff), F32)],
        scratch_shapes=[pltpu.VMEM((base, d_ff), F32), pltpu.VMEM((base + tm, ncol), F32),
                        pltpu.VMEM((tm, d), F32)],
        compiler_params=_cparams("parallel", "arbitrary"),
        name="ffn",
    )(x, oc, *mem, wco, nf, wg, wu, cw, wd, hist)


def _rotary_tables(pos):
    half = DIFF_DH // 2
    inv = ROPE_THETA ** (-jnp.arange(half, dtype=F32) / half)
    ang = pos.astype(F32)[:, None] * inv[None, :]
    cos, sin, zero = jnp.cos(ang), jnp.sin(ang), jnp.zeros_like(ang)
    reps = (1, LANE // DIFF_DH)
    return (jnp.tile(jnp.concatenate([cos, cos], axis=1), reps),
            jnp.tile(jnp.concatenate([-sin, zero], axis=1), reps),
            jnp.tile(jnp.concatenate([zero, sin], axis=1), reps))


def kernel(x_prompt, x_sample, mem_prompt, cache_k, cache_v, page_table, state_gdn, state_gdn_conv, cache_mem_k, cache_mem_v, state_ffn_conv, norm_mix, w_in, conv_qkv, a_log, dt_bias, gdn_norm, qnorm_diff, knorm_diff, lam_q1, lam_k1, lam_q2, lam_k2, diff_norm, w_out, norm_cross, norm_mem, w_cq, w_ck, w_cv, qnorm_cross, knorm_cross, w_co, norm_ffn, w_gate, w_up, conv_ffn, w_down):
    depth = w_in.shape[0]
    bp, lp, d = x_prompt.shape
    bs, ls, _ = x_sample.shape
    mlen = mem_prompt.shape[1]
    d_ff = w_gate.shape[2]
    past_len = page_table.shape[1] * PAGE
    mp, ms = bp * lp, bs * ls

    cos_p, sa_p, sb_p = _rotary_tables(jnp.arange(lp, dtype=jnp.int32))
    cos_s, sa_s, sb_s = [jnp.tile(t, (bs, 1)) for t in
                         _rotary_tables(past_len + jnp.arange(ls, dtype=jnp.int32))]
    ii = jnp.arange(DIFF_W)
    gmat = jnp.where(ii[:, None] // DIFF_DH == ii[None, :] // DIFF_DH, 1.0 / DIFF_DH, 0.0).astype(BF16)
    c_qkv, c_gate = 3 * GDN_W, 4 * GDN_W
    c_ba = c_gate + 2 * GDN_H

    xp = x_prompt.reshape(mp, d)
    xs = x_sample.reshape(ms, d)
    memp = mem_prompt.reshape(bp * mlen, d)
    zeros_s0 = jnp.zeros((bp, GDN_H, GDN_D, GDN_D), F32)
    zeros_cb = jnp.zeros((bp, GDN_CONV - 1, 3 * GDN_W), F32)
    zeros_fb = jnp.zeros((bp, FFN_CONV - 1, d_ff), F32)

    wb = {name: w.astype(BF16) for name, w in
          dict(out=w_out, cq=w_cq, ck=w_ck, cv=w_cv, co=w_co, gate=w_gate, up=w_up, down=w_down).items()}

    w_in_t = w_in.transpose(0, 2, 1)
    wb["in"] = jnp.concatenate([w_in_t[:, :c_gate], w_in_t[:, c_ba:], w_in_t[:, c_gate:c_ba],
                                jnp.zeros((depth, LANE - 2 * GDN_H, d), F32)], axis=1).astype(BF16)
    rows = lambda a: a.astype(F32)[:, None, :]
    lane_pad = lambda a: jnp.pad(rows(a), ((0, 0), (0, 0), (GDN_H, LANE - 2 * GDN_H)))
    head_tile = lambda a: jnp.tile(rows(a), (1, 1, 2 * DIFF_H))
    lams = (rows(lam_q1), rows(lam_k1), rows(lam_q2), rows(lam_k2))
    dn = rows(diff_norm)

    p_out, s_out, kv_stack = [], [], ()
    for l in range(depth):
        lam_init = 0.8 - 0.6 * math.exp(-0.3 * l)
        inproj = functools.partial(_inproj, gain=rows(norm_mix), w=wb["in"], qg=head_tile(qnorm_diff),
                                   kg=head_tile(knorm_diff), gmat=gmat, layer=l)
        gdn = functools.partial(_gdn, cw=conv_qkv, alog=lane_pad(a_log), dtb=lane_pad(dt_bias),
                                gn=rows(gdn_norm), layer=l)
        mixout = functools.partial(_mixout, wo=wb["out"], nc=rows(norm_cross), wq=wb["cq"],
                                   qn=rows(qnorm_cross), layer=l)
        ffn = functools.partial(_ffn, wco=wb["co"], nf=rows(norm_ffn), wg=wb["gate"], wu=wb["up"],
                                cw=conv_ffn, wd=wb["down"], layer=l)

        mk_p, mv_p = _memkv(memp, rows(norm_mem), wb["ck"], wb["cv"], rows(knorm_cross), tm=512, layer=l)
        qkv, gate, ba, q, kt, k16, v2, vt16 = inproj(xp, cos=cos_p, sa=sa_p, sb=sb_p, q_dtype=BF16, seq=lp,
                                                     tm=512 if kv_stack else 1024, prev=kv_stack)
        kv_stack = (kt, v2)
        og, sp = gdn(qkv, gate, ba, s0=zeros_s0, cb=zeros_cb, nb=bp, seq=lp, lt=512)
        od = _flash(q, k16, vt16, lams, dn, nb=bp, seq=lp, tq=512, lam_init=lam_init, layer=l)
        x1, qc = mixout(xp, og, od, tm=1024, q_dtype=BF16)
        xp, fp = ffn(x1, qc, hist=zeros_fb, ngroups=bp, tm=512, stride=1, ncol=1408,
                     mem=(mk_p.reshape(bp, mlen, CA_W), mv_p.reshape(bp, mlen, CA_W)))
        p_out.append((sp, qkv.reshape(bp, lp, c_qkv)[:, lp - (GDN_CONV - 1):],
                      mk_p.reshape(bp, mlen, CA_H, CA_DH), mv_p.reshape(bp, mlen, CA_H, CA_DH), fp))

        qkv, gate, ba, q, k, v = inproj(xs, cos=cos_s, sa=sa_s, sb=sb_s, tm=ms, q_dtype=F32)
        r3 = lambda a: a.reshape(bs, ls, a.shape[-1])
        og, ss = gdn(r3(qkv), r3(gate), r3(ba), s0=state_gdn[l], cb=state_gdn_conv[l], nb=bs, seq=ls,
                     lt=GDN_CHUNK)
        od = _decode(page_table, r3(q), r3(k), r3(v), cache_k, cache_v, lams, dn,
                     layer=l, ppg=math.gcd(page_table.shape[1], 32), lam_init=lam_init)
        x1, qc = mixout(xs, og.reshape(ms, GDN_W), od.reshape(ms, DIFF_W), tm=ms, q_dtype=F32)
        oc = _cross(r3(qc), cache_mem_k, cache_mem_v, layer=l, group=math.gcd(bs, 4))
        tmaj = lambda a: a.reshape(bs, ls, -1).transpose(1, 0, 2).reshape(ms, -1)
        hist = state_ffn_conv[l].transpose(1, 0, 2).reshape(1, (FFN_CONV - 1) * bs, d_ff)
        y_t, fs_t = ffn(tmaj(x1), tmaj(oc), hist=hist, ngroups=1, tm=ms, stride=bs, ncol=1408)
        xs = y_t.reshape(ls, bs, d).transpose(1, 0, 2).reshape(ms, d)
        fs = fs_t.reshape(FFN_CONV - 1, bs, d_ff).transpose(1, 0, 2)
        conv_s = jnp.concatenate([state_gdn_conv[l], r3(qkv)], axis=1)[:, ls:]
        s_out.append((k.reshape(bs, ls, 2 * DIFF_H, DIFF_DH), v.reshape(bs, ls, DIFF_H, DIFF_DV), ss, conv_s, fs))

    pg, pc, pmk, pmv, pf = [jnp.stack(t, axis=0) for t in zip(*p_out)]
    pk = kv_stack[0].reshape(depth, bp, 2 * DIFF_H, DIFF_DH, lp).transpose(0, 1, 4, 2, 3)
    pv = kv_stack[1].reshape(depth, bp, lp, DIFF_H, DIFF_DV)
    sk, sv, sg, sc, sf = [jnp.stack(t, axis=0) for t in zip(*s_out)]
    return (xp.reshape(bp, lp, d), xs.reshape(bs, ls, d), pk, pv, pg, pc, pmk, pmv, pf, sk, sv, sg, sc, sf)
```

```python
import functools
import math

import jax
import jax.numpy as jnp
from jax import lax
from jax.experimental import pallas as pl
from jax.experimental.pallas import tpu as pltpu

F32 = jnp.float32
BF16 = jnp.bfloat16

EPS = 1e-6
ROPE_THETA = 10000.0
GDN_H = 4
GDN_D = 128
GDN_W = GDN_H * GDN_D
GDN_CONV = 4
GDN_CHUNK = 64
DIFF_H = 4
DIFF_DH = 64
DIFF_DV = 128
DIFF_W = DIFF_H * DIFF_DV
CA_H = 4
CA_DH = 128
CA_W = CA_H * CA_DH
FFN_CONV = 3
PAGE = 128
LANE = 128
VMEM_LIMIT = 56 * 1024 * 1024
NOT_SAME_CHUNK = 99.0
LOG2E = math.log2(math.e)

ROWS_PROJ = 1024
ROWS_PROJ_STACKING = 512
ROWS_SUB = 256
ROWS_MLP = 512
ROWS_MEM = 512
ROWS_GDN = 512
ROWS_FLASH = 512
PAGES_PER_STEP = 32
SEQS_PER_STEP = 4


def _mlp_cols(d_ff):
    return max(c for c in range(LANE, d_ff // 2 + 1, LANE) if d_ff % c == 0)


def _cparams(*sem):
    return pltpu.CompilerParams(dimension_semantics=sem, vmem_limit_bytes=VMEM_LIMIT)


def _dot(a, b):
    return jnp.dot(a, b, preferred_element_type=F32)


def _dot_nt(a, b):
    return lax.dot_general(a, b, (((1,), (1,)), ((), ())), preferred_element_type=F32)


def _dot_tn(a, b):
    return lax.dot_general(a, b, (((0,), (0,)), ((), ())), preferred_element_type=F32)


def _sigmoid(x):
    return 1.0 / (1.0 + jnp.exp(-x))


def _silu(x):
    return x * _sigmoid(x)


def _softplus(x):
    return jnp.maximum(x, 0.0) + jnp.log1p(jnp.exp(-jnp.abs(x)))


def _rms(x, gain):
    return x * lax.rsqrt(jnp.mean(x * x, axis=-1, keepdims=True) + EPS) * gain


def _const_spec(shape):
    nd = len(shape)
    return pl.BlockSpec(shape, lambda *_: (0,) * nd)


def _layer_spec(a, layer):
    return pl.BlockSpec((None,) + tuple(a.shape[1:]), lambda *_: (layer, 0, 0))


def _weight_spec(shape, layer=None):
    if layer is None:
        nd = len(shape)
        return pl.BlockSpec(shape, lambda *_: (0,) * nd, pipeline_mode=pl.Buffered(1))
    nd = len(shape) - 1
    return pl.BlockSpec((None,) + tuple(shape[1:]), lambda *_: (layer,) + (0,) * nd, pipeline_mode=pl.Buffered(1))


def _inproj_kernel(x_ref, gain_ref, w_ref, qg_ref, kg_ref, cos_ref, sa_ref, sb_ref, gmat_ref, *refs, n_prev):
    prev_refs, (qkv_ref, gate_ref, ba_ref, q_ref, *kv_refs) = refs[:2 * (n_prev > 0)], refs[2 * (n_prev > 0):]
    c_qkv, c_gate = 3 * GDN_W, 4 * GDN_W
    c_q, c_k, c_v = c_gate + DIFF_W, c_gate + 2 * DIFF_W, c_gate + 3 * DIFF_W
    half = DIFF_DH // 2
    widen = lambda t: jnp.concatenate([t] * (DIFF_W // LANE), axis=1)
    tm = x_ref.shape[0]
    sub = math.gcd(tm, ROWS_SUB)
    parts = [slice(r, r + sub) for r in range(0, tm, sub)]
    hs = [_rms(x_ref[p, :], gain_ref[...]).astype(BF16) for p in parts]

    def proj(a, b):
        return [_dot_nt(h, w_ref[a:b, :]) for h in hs]

    def norm_rot(zs, g):
        ms = [_dot((z * z).astype(BF16), gmat_ref[...]) for z in zs]
        out = []
        for p, z, m in zip(parts, zs, ms):
            cos, sa, sb = widen(cos_ref[p, :]), widen(sa_ref[p, :]), widen(sb_ref[p, :])
            y = z * lax.rsqrt(m + EPS) * g
            out.append(y * cos + pltpu.roll(y, DIFF_W - half, 1) * sa + pltpu.roll(y, half, 1) * sb)
        return out

    for p, z in zip(parts, proj(0, c_qkv)):
        qkv_ref[p, :] = z
    for p, z in zip(parts, proj(c_qkv, c_gate)):
        gate_ref[p, :] = z.astype(gate_ref.dtype)
    vs = proj(c_k, c_v)
    for p, z in zip(parts, proj(c_v, c_v + LANE)):
        ba_ref[p, :] = z
    for p, z in zip(parts, norm_rot(proj(c_gate, c_q), qg_ref[...])):
        q_ref[p, :] = (z * (LOG2E * DIFF_DH ** -0.5)).astype(q_ref.dtype)
    ks = norm_rot(proj(c_q, c_k), kg_ref[...])
    if len(kv_refs) == 2:
        for p, k, v in zip(parts, ks, vs):
            kv_refs[0][p, :] = k
            kv_refs[1][p, :] = v
    else:
        kt_ref, k16_ref, v2_ref, vt16_ref = kv_refs
        if n_prev:
            kt_ref[0:n_prev] = prev_refs[0][...]
            v2_ref[0:n_prev] = prev_refs[1][...]
        for p, k, v in zip(parts, ks, vs):
            kt_ref[n_prev, :, p] = k.T
            k16_ref[p, :] = k.astype(BF16)
            vt16_ref[:, p] = v.T.astype(BF16)
            for hd in range(DIFF_H):
                v2_ref[n_prev, pl.ds(DIFF_H * p.start + hd, sub, stride=DIFF_H), :] = v[:, hd * DIFF_DV:(hd + 1) * DIFF_DV]


def _inproj(x, gain, w, qg, kg, cos, sa, sb, gmat, *, tm, q_dtype, layer, seq=None, prev=()):
    m, d = x.shape
    npos = cos.shape[0] // tm
    row = lambda i: (i, 0)
    pos = lambda i: (i % npos, 0)
    n_prev = prev[0].shape[0] if prev else 0
    if seq is None:
        kv_specs = [pl.BlockSpec((tm, DIFF_W), row)] * 2
        kv_shapes = [jax.ShapeDtypeStruct((m, DIFF_W), F32)] * 2
        prev_specs = []
    else:
        nt, nb = seq // tm, m // seq
        kcols = lambda i: (0, i // nt, 0, i % nt)
        vrows = lambda i: (0, i // nt, i % nt, 0)
        kt_spec = lambda n: pl.BlockSpec((n, None, DIFF_W, tm), kcols)
        v2_spec = lambda n: pl.BlockSpec((n, None, tm * DIFF_H, DIFF_DV), vrows)
        kv_specs = [kt_spec(n_prev + 1), pl.BlockSpec((tm, DIFF_W), row), v2_spec(n_prev + 1),
                    pl.BlockSpec((None, DIFF_W, tm), lambda i: (i // nt, 0, i % nt))]
        kv_shapes = [jax.ShapeDtypeStruct((n_prev + 1, nb, DIFF_W, seq), F32),
                     jax.ShapeDtypeStruct((m, DIFF_W), BF16),
                     jax.ShapeDtypeStruct((n_prev + 1, nb, seq * DIFF_H, DIFF_DV), F32),
                     jax.ShapeDtypeStruct((nb, DIFF_W, seq), BF16)]
        prev_specs = [kt_spec(n_prev), v2_spec(n_prev)] if prev else []
    return pl.pallas_call(
        functools.partial(_inproj_kernel, n_prev=n_prev),
        grid=(m // tm,),
        in_specs=[pl.BlockSpec((tm, d), row), _layer_spec(gain, layer), _weight_spec(w.shape, layer),
                  _layer_spec(qg, layer), _layer_spec(kg, layer),
                  pl.BlockSpec((tm, LANE), pos), pl.BlockSpec((tm, LANE), pos),
                  pl.BlockSpec((tm, LANE), pos), _const_spec((DIFF_W, DIFF_W))] + prev_specs,
        out_specs=[pl.BlockSpec((tm, 3 * GDN_W), row), pl.BlockSpec((tm, GDN_W), row),
                   pl.BlockSpec((tm, LANE), row), pl.BlockSpec((tm, DIFF_W), row)] + kv_specs,
        out_shape=[jax.ShapeDtypeStruct((m, 3 * GDN_W), F32), jax.ShapeDtypeStruct((m, GDN_W), BF16),
                   jax.ShapeDtypeStruct((m, LANE), F32), jax.ShapeDtypeStruct((m, DIFF_W), q_dtype)] + kv_shapes,
        compiler_params=_cparams("parallel"),
        name="inproj",
    )(x, gain, w, qg, kg, cos, sa, sb, gmat, *prev)


def _gdn_kernel(x_ref, gate_ref, ba_ref, cw_ref, alog_ref, dtb_ref, gn_ref, s0_ref, cb_ref, tri_ref, lv_ref,
                lvl_ref, o_ref, sout_ref, xbuf, s_scr, *pad_scr, lt, n_valid, chunk):
    j = pl.program_id(1)
    padded = n_valid < lt
    tail = GDN_CONV - 1

    @pl.when(j == 0)
    def _():
        for h in range(GDN_H):
            s_scr[h] = s0_ref[0, h].T
        xbuf[0:8, :] = jnp.zeros((8, 3 * GDN_W), F32)
        xbuf[8 - tail:8, :] = cb_ref[0]

    @pl.when(j > 0)
    def _():
        xbuf[0:8, :] = xbuf[lt:lt + 8, :]

    if padded:
        gate_scr, ba_scr = pad_scr
        xbuf[8:8 + lt, :] = jnp.zeros((lt, 3 * GDN_W), F32)
        xbuf[8:8 + n_valid, :] = x_ref[0]
        gate_scr[...] = jnp.zeros(gate_scr.shape, F32)
        gate_scr[0:n_valid, :] = gate_ref[0].astype(F32)
        ba_scr[...] = jnp.zeros(ba_scr.shape, F32)
        ba_scr[0:n_valid, :] = ba_ref[0]
        gate_src, ba = gate_scr, ba_scr[...]
    else:
        xbuf[8:8 + lt, :] = x_ref[...]
        gate_src, ba = gate_ref, ba_ref[...]

    def conv_act(c0):
        w = cw_ref[:, c0:c0 + GDN_D]
        y = xbuf[8 - tail:8 - tail + lt, c0:c0 + GDN_D] * w[0:1]
        for t in range(1, GDN_CONV):
            y = y + xbuf[8 - tail + t:8 - tail + t + lt, c0:c0 + GDN_D] * w[t:t + 1]
        return _silu(y)

    def l2n(z):
        return z * lax.rsqrt(jnp.sum(z * z, axis=-1, keepdims=True) + EPS)

    beta_all = _sigmoid(ba)
    g_all = -jnp.exp(alog_ref[...]) * _softplus(ba + dtb_ref[...])
    if padded:
        valid = lax.broadcasted_iota(jnp.int32, (lt, 1), 0) < n_valid
        beta_all = jnp.where(valid, beta_all, 0.0)
        g_all = jnp.where(valid, g_all, 0.0)
    g_hi = g_all.astype(BF16)
    g_lo = (g_all - g_hi.astype(F32)).astype(BF16)
    gc_all = _dot(tri_ref[...], g_hi) + _dot(tri_ref[...], g_lo)
    gc_rows = gc_all.T

    lv = lv_ref[...]
    bl = lv.shape[0]
    incl = lv < NOT_SAME_CHUNK
    strict = jnp.logical_and(incl, lv >= 0.0)
    eye = jnp.where(lv == -1.0, 1.0, 0.0)
    lev0 = lv == 0.0
    n_levels = int(math.log2(chunk))
    run_levels = min(n_levels, max(1, math.ceil(math.log2(n_valid)))) if padded else n_levels
    heads = range(GDN_H)
    units = [(h, r) for r in range(0, lt, bl) for h in heads]

    gcol = [gc_all[:, GDN_H + h:GDN_H + h + 1] for h in heads]
    bcol = [beta_all[:, h:h + 1] for h in heads]
    eg = [jnp.exp(gcol[h]) for h in heads]
    q = [l2n(conv_act(h * GDN_D)) * (GDN_D ** -0.5) for h in heads]
    k = [l2n(conv_act(GDN_W + h * GDN_D)) for h in heads]
    rhs = [jnp.concatenate([conv_act(2 * GDN_W + h * GDN_D) * bcol[h], k[h] * (bcol[h] * eg[h])],
                           axis=1).astype(BF16) for h in heads]
    qb = [q[h].astype(BF16) for h in heads]
    kb = [k[h].astype(BF16) for h in heads]
    qg = [q[h] * eg[h] for h in heads]
    mm, aqk = [], []
    for h, r in units:
        decay = jnp.where(incl, jnp.exp(gcol[h][r:r + bl] - gc_rows[GDN_H + h:GDN_H + h + 1, r:r + bl]), 0.0)
        mm.append(jnp.where(strict, bcol[h][r:r + bl] * _dot_nt(kb[h][r:r + bl], kb[h][r:r + bl]) * decay, 0.0))
        aqk.append((_dot_nt(qb[h][r:r + bl], kb[h][r:r + bl]) * decay).astype(BF16))
    x = [eye - jnp.where(lev0, m, 0.0) for m in mm]
    mmb = [m.astype(BF16) for m in mm]
    for lev in range(1, run_levels):
        xb = [xi.astype(BF16) for xi in x]
        y = [_dot(mmb[i] * lvl_ref[lev], xb[i]).astype(BF16) for i in range(len(units))]
        x = [x[i] - _dot(xb[i], y[i]) for i in range(len(units))]
    uwb_u = [_dot(x[i].astype(BF16), rhs[h][r:r + bl]).astype(BF16) for i, (h, r) in enumerate(units)]
    ou_u = [_dot(aqk[i], uwb_u[i]) for i in range(len(units))]
    per_head = lambda vals, h: jnp.concatenate([vals[i] for i, (hh, _) in enumerate(units) if hh == h], axis=0)
    uwb = [per_head(uwb_u, h) for h in heads]
    ou = [per_head(ou_u, h) for h in heads]
    qp = [(qg[h] - ou[h][:, GDN_D:]).astype(BF16) for h in heads]
    for c in range(lt // chunk):
        r0, r1 = c * chunk, (c + 1) * chunk
        for h in heads:
            glast = gcol[h][r1 - 1:r1, :]
            kg = (k[h][r0:r1] * jnp.exp(glast - gcol[h][r0:r1])).astype(BF16)
            ab = _dot_tn(uwb[h][r0:r1, :], kg)
            st = s_scr[h]
            stb = st.astype(BF16)
            o = _dot_nt(qp[h][r0:r1], stb) + ou[h][r0:r1, :GDN_D]
            s_scr[h] = st * jnp.exp(glast) + ab[:GDN_D] - _dot(stb, ab[GDN_D:].astype(BF16))
            gate = gate_src[r0:r1, h * GDN_D:(h + 1) * GDN_D].astype(F32)
            on = _rms(o, gn_ref[...]) * _silu(gate)
            if padded:
                if r0 < n_valid:
                    o_ref[0, :, h * GDN_D:(h + 1) * GDN_D] = on[0:n_valid].astype(o_ref.dtype)
            else:
                o_ref[r0:r1, h * GDN_D:(h + 1) * GDN_D] = on.astype(o_ref.dtype)

    @pl.when(j == pl.num_programs(1) - 1)
    def _():
        for h in range(GDN_H):
            sout_ref[0, h] = s_scr[h].T


def _chunk_level_codes(lt, chunk):
    i = jnp.arange(lt)[:, None]
    j = jnp.arange(lt)[None, :]
    code = jnp.full((lt, lt), NOT_SAME_CHUNK, F32)
    for lev in reversed(range(int(math.log2(chunk)))):
        code = jnp.where((i // (2 << lev)) == (j // (2 << lev)), float(lev), code)
    code = jnp.where(i > j, code, NOT_SAME_CHUNK)
    return jnp.where(i == j, -1.0, code).astype(F32)


def _gdn(qkv, gate, ba, cw, alog, dtb, gn, s0, cb, *, nb, seq, lt, layer):
    padded = seq < lt
    n_valid = seq if padded else lt
    nt = 1 if padded else seq // lt
    chunk = GDN_CHUNK
    i = jnp.arange(lt)
    tri = ((i[:, None] // chunk == i[None, :] // chunk) & (i[:, None] >= i[None, :])).astype(BF16)
    bl = min(lt, 2 * chunk)
    lv = _chunk_level_codes(bl, chunk)
    n_levels = int(math.log2(chunk))
    lvl = jnp.stack([(lv == float(lev)).astype(BF16) for lev in range(n_levels)], axis=0)
    if padded:
        tile = lambda c: pl.BlockSpec((1, seq, c), lambda b, j: (b, 0, 0))
        out_o = jax.ShapeDtypeStruct((nb, seq, GDN_W), BF16)
        scratch = [pltpu.VMEM((lt, GDN_W), F32), pltpu.VMEM((lt, LANE), F32)]
    else:
        tile = lambda c: pl.BlockSpec((lt, c), lambda b, j: (b * nt + j, 0))
        out_o = jax.ShapeDtypeStruct((nb * seq, GDN_W), BF16)
        scratch = []
    return pl.pallas_call(
        functools.partial(_gdn_kernel, lt=lt, n_valid=n_valid, chunk=chunk),
        grid=(nb, nt),
        in_specs=[tile(3 * GDN_W), tile(GDN_W), tile(LANE),
                  _layer_spec(cw, layer), _layer_spec(alog, layer), _layer_spec(dtb, layer),
                  _layer_spec(gn, layer),
                  pl.BlockSpec((1, GDN_H, GDN_D, GDN_D), lambda b, j: (b, 0, 0, 0)),
                  pl.BlockSpec((1, GDN_CONV - 1, 3 * GDN_W), lambda b, j: (b, 0, 0)),
                  _const_spec((lt, lt)), _const_spec((bl, bl)), _const_spec((n_levels, bl, bl))],
        out_specs=[tile(GDN_W), pl.BlockSpec((1, GDN_H, GDN_D, GDN_D), lambda b, j: (b, 0, 0, 0))],
        out_shape=[out_o, jax.ShapeDtypeStruct((nb, GDN_H, GDN_D, GDN_D), F32)],
        scratch_shapes=[pltpu.VMEM((lt + 8, 3 * GDN_W), F32), pltpu.VMEM((GDN_H, GDN_D, GDN_D), F32)] + scratch,
        compiler_params=_cparams("parallel", "arbitrary"),
        name="gdn",
    )(qkv, gate, ba, cw, alog, dtb, gn, s0, cb, tri, lv, lvl)


def _lambda(l1q, l1k, l2q, l2k, lam_init):
    return (jnp.exp(jnp.sum(l1q[...] * l1k[...], axis=-1, keepdims=True))
            - jnp.exp(jnp.sum(l2q[...] * l2k[...], axis=-1, keepdims=True)) + lam_init)


def _flash_kernel(qi_tab, ki_tab, q_ref, k_ref, vt_ref, l1q, l1k, l2q, l2k, dn_ref, o_ref,
                  m_scr, acc_scr, *, tq, tk, lam_init):
    p = pl.program_id(1)
    qi, ki = qi_tab[p], ki_tab[p]

    @pl.when(ki == 0)
    def _():
        m_scr[...] = jnp.full(m_scr.shape, -jnp.inf, F32)
        acc_scr[...] = jnp.zeros(acc_scr.shape, F32)

    lane = lax.broadcasted_iota(jnp.int32, (1, DIFF_DV), 1)
    pairs = range(2 * DIFF_H)
    ones = jnp.ones((acc_scr.shape[1] - DIFF_DV, tk), BF16)

    def step(masked):
        kc, vt = [], []
        for h in range(DIFF_H):
            hs = slice(h * DIFF_DV, (h + 1) * DIFF_DV)
            kh = k_ref[:, hs]
            vt.append(jnp.concatenate([vt_ref[hs, :], ones], axis=0))
            kc += [jnp.where((lane // DIFF_DH) == c, kh, jnp.zeros_like(kh)) for c in range(2)]
        st = [_dot_nt(kc[i], q_ref[:, (i // 2) * DIFF_DV:(i // 2 + 1) * DIFF_DV]) for i in pairs]
        if masked:
            keys = ki * tk + lax.broadcasted_iota(jnp.int32, (tk, tq), 0)
            queries = qi * tq + lax.broadcasted_iota(jnp.int32, (tk, tq), 1)
            visible = keys <= queries
            st = [jnp.where(visible, st[i], -jnp.inf) for i in pairs]
        m_old = [m_scr[i:i + 1, :] for i in pairs]
        m_new = [jnp.maximum(m_old[i], jnp.max(st[i], axis=0, keepdims=True)) for i in pairs]
        pt = [jnp.exp2(st[i] - m_new[i]).astype(BF16) for i in pairs]
        alpha = [jnp.exp2(m_old[i] - m_new[i]) for i in pairs]
        for i in pairs:
            acc_scr[i] = alpha[i] * acc_scr[i] + _dot(vt[i // 2], pt[i])
            m_scr[i:i + 1, :] = m_new[i]

    @pl.when(ki < qi)
    def _():
        step(False)

    @pl.when(ki == qi)
    def _():
        step(True)
        lam = _lambda(l1q, l1k, l2q, l2k, lam_init)
        for h in range(DIFF_H):
            a0, a1 = acc_scr[2 * h], acc_scr[2 * h + 1]
            ot = (a0[:DIFF_DV] / a0[DIFF_DV:DIFF_DV + 1]
                  - lam * (a1[:DIFF_DV] / a1[DIFF_DV:DIFF_DV + 1]))
            o_ref[:, h * DIFF_DV:(h + 1) * DIFF_DV] = (_rms(ot.T, dn_ref[...]) * (1.0 - lam_init)).astype(o_ref.dtype)


def _flash(q, k, v, lams, dn, *, nb, seq, tq, lam_init, layer):
    nq = seq // tq
    pairs = [(a, b) for a in range(nq) for b in range(a + 1)]
    qi_tab = jnp.array([a for a, _ in pairs], jnp.int32)
    ki_tab = jnp.array([b for _, b in pairs], jnp.int32)
    qmap = lambda b, p, qt, kt: (b * nq + qt[p], 0)
    kmap = lambda b, p, qt, kt: (b * nq + kt[p], 0)
    cmap = lambda b, p, qt, kt: (0, 0)
    grid_spec = pltpu.PrefetchScalarGridSpec(
        num_scalar_prefetch=2,
        grid=(nb, len(pairs)),
        in_specs=[pl.BlockSpec((tq, DIFF_W), qmap), pl.BlockSpec((tq, DIFF_W), kmap),
                  pl.BlockSpec((None, DIFF_W, tq), lambda b, p, qt, kt: (b, 0, kt[p]))]
                 + [_layer_spec(a, layer) for a in (*lams, dn)],
        out_specs=pl.BlockSpec((tq, DIFF_W), qmap),
        scratch_shapes=[pltpu.VMEM((2 * DIFF_H, tq), F32),
                        pltpu.VMEM((2 * DIFF_H, DIFF_DV + 16, tq), F32)],
    )
    return pl.pallas_call(
        functools.partial(_flash_kernel, tq=tq, tk=tq, lam_init=lam_init),
        grid_spec=grid_spec,
        out_shape=jax.ShapeDtypeStruct((nb * seq, DIFF_W), BF16),
        compiler_params=_cparams("parallel", "arbitrary"),
        name="diff_flash",
    )(qi_tab, ki_tab, q, k, v, *lams, dn)


def _decode_kernel(pt_ref, q_ref, qmask_ref, kn_ref, vn_ref, expand_ref, hmask_ref, l1q, l1k, l2q, l2k, dn_ref,
                   *rest, n_new, ppg, lam_init):
    kpages, vpages = rest[:ppg], rest[ppg:2 * ppg]
    o_ref = rest[2 * ppg]
    qb_scr, kpad_scr, vpad_scr, m_scr, l_scr, acc_scr = rest[2 * ppg + 1:]
    g = pl.program_id(1)
    nrow = 2 * DIFF_H * 8

    @pl.when(g == 0)
    def _():
        kpad_scr[...] = jnp.zeros(kpad_scr.shape, F32)
        kpad_scr[0:n_new, :] = q_ref[0].astype(F32)
        q8 = kpad_scr[0:8, :]
        qb_scr[...] = (jnp.concatenate([q8] * (2 * DIFF_H), axis=0) * qmask_ref[...]).astype(BF16)
        m_scr[...] = jnp.full(m_scr.shape, -jnp.inf, F32)
        l_scr[...] = jnp.zeros(l_scr.shape, F32)
        acc_scr[...] = jnp.zeros(acc_scr.shape, F32)

    def update(s, vals):
        m_old = m_scr[...]
        m_new = jnp.maximum(m_old, jnp.max(s, axis=-1, keepdims=True))
        alpha = jnp.exp2(m_old - m_new)
        pr = jnp.exp2(s - m_new)
        l_scr[...] = alpha * l_scr[...] + jnp.sum(pr, axis=-1, keepdims=True)
        prb = pr.astype(BF16)
        n = len(vals)
        stacked = prb if n == 1 else jnp.concatenate([prb[:, jj * PAGE:(jj + 1) * PAGE] for jj in range(n)], axis=0)
        spread = _dot(stacked, expand_ref[...])
        pes = [(spread[jj * nrow:(jj + 1) * nrow] * hmask_ref[...]).astype(BF16) for jj in range(n)]
        terms = [_dot(pes[jj], vals[jj]) for jj in range(n)]
        while len(terms) > 1:
            terms = [terms[i] + terms[i + 1] for i in range(0, len(terms) - 1, 2)] + terms[len(terms) & ~1:]
        acc_scr[...] = alpha * acc_scr[...] + terms[0]
        m_scr[...] = m_new

    qb = qb_scr[...]
    s = jnp.concatenate([_dot(qb, kp[...].astype(BF16)) for kp in kpages], axis=1)
    update(s, [vp[...].astype(BF16) for vp in vpages])

    @pl.when(g == pl.num_programs(1) - 1)
    def _():
        kpad_scr[...] = jnp.zeros(kpad_scr.shape, F32)
        kpad_scr[0:n_new, :] = kn_ref[0]
        vpad_scr[...] = jnp.zeros(vpad_scr.shape, F32)
        vpad_scr[0:n_new * DIFF_H, :] = vn_ref[0]
        tok = lax.broadcasted_iota(jnp.int32, (nrow, PAGE), 0) % 8
        key = lax.broadcasted_iota(jnp.int32, (nrow, PAGE), 1)
        ok = jnp.logical_and(key < n_new, key <= tok)
        update(jnp.where(ok, _dot_nt(qb, kpad_scr[...].astype(BF16)), -jnp.inf), [vpad_scr[...].astype(BF16)])
        lam = _lambda(l1q, l1k, l2q, l2k, lam_init)
        on = acc_scr[...] / l_scr[...]
        half = nrow // 2
        for h in range(DIFF_H):
            o = on[8 * h:8 * h + 8, :] - lam * on[half + 8 * h:half + 8 * h + 8, :]
            o_ref[0, :, h * DIFF_DV:(h + 1) * DIFF_DV] = (_rms(o, dn_ref[...]) * (1.0 - lam_init)).astype(o_ref.dtype)


def _decode(page_table, q, kn, vn, cache_k, cache_v, lams, dn, *, layer, ppg, lam_init):
    nb, n_new, _ = q.shape
    n_pages = page_table.shape[1]
    nrow = 2 * DIFF_H * 8
    r = jnp.arange(nrow)
    hc = 2 * ((r // 8) % DIFF_H) + r // (nrow // 2)
    qmask = (jnp.arange(DIFF_W)[None, :] // DIFF_DH == hc[:, None]).astype(F32)
    vrows = PAGE * DIFF_H
    jj = jnp.arange(vrows)
    expand = (jj[None, :] // DIFF_H == jnp.arange(PAGE)[:, None]).astype(BF16)
    hmask = (jj[None, :] % DIFF_H == ((r // 8) % DIFF_H)[:, None]).astype(F32)
    per_b = lambda n, c: pl.BlockSpec((1, n, c), lambda b, g, pt: (b, 0, 0))
    cmap = lambda b, g, pt: (0, 0)

    def page_spec(j, rows, cols):
        return pl.BlockSpec((None, None, rows, cols),
                            lambda b, g, pt: (layer, pt[b * n_pages + g * ppg + j], 0, 0))

    grid_spec = pltpu.PrefetchScalarGridSpec(
        num_scalar_prefetch=1,
        grid=(nb, n_pages // ppg),
        in_specs=[per_b(n_new, DIFF_W), pl.BlockSpec((nrow, DIFF_W), cmap), per_b(n_new, DIFF_W),
                  per_b(n_new * DIFF_H, DIFF_DV), pl.BlockSpec((PAGE, vrows), cmap),
                  pl.BlockSpec((nrow, vrows), cmap)]
                 + [_layer_spec(a, layer) for a in (*lams, dn)]
                 + [page_spec(j, DIFF_W, PAGE) for j in range(ppg)]
                 + [page_spec(j, vrows, DIFF_DV) for j in range(ppg)],
        out_specs=pl.BlockSpec((1, 8, DIFF_W), lambda b, g, pt: (b, 0, 0)),
        scratch_shapes=[pltpu.VMEM((nrow, DIFF_W), BF16), pltpu.VMEM((PAGE, DIFF_W), F32),
                        pltpu.VMEM((vrows, DIFF_DV), F32),
                        pltpu.VMEM((nrow, 1), F32), pltpu.VMEM((nrow, 1), F32),
                        pltpu.VMEM((nrow, DIFF_DV), F32)],
    )
    ck = cache_k.transpose(0, 1, 3, 4, 2).reshape(cache_k.shape[0], cache_k.shape[1], DIFF_W, PAGE)
    cv = cache_v.reshape(cache_v.shape[0], cache_v.shape[1], vrows, DIFF_DV)
    out = pl.pallas_call(
        functools.partial(_decode_kernel, n_new=n_new, ppg=ppg, lam_init=lam_init),
        grid_spec=grid_spec,
        out_shape=jax.ShapeDtypeStruct((nb, 8, DIFF_W), BF16),
        compiler_params=_cparams("parallel", "arbitrary"),
        name="diff_decode",
    )(page_table.reshape(-1), q, qmask, kn, vn.reshape(nb, n_new * DIFF_H, DIFF_DV), expand, hmask, *lams, dn,
      *([ck] * ppg), *([cv] * ppg))
    return out[:, :n_new]


def _head_rms(z, gain, width):
    outs = [_rms(z[:, a:a + width], gain) for a in range(0, z.shape[-1], width)]
    return jnp.concatenate(outs, axis=1)


def _mixout_kernel(x_ref, og_ref, od_ref, wo_ref, nc_ref, wq_ref, qn_ref, x1_ref, qc_ref):
    tm = x_ref.shape[0]
    sub = math.gcd(tm, ROWS_SUB)
    parts = [slice(r, r + sub) for r in range(0, tm, sub)]
    mix = [jnp.concatenate([og_ref[p, :], od_ref[p, :]], axis=1) for p in parts]
    x1 = [x_ref[p, :] + _dot(m, wo_ref[...]) for p, m in zip(parts, mix)]
    for p, v in zip(parts, x1):
        x1_ref[p, :] = v
    hc = [_rms(v, nc_ref[...]).astype(BF16) for v in x1]
    qc = [_head_rms(_dot(h, wq_ref[...]), qn_ref[...], CA_DH) * (CA_DH ** -0.5) for h in hc]
    for p, v in zip(parts, qc):
        qc_ref[p, :] = v.astype(qc_ref.dtype)


def _mixout(x, og, od, wo, nc, wq, qn, *, tm, q_dtype, layer):
    m, d = x.shape
    row = lambda i: (i, 0)
    return pl.pallas_call(
        _mixout_kernel,
        grid=(m // tm,),
        in_specs=[pl.BlockSpec((tm, d), row), pl.BlockSpec((tm, GDN_W), row), pl.BlockSpec((tm, DIFF_W), row),
                  _weight_spec(wo.shape, layer), _layer_spec(nc, layer), _weight_spec(wq.shape, layer),
                  _layer_spec(qn, layer)],
        out_specs=[pl.BlockSpec((tm, d), row), pl.BlockSpec((tm, CA_W), row)],
        out_shape=[jax.ShapeDtypeStruct((m, d), F32), jax.ShapeDtypeStruct((m, CA_W), q_dtype)],
        compiler_params=_cparams("parallel"),
        name="mixout",
    )(x, og, od, wo, nc, wq, qn)


def _memkv_kernel(m_ref, nm_ref, wk_ref, wv_ref, kn_ref, mk_ref, mv_ref):
    mn = _rms(m_ref[...], nm_ref[...]).astype(BF16)
    mk_ref[...] = _head_rms(_dot(mn, wk_ref[...]), kn_ref[...], CA_DH)
    mv_ref[...] = _dot(mn, wv_ref[...])


def _memkv(mem, nm, wk, wv, kn, *, tm, layer):
    m, d = mem.shape
    row = lambda i: (i, 0)
    return pl.pallas_call(
        _memkv_kernel,
        grid=(m // tm,),
        in_specs=[pl.BlockSpec((tm, d), row), _layer_spec(nm, layer), _weight_spec(wk.shape, layer),
                  _weight_spec(wv.shape, layer), _layer_spec(kn, layer)],
        out_specs=[pl.BlockSpec((tm, CA_W), row)] * 2,
        out_shape=[jax.ShapeDtypeStruct((m, CA_W), F32)] * 2,
        compiler_params=_cparams("parallel"),
        name="memkv",
    )(mem, nm, wk, wv, kn)


def _memory_attend(qs, ks, vs):
    s = [_dot_nt(q, k) for q, k in zip(qs, ks)]
    e = [jnp.exp(si - jnp.max(si, axis=-1, keepdims=True)) for si in s]
    pr = [(ei / jnp.sum(ei, axis=-1, keepdims=True)).astype(BF16) for ei in e]
    return [_dot(p, v) for p, v in zip(pr, vs)]


def _head_cols(h):
    return slice(h * CA_DH, (h + 1) * CA_DH)


def _cross_kernel(q_ref, mk_ref, mv_ref, o_ref, pad_scr, *, n_valid):
    groups = range(q_ref.shape[0])
    pad_scr[...] = jnp.zeros(pad_scr.shape, F32)
    for g in groups:
        pad_scr[g, 0:n_valid, :] = q_ref[g].astype(F32)
    q = [pad_scr[g].astype(BF16) for g in groups]
    units = [(g, h) for g in groups for h in range(CA_H)]
    outs = _memory_attend([q[g][:, _head_cols(h)] for g, h in units],
                          [mk_ref[g, :, h, :].astype(BF16) for g, h in units],
                          [mv_ref[g, :, h, :].astype(BF16) for g, h in units])
    for (g, h), oh in zip(units, outs):
        o_ref[g, :, _head_cols(h)] = oh[0:n_valid].astype(o_ref.dtype)


def _cross(q, mk, mv, *, layer, group):
    nb, seq, _ = q.shape
    mem_spec = pl.BlockSpec((None, group, mk.shape[2], CA_H, CA_DH), lambda b: (layer, b, 0, 0, 0))
    q_spec = pl.BlockSpec((group, seq, CA_W), lambda b: (b, 0, 0))
    return pl.pallas_call(
        functools.partial(_cross_kernel, n_valid=seq),
        grid=(nb // group,),
        in_specs=[q_spec, mem_spec, mem_spec],
        out_specs=q_spec,
        out_shape=jax.ShapeDtypeStruct((nb, seq, CA_W), BF16),
        scratch_shapes=[pltpu.VMEM((group, 8, CA_W), F32)],
        compiler_params=_cparams("parallel"),
        name="cross_attn",
    )(q, mk, mv)


def _ffn_kernel(x_ref, oc_ref, *rest, tm, stride, ncol, attend):
    if attend:
        mk_ref, mv_ref = rest[:2]
        rest = rest[2:]
    wco_ref, nf_ref, wg_ref, wu_ref, cw_ref, wd_ref, hist_ref, y_ref, hout_ref, hbuf, gbuf, acc = rest
    j = pl.program_id(1)
    d_ff = wg_ref.shape[1]
    hrows = (FFN_CONV - 1) * stride
    base = hbuf.shape[0]

    @pl.when(j == 0)
    def _():
        hbuf[base - hrows:base, :] = hist_ref[0]

    if attend:
        q = oc_ref[...]
        heads = range(CA_H)
        oc = jnp.concatenate(_memory_attend([q[:, _head_cols(h)] for h in heads],
                                            [mk_ref[0, :, _head_cols(h)].astype(BF16) for h in heads],
                                            [mv_ref[0, :, _head_cols(h)].astype(BF16) for h in heads]),
                             axis=1).astype(BF16)
    else:
        oc = oc_ref[...]
    x2 = x_ref[...] + _dot(oc, wco_ref[...])
    hf = _rms(x2, nf_ref[...]).astype(BF16)
    acc[...] = x2
    for c0 in range(0, d_ff, ncol):
        cs = slice(c0, c0 + ncol)
        gbuf[base - hrows:base, :] = hbuf[base - hrows:base, cs]
        gbuf[base:base + tm, :] = _dot(hf, wg_ref[:, cs])
        hbuf[base - hrows:base, cs] = gbuf[base + tm - hrows:base + tm, :]
        w = cw_ref[:, cs]
        gt = gbuf[base - hrows:base - hrows + tm, :] * w[0:1]
        for t in range(1, FFN_CONV):
            off = base - hrows + t * stride
            gt = gt + gbuf[off:off + tm, :] * w[t:t + 1]
        act = (_silu(gt) * _dot(hf, wu_ref[:, cs])).astype(BF16)
        acc[...] += _dot(act, wd_ref[cs, :])
    y_ref[...] = acc[...]
    hout_ref[0] = hbuf[base - hrows:base, :]


def _ffn(x, oc, wco, nf, wg, wu, cw, wd, hist, *, ngroups, tm, stride, ncol, layer, mem=None):
    m, d = x.shape
    d_ff = wg.shape[-1]
    nt = m // (ngroups * tm)
    hrows = (FFN_CONV - 1) * stride
    base = -(-hrows // 8) * 8
    row = lambda b, j: (b * nt + j, 0)
    grp = lambda b, j: (b, 0, 0)
    mem = tuple(mem or ())
    return pl.pallas_call(
        functools.partial(_ffn_kernel, tm=tm, stride=stride, ncol=ncol, attend=bool(mem)),
        grid=(ngroups, nt),
        in_specs=[pl.BlockSpec((tm, d), row), pl.BlockSpec((tm, CA_W), row)]
                 + [pl.BlockSpec((1,) + a.shape[1:], grp) for a in mem]
                 + [_weight_spec(wco.shape, layer),
                  _layer_spec(nf, layer), _weight_spec(wg.shape, layer), _weight_spec(wu.shape, layer),
                  _layer_spec(cw, layer), _weight_spec(wd.shape, layer),
                  pl.BlockSpec((1, hrows, d_ff), grp)],
        out_specs=[pl.BlockSpec((tm, d), row), pl.BlockSpec((1, hrows, d_ff), grp)],
        out_shape=[jax.ShapeDtypeStruct((m, d), F32), jax.ShapeDtypeStruct((ngroups, hrows, d_ff), F32)],
        scratch_shapes=[pltpu.VMEM((base, d_ff), F32), pltpu.VMEM((base + tm, ncol), F32),
                        pltpu.VMEM((tm, d), F32)],
        compiler_params=_cparams("parallel", "arbitrary"),
        name="ffn",
    )(x, oc, *mem, wco, nf, wg, wu, cw, wd, hist)


def _rotary_tables(pos):
    half = DIFF_DH // 2
    inv = ROPE_THETA ** (-jnp.arange(half, dtype=F32) / half)
    ang = pos.astype(F32)[:, None] * inv[None, :]
    cos, sin, zero = jnp.cos(ang), jnp.sin(ang), jnp.zeros_like(ang)
    reps = (1, LANE // DIFF_DH)
    return (jnp.tile(jnp.concatenate([cos, cos], axis=1), reps),
            jnp.tile(jnp.concatenate([-sin, zero], axis=1), reps),
            jnp.tile(jnp.concatenate([zero, sin], axis=1), reps))


def kernel(x_prompt, x_sample, mem_prompt, cache_k, cache_v, page_table, state_gdn, state_gdn_conv, cache_mem_k, cache_mem_v, state_ffn_conv, norm_mix, w_in, conv_qkv, a_log, dt_bias, gdn_norm, qnorm_diff, knorm_diff, lam_q1, lam_k1, lam_q2, lam_k2, diff_norm, w_out, norm_cross, norm_mem, w_cq, w_ck, w_cv, qnorm_cross, knorm_cross, w_co, norm_ffn, w_gate, w_up, conv_ffn, w_down):
    depth = w_in.shape[0]
    bp, lp, d = x_prompt.shape
    bs, ls, _ = x_sample.shape
    mlen = mem_prompt.shape[1]
    d_ff = w_gate.shape[2]
    past_len = page_table.shape[1] * PAGE
    mp, ms = bp * lp, bs * ls

    cos_p, sa_p, sb_p = _rotary_tables(jnp.arange(lp, dtype=jnp.int32))
    cos_s, sa_s, sb_s = [jnp.tile(t, (bs, 1)) for t in
                         _rotary_tables(past_len + jnp.arange(ls, dtype=jnp.int32))]
    ii = jnp.arange(DIFF_W)
    gmat = jnp.where(ii[:, None] // DIFF_DH == ii[None, :] // DIFF_DH, 1.0 / DIFF_DH, 0.0).astype(BF16)
    c_qkv, c_gate = 3 * GDN_W, 4 * GDN_W
    c_ba = c_gate + 2 * GDN_H

    xp = x_prompt.reshape(mp, d)
    xs = x_sample.reshape(ms, d)
    memp = mem_prompt.reshape(bp * mlen, d)
    zeros_s0 = jnp.zeros((bp, GDN_H, GDN_D, GDN_D), F32)
    zeros_cb = jnp.zeros((bp, GDN_CONV - 1, 3 * GDN_W), F32)
    zeros_fb = jnp.zeros((bp, FFN_CONV - 1, d_ff), F32)

    wb = {name: w.astype(BF16) for name, w in
          dict(out=w_out, cq=w_cq, ck=w_ck, cv=w_cv, co=w_co, gate=w_gate, up=w_up, down=w_down).items()}

    w_in_t = w_in.transpose(0, 2, 1)
    wb["in"] = jnp.concatenate([w_in_t[:, :c_gate], w_in_t[:, c_ba:], w_in_t[:, c_gate:c_ba],
                                jnp.zeros((depth, LANE - 2 * GDN_H, d), F32)], axis=1).astype(BF16)
    rows = lambda a: a.astype(F32)[:, None, :]
    lane_pad = lambda a: jnp.pad(rows(a), ((0, 0), (0, 0), (GDN_H, LANE - 2 * GDN_H)))
    head_tile = lambda a: jnp.tile(rows(a), (1, 1, 2 * DIFF_H))
    lams = (rows(lam_q1), rows(lam_k1), rows(lam_q2), rows(lam_k2))
    dn = rows(diff_norm)

    p_out, s_out, kv_stack = [], [], ()
    for l in range(depth):
        lam_init = 0.8 - 0.6 * math.exp(-0.3 * l)
        inproj = functools.partial(_inproj, gain=rows(norm_mix), w=wb["in"], qg=head_tile(qnorm_diff),
                                   kg=head_tile(knorm_diff), gmat=gmat, layer=l)
        gdn = functools.partial(_gdn, cw=conv_qkv, alog=lane_pad(a_log), dtb=lane_pad(dt_bias),
                                gn=rows(gdn_norm), layer=l)
        mixout = functools.partial(_mixout, wo=wb["out"], nc=rows(norm_cross), wq=wb["cq"],
                                   qn=rows(qnorm_cross), layer=l)
        ffn = functools.partial(_ffn, wco=wb["co"], nf=rows(norm_ffn), wg=wb["gate"], wu=wb["up"],
                                cw=conv_ffn, wd=wb["down"], layer=l)

        mk_p, mv_p = _memkv(memp, rows(norm_mem), wb["ck"], wb["cv"], rows(knorm_cross),
                            tm=math.gcd(bp * mlen, ROWS_MEM), layer=l)
        qkv, gate, ba, q, kt, k16, v2, vt16 = inproj(xp, cos=cos_p, sa=sa_p, sb=sb_p, q_dtype=BF16, seq=lp,
                                                     tm=math.gcd(lp, ROWS_PROJ_STACKING if kv_stack else ROWS_PROJ),
                                                     prev=kv_stack)
        kv_stack = (kt, v2)
        og, sp = gdn(qkv, gate, ba, s0=zeros_s0, cb=zeros_cb, nb=bp, seq=lp, lt=math.gcd(lp, ROWS_GDN))
        od = _flash(q, k16, vt16, lams, dn, nb=bp, seq=lp, tq=math.gcd(lp, ROWS_FLASH), lam_init=lam_init,
                    layer=l)
        x1, qc = mixout(xp, og, od, tm=math.gcd(lp, ROWS_PROJ), q_dtype=BF16)
        xp, fp = ffn(x1, qc, hist=zeros_fb, ngroups=bp, tm=math.gcd(lp, ROWS_MLP), stride=1, ncol=_mlp_cols(d_ff),
                     mem=(mk_p.reshape(bp, mlen, CA_W), mv_p.reshape(bp, mlen, CA_W)))
        p_out.append((sp, qkv.reshape(bp, lp, c_qkv)[:, lp - (GDN_CONV - 1):],
                      mk_p.reshape(bp, mlen, CA_H, CA_DH), mv_p.reshape(bp, mlen, CA_H, CA_DH), fp))

        qkv, gate, ba, q, k, v = inproj(xs, cos=cos_s, sa=sa_s, sb=sb_s, tm=ms, q_dtype=F32)
        r3 = lambda a: a.reshape(bs, ls, a.shape[-1])
        og, ss = gdn(r3(qkv), r3(gate), r3(ba), s0=state_gdn[l], cb=state_gdn_conv[l], nb=bs, seq=ls,
                     lt=GDN_CHUNK)
        od = _decode(page_table, r3(q), r3(k), r3(v), cache_k, cache_v, lams, dn,
                     layer=l, ppg=math.gcd(page_table.shape[1], PAGES_PER_STEP), lam_init=lam_init)
        x1, qc = mixout(xs, og.reshape(ms, GDN_W), od.reshape(ms, DIFF_W), tm=ms, q_dtype=F32)
        oc = _cross(r3(qc), cache_mem_k, cache_mem_v, layer=l, group=math.gcd(bs, SEQS_PER_STEP))
        tmaj = lambda a: a.reshape(bs, ls, -1).transpose(1, 0, 2).reshape(ms, -1)
        hist = state_ffn_conv[l].transpose(1, 0, 2).reshape(1, (FFN_CONV - 1) * bs, d_ff)
        y_t, fs_t = ffn(tmaj(x1), tmaj(oc), hist=hist, ngroups=1, tm=ms, stride=bs, ncol=_mlp_cols(d_ff))
        xs = y_t.reshape(ls, bs, d).transpose(1, 0, 2).reshape(ms, d)
        fs = fs_t.reshape(FFN_CONV - 1, bs, d_ff).transpose(1, 0, 2)
        conv_s = jnp.concatenate([state_gdn_conv[l], r3(qkv)], axis=1)[:, ls:]
        s_out.append((k.reshape(bs, ls, 2 * DIFF_H, DIFF_DH), v.reshape(bs, ls, DIFF_H, DIFF_DV), ss, conv_s, fs))

    pg, pc, pmk, pmv, pf = [jnp.stack(t, axis=0) for t in zip(*p_out)]
    pk = kv_stack[0].reshape(depth, bp, 2 * DIFF_H, DIFF_DH, lp).transpose(0, 1, 4, 2, 3)
    pv = kv_stack[1].reshape(depth, bp, lp, DIFF_H, DIFF_DV)
    sk, sv, sg, sc, sf = [jnp.stack(t, axis=0) for t in zip(*s_out)]
    return (xp.reshape(bp, lp, d), xs.reshape(bs, ls, d), pk, pv, pg, pc, pmk, pmv, pf, sk, sv, sg, sc, sf)
```

```python
import functools
import math

import jax
import jax.numpy as jnp
from jax import lax
from jax.experimental import pallas as pl
from jax.experimental.pallas import tpu as pltpu

F32 = jnp.float32
BF16 = jnp.bfloat16

EPS = 1e-6
ROPE_THETA = 10000.0
GDN_H = 4
GDN_D = 128
GDN_W = GDN_H * GDN_D
GDN_CONV = 4
GDN_CHUNK = 64
DIFF_H = 4
DIFF_DH = 64
DIFF_DV = 128
DIFF_W = DIFF_H * DIFF_DV
CA_H = 4
CA_DH = 128
CA_W = CA_H * CA_DH
FFN_CONV = 3
PAGE = 128
LANE = 128
VMEM_LIMIT = 56 * 1024 * 1024
NOT_SAME_CHUNK = 99.0
LOG2E = math.log2(math.e)

ROWS_PROJ = 1024
ROWS_PROJ_STACKING = 512
ROWS_SUB = 256
ROWS_MLP = 512
ROWS_MEM = 512
ROWS_GDN = 512
ROWS_FLASH = 512
PAGES_PER_STEP = 32
SEQS_PER_STEP = 4


def _mlp_cols(d_ff):
    return max(c for c in range(LANE, d_ff // 2 + 1, LANE) if d_ff % c == 0)


def _cparams(*sem):
    return pltpu.CompilerParams(dimension_semantics=sem, vmem_limit_bytes=VMEM_LIMIT)


def _dot(a, b):
    return jnp.dot(a, b, preferred_element_type=F32)


def _dot_nt(a, b):
    return lax.dot_general(a, b, (((1,), (1,)), ((), ())), preferred_element_type=F32)


def _dot_tn(a, b):
    return lax.dot_general(a, b, (((0,), (0,)), ((), ())), preferred_element_type=F32)


def _sigmoid(x):
    return 1.0 / (1.0 + jnp.exp(-x))


def _silu(x):
    return x * _sigmoid(x)


def _softplus(x):
    return jnp.maximum(x, 0.0) + jnp.log1p(jnp.exp(-jnp.abs(x)))


def _rms(x, gain):
    return x * lax.rsqrt(jnp.mean(x * x, axis=-1, keepdims=True) + EPS) * gain


def _const_spec(shape):
    nd = len(shape)
    return pl.BlockSpec(shape, lambda *_: (0,) * nd)


def _layer_spec(a, layer):
    return pl.BlockSpec((None,) + tuple(a.shape[1:]), lambda *_: (layer, 0, 0))


def _weight_spec(shape, layer=None):
    if layer is None:
        nd = len(shape)
        return pl.BlockSpec(shape, lambda *_: (0,) * nd, pipeline_mode=pl.Buffered(1))
    nd = len(shape) - 1
    return pl.BlockSpec((None,) + tuple(shape[1:]), lambda *_: (layer,) + (0,) * nd, pipeline_mode=pl.Buffered(1))


def _inproj_kernel(x_ref, gain_ref, w_ref, qg_ref, kg_ref, cos_ref, sa_ref, sb_ref, gmat_ref, *refs, n_prev):
    prev_refs, (qkv_ref, gate_ref, ba_ref, q_ref, *kv_refs) = refs[:2 * (n_prev > 0)], refs[2 * (n_prev > 0):]
    c_qkv, c_gate = 3 * GDN_W, 4 * GDN_W
    c_q, c_k, c_v = c_gate + DIFF_W, c_gate + 2 * DIFF_W, c_gate + 3 * DIFF_W
    half = DIFF_DH // 2
    widen = lambda t: jnp.concatenate([t] * (DIFF_W // LANE), axis=1)
    tm = x_ref.shape[0]
    sub = math.gcd(tm, ROWS_SUB)
    parts = [slice(r, r + sub) for r in range(0, tm, sub)]
    hs = [_rms(x_ref[p, :], gain_ref[...]).astype(BF16) for p in parts]

    def proj(a, b):
        return [_dot_nt(h, w_ref[a:b, :]) for h in hs]

    def norm_rot(zs, g):
        ms = [_dot((z * z).astype(BF16), gmat_ref[...]) for z in zs]
        out = []
        for p, z, m in zip(parts, zs, ms):
            cos, sa, sb = widen(cos_ref[p, :]), widen(sa_ref[p, :]), widen(sb_ref[p, :])
            y = z * lax.rsqrt(m + EPS) * g
            out.append(y * cos + pltpu.roll(y, DIFF_W - half, 1) * sa + pltpu.roll(y, half, 1) * sb)
        return out

    for p, z in zip(parts, proj(0, c_qkv)):
        qkv_ref[p, :] = z
    for p, z in zip(parts, proj(c_qkv, c_gate)):
        gate_ref[p, :] = z.astype(gate_ref.dtype)
    vs = proj(c_k, c_v)
    for p, z in zip(parts, proj(c_v, c_v + LANE)):
        ba_ref[p, :] = z
    for p, z in zip(parts, norm_rot(proj(c_gate, c_q), qg_ref[...])):
        q_ref[p, :] = (z * (LOG2E * DIFF_DH ** -0.5)).astype(q_ref.dtype)
    ks = norm_rot(proj(c_q, c_k), kg_ref[...])
    if len(kv_refs) == 2:
        for p, k, v in zip(parts, ks, vs):
            kv_refs[0][p, :] = k
            kv_refs[1][p, :] = v
    else:
        kt_ref, k16_ref, v2_ref, vt16_ref = kv_refs
        if n_prev:
            kt_ref[0:n_prev] = prev_refs[0][...]
            v2_ref[0:n_prev] = prev_refs[1][...]
        for p, k, v in zip(parts, ks, vs):
            kt_ref[n_prev, :, p] = k.T
            k16_ref[p, :] = k.astype(BF16)
            vt16_ref[:, p] = v.T.astype(BF16)
            for hd in range(DIFF_H):
                v2_ref[n_prev, pl.ds(DIFF_H * p.start + hd, sub, stride=DIFF_H), :] = v[:, hd * DIFF_DV:(hd + 1) * DIFF_DV]


def _inproj(x, gain, w, qg, kg, cos, sa, sb, gmat, *, tm, q_dtype, layer, seq=None, prev=()):
    m, d = x.shape
    npos = cos.shape[0] // tm
    row = lambda i: (i, 0)
    pos = lambda i: (i % npos, 0)
    n_prev = prev[0].shape[0] if prev else 0
    if seq is None:
        kv_specs = [pl.BlockSpec((tm, DIFF_W), row)] * 2
        kv_shapes = [jax.ShapeDtypeStruct((m, DIFF_W), F32)] * 2
        prev_specs = []
    else:
        nt, nb = seq // tm, m // seq
        kcols = lambda i: (0, i // nt, 0, i % nt)
        vrows = lambda i: (0, i // nt, i % nt, 0)
        kt_spec = lambda n: pl.BlockSpec((n, None, DIFF_W, tm), kcols)
        v2_spec = lambda n: pl.BlockSpec((n, None, tm * DIFF_H, DIFF_DV), vrows)
        kv_specs = [kt_spec(n_prev + 1), pl.BlockSpec((tm, DIFF_W), row), v2_spec(n_prev + 1),
                    pl.BlockSpec((None, DIFF_W, tm), lambda i: (i // nt, 0, i % nt))]
        kv_shapes = [jax.ShapeDtypeStruct((n_prev + 1, nb, DIFF_W, seq), F32),
                     jax.ShapeDtypeStruct((m, DIFF_W), BF16),
                     jax.ShapeDtypeStruct((n_prev + 1, nb, seq * DIFF_H, DIFF_DV), F32),
                     jax.ShapeDtypeStruct((nb, DIFF_W, seq), BF16)]
        prev_specs = [kt_spec(n_prev), v2_spec(n_prev)] if prev else []
    return pl.pallas_call(
        functools.partial(_inproj_kernel, n_prev=n_prev),
        grid=(m // tm,),
        in_specs=[pl.BlockSpec((tm, d), row), _layer_spec(gain, layer), _weight_spec(w.shape, layer),
                  _layer_spec(qg, layer), _layer_spec(kg, layer),
                  pl.BlockSpec((tm, LANE), pos), pl.BlockSpec((tm, LANE), pos),
                  pl.BlockSpec((tm, LANE), pos), _const_spec((DIFF_W, DIFF_W))] + prev_specs,
        out_specs=[pl.BlockSpec((tm, 3 * GDN_W), row), pl.BlockSpec((tm, GDN_W), row),
                   pl.BlockSpec((tm, LANE), row), pl.BlockSpec((tm, DIFF_W), row)] + kv_specs,
        out_shape=[jax.ShapeDtypeStruct((m, 3 * GDN_W), F32), jax.ShapeDtypeStruct((m, GDN_W), BF16),
                   jax.ShapeDtypeStruct((m, LANE), F32), jax.ShapeDtypeStruct((m, DIFF_W), q_dtype)] + kv_shapes,
        compiler_params=_cparams("parallel"),
        name="inproj",
    )(x, gain, w, qg, kg, cos, sa, sb, gmat, *prev)


def _gdn_kernel(x_ref, gate_ref, ba_ref, cw_ref, alog_ref, dtb_ref, gn_ref, s0_ref, cb_ref, tri_ref, lv_ref,
                lvl_ref, o_ref, sout_ref, xbuf, s_scr, *pad_scr, lt, n_valid, chunk):
    j = pl.program_id(1)
    padded = n_valid < lt
    tail = GDN_CONV - 1

    @pl.when(j == 0)
    def _():
        for h in range(GDN_H):
            s_scr[h] = s0_ref[0, h].T
        xbuf[0:8, :] = jnp.zeros((8, 3 * GDN_W), F32)
        xbuf[8 - tail:8, :] = cb_ref[0]

    @pl.when(j > 0)
    def _():
        xbuf[0:8, :] = xbuf[lt:lt + 8, :]

    if padded:
        gate_scr, ba_scr = pad_scr
        xbuf[8:8 + lt, :] = jnp.zeros((lt, 3 * GDN_W), F32)
        xbuf[8:8 + n_valid, :] = x_ref[0]
        gate_scr[...] = jnp.zeros(gate_scr.shape, F32)
        gate_scr[0:n_valid, :] = gate_ref[0].astype(F32)
        ba_scr[...] = jnp.zeros(ba_scr.shape, F32)
        ba_scr[0:n_valid, :] = ba_ref[0]
        gate_src, ba = gate_scr, ba_scr[...]
    else:
        xbuf[8:8 + lt, :] = x_ref[...]
        gate_src, ba = gate_ref, ba_ref[...]

    def conv_act(c0):
        w = cw_ref[:, c0:c0 + GDN_D]
        y = xbuf[8 - tail:8 - tail + lt, c0:c0 + GDN_D] * w[0:1]
        for t in range(1, GDN_CONV):
            y = y + xbuf[8 - tail + t:8 - tail + t + lt, c0:c0 + GDN_D] * w[t:t + 1]
        return _silu(y)

    def l2n(z):
        return z * lax.rsqrt(jnp.sum(z * z, axis=-1, keepdims=True) + EPS)

    beta_all = _sigmoid(ba)
    g_all = -jnp.exp(alog_ref[...]) * _softplus(ba + dtb_ref[...])
    if padded:
        valid = lax.broadcasted_iota(jnp.int32, (lt, 1), 0) < n_valid
        beta_all = jnp.where(valid, beta_all, 0.0)
        g_all = jnp.where(valid, g_all, 0.0)
    g_hi = g_all.astype(BF16)
    g_lo = (g_all - g_hi.astype(F32)).astype(BF16)
    gc_all = _dot(tri_ref[...], g_hi) + _dot(tri_ref[...], g_lo)
    gc_rows = gc_all.T

    lv = lv_ref[...]
    bl = lv.shape[0]
    incl = lv < NOT_SAME_CHUNK
    strict = jnp.logical_and(incl, lv >= 0.0)
    eye = jnp.where(lv == -1.0, 1.0, 0.0)
    lev0 = lv == 0.0
    n_levels = int(math.log2(chunk))
    run_levels = min(n_levels, max(1, math.ceil(math.log2(n_valid)))) if padded else n_levels
    heads = range(GDN_H)
    units = [(h, r) for r in range(0, lt, bl) for h in heads]

    gcol = [gc_all[:, GDN_H + h:GDN_H + h + 1] for h in heads]
    bcol = [beta_all[:, h:h + 1] for h in heads]
    eg = [jnp.exp(gcol[h]) for h in heads]
    q = [l2n(conv_act(h * GDN_D)) * (GDN_D ** -0.5) for h in heads]
    k = [l2n(conv_act(GDN_W + h * GDN_D)) for h in heads]
    rhs = [jnp.concatenate([conv_act(2 * GDN_W + h * GDN_D) * bcol[h], k[h] * (bcol[h] * eg[h])],
                           axis=1).astype(BF16) for h in heads]
    qb = [q[h].astype(BF16) for h in heads]
    kb = [k[h].astype(BF16) for h in heads]
    qg = [q[h] * eg[h] for h in heads]
    mm, aqk = [], []
    for h, r in units:
        decay = jnp.where(incl, jnp.exp(gcol[h][r:r + bl] - gc_rows[GDN_H + h:GDN_H + h + 1, r:r + bl]), 0.0)
        mm.append(jnp.where(strict, bcol[h][r:r + bl] * _dot_nt(kb[h][r:r + bl], kb[h][r:r + bl]) * decay, 0.0))
        aqk.append((_dot_nt(qb[h][r:r + bl], kb[h][r:r + bl]) * decay).astype(BF16))
    x = [eye - jnp.where(lev0, m, 0.0) for m in mm]
    mmb = [m.astype(BF16) for m in mm]
    for lev in range(1, run_levels):
        xb = [xi.astype(BF16) for xi in x]
        y = [_dot(mmb[i] * lvl_ref[lev], xb[i]).astype(BF16) for i in range(len(units))]
        x = [x[i] - _dot(xb[i], y[i]) for i in range(len(units))]
    uwb_u = [_dot(x[i].astype(BF16), rhs[h][r:r + bl]).astype(BF16) for i, (h, r) in enumerate(units)]
    ou_u = [_dot(aqk[i], uwb_u[i]) for i in range(len(units))]
    per_head = lambda vals, h: jnp.concatenate([vals[i] for i, (hh, _) in enumerate(units) if hh == h], axis=0)
    uwb = [per_head(uwb_u, h) for h in heads]
    ou = [per_head(ou_u, h) for h in heads]
    qp = [(qg[h] - ou[h][:, GDN_D:]).astype(BF16) for h in heads]
    for c in range(lt // chunk):
        r0, r1 = c * chunk, (c + 1) * chunk
        for h in heads:
            glast = gcol[h][r1 - 1:r1, :]
            kg = (k[h][r0:r1] * jnp.exp(glast - gcol[h][r0:r1])).astype(BF16)
            ab = _dot_tn(uwb[h][r0:r1, :], kg)
            st = s_scr[h]
            stb = st.astype(BF16)
            o = _dot_nt(qp[h][r0:r1], stb) + ou[h][r0:r1, :GDN_D]
            s_scr[h] = st * jnp.exp(glast) + ab[:GDN_D] - _dot(stb, ab[GDN_D:].astype(BF16))
            gate = gate_src[r0:r1, h * GDN_D:(h + 1) * GDN_D].astype(F32)
            on = _rms(o, gn_ref[...]) * _silu(gate)
            if padded:
                if r0 < n_valid:
                    o_ref[0, :, h * GDN_D:(h + 1) * GDN_D] = on[0:n_valid].astype(o_ref.dtype)
            else:
                o_ref[r0:r1, h * GDN_D:(h + 1) * GDN_D] = on.astype(o_ref.dtype)

    @pl.when(j == pl.num_programs(1) - 1)
    def _():
        for h in range(GDN_H):
            sout_ref[0, h] = s_scr[h].T


def _chunk_level_codes(lt, chunk):
    i = jnp.arange(lt)[:, None]
    j = jnp.arange(lt)[None, :]
    code = jnp.full((lt, lt), NOT_SAME_CHUNK, F32)
    for lev in reversed(range(int(math.log2(chunk)))):
        code = jnp.where((i // (2 << lev)) == (j // (2 << lev)), float(lev), code)
    code = jnp.where(i > j, code, NOT_SAME_CHUNK)
    return jnp.where(i == j, -1.0, code).astype(F32)


def _gdn(qkv, gate, ba, cw, alog, dtb, gn, s0, cb, *, nb, seq, lt, layer):
    padded = seq < lt
    n_valid = seq if padded else lt
    nt = 1 if padded else seq // lt
    chunk = GDN_CHUNK
    i = jnp.arange(lt)
    tri = ((i[:, None] // chunk == i[None, :] // chunk) & (i[:, None] >= i[None, :])).astype(BF16)
    bl = min(lt, 2 * chunk)
    lv = _chunk_level_codes(bl, chunk)
    n_levels = int(math.log2(chunk))
    lvl = jnp.stack([(lv == float(lev)).astype(BF16) for lev in range(n_levels)], axis=0)
    if padded:
        tile = lambda c: pl.BlockSpec((1, seq, c), lambda b, j: (b, 0, 0))
        out_o = jax.ShapeDtypeStruct((nb, seq, GDN_W), BF16)
        scratch = [pltpu.VMEM((lt, GDN_W), F32), pltpu.VMEM((lt, LANE), F32)]
    else:
        tile = lambda c: pl.BlockSpec((lt, c), lambda b, j: (b * nt + j, 0))
        out_o = jax.ShapeDtypeStruct((nb * seq, GDN_W), BF16)
        scratch = []
    return pl.pallas_call(
        functools.partial(_gdn_kernel, lt=lt, n_valid=n_valid, chunk=chunk),
        grid=(nb, nt),
        in_specs=[tile(3 * GDN_W), tile(GDN_W), tile(LANE),
                  _layer_spec(cw, layer), _layer_spec(alog, layer), _layer_spec(dtb, layer),
                  _layer_spec(gn, layer),
                  pl.BlockSpec((1, GDN_H, GDN_D, GDN_D), lambda b, j: (b, 0, 0, 0)),
                  pl.BlockSpec((1, GDN_CONV - 1, 3 * GDN_W), lambda b, j: (b, 0, 0)),
                  _const_spec((lt, lt)), _const_spec((bl, bl)), _const_spec((n_levels, bl, bl))],
        out_specs=[tile(GDN_W), pl.BlockSpec((1, GDN_H, GDN_D, GDN_D), lambda b, j: (b, 0, 0, 0))],
        out_shape=[out_o, jax.ShapeDtypeStruct((nb, GDN_H, GDN_D, GDN_D), F32)],
        scratch_shapes=[pltpu.VMEM((lt + 8, 3 * GDN_W), F32), pltpu.VMEM((GDN_H, GDN_D, GDN_D), F32)] + scratch,
        compiler_params=_cparams("parallel", "arbitrary"),
        name="gdn",
    )(qkv, gate, ba, cw, alog, dtb, gn, s0, cb, tri, lv, lvl)


def _lambda(l1q, l1k, l2q, l2k, lam_init):
    return (jnp.exp(jnp.sum(l1q[...] * l1k[...], axis=-1, keepdims=True))
            - jnp.exp(jnp.sum(l2q[...] * l2k[...], axis=-1, keepdims=True)) + lam_init)


def _flash_kernel(qi_tab, ki_tab, q_ref, k_ref, vt_ref, l1q, l1k, l2q, l2k, dn_ref, o_ref,
                  m_scr, acc_scr, *, tq, tk, lam_init):
    p = pl.program_id(1)
    qi, ki = qi_tab[p], ki_tab[p]

    @pl.when(ki == 0)
    def _():
        m_scr[...] = jnp.full(m_scr.shape, -jnp.inf, F32)
        acc_scr[...] = jnp.zeros(acc_scr.shape, F32)

    lane = lax.broadcasted_iota(jnp.int32, (1, DIFF_DV), 1)
    pairs = range(2 * DIFF_H)
    ones = jnp.ones((acc_scr.shape[1] - DIFF_DV, tk), BF16)

    def step(masked):
        kc, vt = [], []
        for h in range(DIFF_H):
            hs = slice(h * DIFF_DV, (h + 1) * DIFF_DV)
            kh = k_ref[:, hs]
            vt.append(jnp.concatenate([vt_ref[hs, :], ones], axis=0))
            kc += [jnp.where((lane // DIFF_DH) == c, kh, jnp.zeros_like(kh)) for c in range(2)]
        st = [_dot_nt(kc[i], q_ref[:, (i // 2) * DIFF_DV:(i // 2 + 1) * DIFF_DV]) for i in pairs]
        if masked:
            keys = ki * tk + lax.broadcasted_iota(jnp.int32, (tk, tq), 0)
            queries = qi * tq + lax.broadcasted_iota(jnp.int32, (tk, tq), 1)
            visible = keys <= queries
            st = [jnp.where(visible, st[i], -jnp.inf) for i in pairs]
        m_old = [m_scr[i:i + 1, :] for i in pairs]
        m_new = [jnp.maximum(m_old[i], jnp.max(st[i], axis=0, keepdims=True)) for i in pairs]
        pt = [jnp.exp2(st[i] - m_new[i]).astype(BF16) for i in pairs]
        alpha = [jnp.exp2(m_old[i] - m_new[i]) for i in pairs]
        for i in pairs:
            acc_scr[i] = alpha[i] * acc_scr[i] + _dot(vt[i // 2], pt[i])
            m_scr[i:i + 1, :] = m_new[i]

    @pl.when(ki < qi)
    def _():
        step(False)

    @pl.when(ki == qi)
    def _():
        step(True)
        lam = _lambda(l1q, l1k, l2q, l2k, lam_init)
        for h in range(DIFF_H):
            a0, a1 = acc_scr[2 * h], acc_scr[2 * h + 1]
            ot = (a0[:DIFF_DV] / a0[DIFF_DV:DIFF_DV + 1]
                  - lam * (a1[:DIFF_DV] / a1[DIFF_DV:DIFF_DV + 1]))
            o_ref[:, h * DIFF_DV:(h + 1) * DIFF_DV] = (_rms(ot.T, dn_ref[...]) * (1.0 - lam_init)).astype(o_ref.dtype)


def _flash(q, k, v, lams, dn, *, nb, seq, tq, lam_init, layer):
    nq = seq // tq
    pairs = [(a, b) for a in range(nq) for b in range(a + 1)]
    qi_tab = jnp.array([a for a, _ in pairs], jnp.int32)
    ki_tab = jnp.array([b for _, b in pairs], jnp.int32)
    qmap = lambda b, p, qt, kt: (b * nq + qt[p], 0)
    kmap = lambda b, p, qt, kt: (b * nq + kt[p], 0)
    grid_spec = pltpu.PrefetchScalarGridSpec(
        num_scalar_prefetch=2,
        grid=(nb, len(pairs)),
        in_specs=[pl.BlockSpec((tq, DIFF_W), qmap), pl.BlockSpec((tq, DIFF_W), kmap),
                  pl.BlockSpec((None, DIFF_W, tq), lambda b, p, qt, kt: (b, 0, kt[p]))]
                 + [_layer_spec(a, layer) for a in (*lams, dn)],
        out_specs=pl.BlockSpec((tq, DIFF_W), qmap),
        scratch_shapes=[pltpu.VMEM((2 * DIFF_H, tq), F32),
                        pltpu.VMEM((2 * DIFF_H, DIFF_DV + 16, tq), F32)],
    )
    return pl.pallas_call(
        functools.partial(_flash_kernel, tq=tq, tk=tq, lam_init=lam_init),
        grid_spec=grid_spec,
        out_shape=jax.ShapeDtypeStruct((nb * seq, DIFF_W), BF16),
        compiler_params=_cparams("parallel", "arbitrary"),
        name="diff_flash",
    )(qi_tab, ki_tab, q, k, v, *lams, dn)


def _decode_kernel(pt_ref, q_ref, qmask_ref, kn_ref, vn_ref, expand_ref, hmask_ref, l1q, l1k, l2q, l2k, dn_ref,
                   *rest, n_new, ppg, lam_init):
    kpages, vpages = rest[:ppg], rest[ppg:2 * ppg]
    o_ref = rest[2 * ppg]
    qb_scr, kpad_scr, vpad_scr, m_scr, l_scr, acc_scr = rest[2 * ppg + 1:]
    g = pl.program_id(1)
    nrow = 2 * DIFF_H * 8

    @pl.when(g == 0)
    def _():
        kpad_scr[...] = jnp.zeros(kpad_scr.shape, F32)
        kpad_scr[0:n_new, :] = q_ref[0].astype(F32)
        q8 = kpad_scr[0:8, :]
        qb_scr[...] = (jnp.concatenate([q8] * (2 * DIFF_H), axis=0) * qmask_ref[...]).astype(BF16)
        m_scr[...] = jnp.full(m_scr.shape, -jnp.inf, F32)
        l_scr[...] = jnp.zeros(l_scr.shape, F32)
        acc_scr[...] = jnp.zeros(acc_scr.shape, F32)

    def update(s, vals):
        m_old = m_scr[...]
        m_new = jnp.maximum(m_old, jnp.max(s, axis=-1, keepdims=True))
        alpha = jnp.exp2(m_old - m_new)
        pr = jnp.exp2(s - m_new)
        l_scr[...] = alpha * l_scr[...] + jnp.sum(pr, axis=-1, keepdims=True)
        prb = pr.astype(BF16)
        n = len(vals)
        stacked = prb if n == 1 else jnp.concatenate([prb[:, jj * PAGE:(jj + 1) * PAGE] for jj in range(n)], axis=0)
        spread = _dot(stacked, expand_ref[...])
        pes = [(spread[jj * nrow:(jj + 1) * nrow] * hmask_ref[...]).astype(BF16) for jj in range(n)]
        terms = [_dot(pes[jj], vals[jj]) for jj in range(n)]
        while len(terms) > 1:
            terms = [terms[i] + terms[i + 1] for i in range(0, len(terms) - 1, 2)] + terms[len(terms) & ~1:]
        acc_scr[...] = alpha * acc_scr[...] + terms[0]
        m_scr[...] = m_new

    qb = qb_scr[...]
    s = jnp.concatenate([_dot(qb, kp[...].astype(BF16)) for kp in kpages], axis=1)
    update(s, [vp[...].astype(BF16) for vp in vpages])

    @pl.when(g == pl.num_programs(1) - 1)
    def _():
        kpad_scr[...] = jnp.zeros(kpad_scr.shape, F32)
        kpad_scr[0:n_new, :] = kn_ref[0]
        vpad_scr[...] = jnp.zeros(vpad_scr.shape, F32)
        vpad_scr[0:n_new * DIFF_H, :] = vn_ref[0]
        tok = lax.broadcasted_iota(jnp.int32, (nrow, PAGE), 0) % 8
        key = lax.broadcasted_iota(jnp.int32, (nrow, PAGE), 1)
        ok = jnp.logical_and(key < n_new, key <= tok)
        update(jnp.where(ok, _dot_nt(qb, kpad_scr[...].astype(BF16)), -jnp.inf), [vpad_scr[...].astype(BF16)])
        lam = _lambda(l1q, l1k, l2q, l2k, lam_init)
        on = acc_scr[...] / l_scr[...]
        half = nrow // 2
        for h in range(DIFF_H):
            o = on[8 * h:8 * h + 8, :] - lam * on[half + 8 * h:half + 8 * h + 8, :]
            o_ref[0, :, h * DIFF_DV:(h + 1) * DIFF_DV] = (_rms(o, dn_ref[...]) * (1.0 - lam_init)).astype(o_ref.dtype)


def _decode(page_table, q, kn, vn, cache_k, cache_v, lams, dn, *, layer, ppg, lam_init):
    nb, n_new, _ = q.shape
    n_pages = page_table.shape[1]
    nrow = 2 * DIFF_H * 8
    r = jnp.arange(nrow)
    hc = 2 * ((r // 8) % DIFF_H) + r // (nrow // 2)
    qmask = (jnp.arange(DIFF_W)[None, :] // DIFF_DH == hc[:, None]).astype(F32)
    vrows = PAGE * DIFF_H
    jj = jnp.arange(vrows)
    expand = (jj[None, :] // DIFF_H == jnp.arange(PAGE)[:, None]).astype(BF16)
    hmask = (jj[None, :] % DIFF_H == ((r // 8) % DIFF_H)[:, None]).astype(F32)
    per_b = lambda n, c: pl.BlockSpec((1, n, c), lambda b, g, pt: (b, 0, 0))
    cmap = lambda b, g, pt: (0, 0)

    def page_spec(j, rows, cols):
        return pl.BlockSpec((None, None, rows, cols),
                            lambda b, g, pt: (layer, pt[b * n_pages + g * ppg + j], 0, 0))

    grid_spec = pltpu.PrefetchScalarGridSpec(
        num_scalar_prefetch=1,
        grid=(nb, n_pages // ppg),
        in_specs=[per_b(n_new, DIFF_W), pl.BlockSpec((nrow, DIFF_W), cmap), per_b(n_new, DIFF_W),
                  per_b(n_new * DIFF_H, DIFF_DV), pl.BlockSpec((PAGE, vrows), cmap),
                  pl.BlockSpec((nrow, vrows), cmap)]
                 + [_layer_spec(a, layer) for a in (*lams, dn)]
                 + [page_spec(j, DIFF_W, PAGE) for j in range(ppg)]
                 + [page_spec(j, vrows, DIFF_DV) for j in range(ppg)],
        out_specs=pl.BlockSpec((1, 8, DIFF_W), lambda b, g, pt: (b, 0, 0)),
        scratch_shapes=[pltpu.VMEM((nrow, DIFF_W), BF16), pltpu.VMEM((PAGE, DIFF_W), F32),
                        pltpu.VMEM((vrows, DIFF_DV), F32),
                        pltpu.VMEM((nrow, 1), F32), pltpu.VMEM((nrow, 1), F32),
                        pltpu.VMEM((nrow, DIFF_DV), F32)],
    )
    ck = cache_k.transpose(0, 1, 3, 4, 2).reshape(cache_k.shape[0], cache_k.shape[1], DIFF_W, PAGE)
    cv = cache_v.reshape(cache_v.shape[0], cache_v.shape[1], vrows, DIFF_DV)
    out = pl.pallas_call(
        functools.partial(_decode_kernel, n_new=n_new, ppg=ppg, lam_init=lam_init),
        grid_spec=grid_spec,
        out_shape=jax.ShapeDtypeStruct((nb, 8, DIFF_W), BF16),
        compiler_params=_cparams("parallel", "arbitrary"),
        name="diff_decode",
    )(page_table.reshape(-1), q, qmask, kn, vn.reshape(nb, n_new * DIFF_H, DIFF_DV), expand, hmask, *lams, dn,
      *([ck] * ppg), *([cv] * ppg))
    return out[:, :n_new]


def _head_rms(z, gain, width):
    outs = [_rms(z[:, a:a + width], gain) for a in range(0, z.shape[-1], width)]
    return jnp.concatenate(outs, axis=1)


def _mixout_kernel(x_ref, og_ref, od_ref, wo_ref, nc_ref, wq_ref, qn_ref, x1_ref, qc_ref):
    tm = x_ref.shape[0]
    sub = math.gcd(tm, ROWS_SUB)
    parts = [slice(r, r + sub) for r in range(0, tm, sub)]
    mix = [jnp.concatenate([og_ref[p, :], od_ref[p, :]], axis=1) for p in parts]
    x1 = [x_ref[p, :] + _dot(m, wo_ref[...]) for p, m in zip(parts, mix)]
    for p, v in zip(parts, x1):
        x1_ref[p, :] = v
    hc = [_rms(v, nc_ref[...]).astype(BF16) for v in x1]
    qc = [_head_rms(_dot(h, wq_ref[...]), qn_ref[...], CA_DH) * (CA_DH ** -0.5) for h in hc]
    for p, v in zip(parts, qc):
        qc_ref[p, :] = v.astype(qc_ref.dtype)


def _mixout(x, og, od, wo, nc, wq, qn, *, tm, q_dtype, layer):
    m, d = x.shape
    row = lambda i: (i, 0)
    return pl.pallas_call(
        _mixout_kernel,
        grid=(m // tm,),
        in_specs=[pl.BlockSpec((tm, d), row), pl.BlockSpec((tm, GDN_W), row), pl.BlockSpec((tm, DIFF_W), row),
                  _weight_spec(wo.shape, layer), _layer_spec(nc, layer), _weight_spec(wq.shape, layer),
                  _layer_spec(qn, layer)],
        out_specs=[pl.BlockSpec((tm, d), row), pl.BlockSpec((tm, CA_W), row)],
        out_shape=[jax.ShapeDtypeStruct((m, d), F32), jax.ShapeDtypeStruct((m, CA_W), q_dtype)],
        compiler_params=_cparams("parallel"),
        name="mixout",
    )(x, og, od, wo, nc, wq, qn)


def _memkv_kernel(m_ref, nm_ref, wk_ref, wv_ref, kn_ref, mk_ref, mv_ref):
    mn = _rms(m_ref[...], nm_ref[...]).astype(BF16)
    mk_ref[...] = _head_rms(_dot(mn, wk_ref[...]), kn_ref[...], CA_DH)
    mv_ref[...] = _dot(mn, wv_ref[...])


def _memkv(mem, nm, wk, wv, kn, *, tm, layer):
    m, d = mem.shape
    row = lambda i: (i, 0)
    return pl.pallas_call(
        _memkv_kernel,
        grid=(m // tm,),
        in_specs=[pl.BlockSpec((tm, d), row), _layer_spec(nm, layer), _weight_spec(wk.shape, layer),
                  _weight_spec(wv.shape, layer), _layer_spec(kn, layer)],
        out_specs=[pl.BlockSpec((tm, CA_W), row)] * 2,
        out_shape=[jax.ShapeDtypeStruct((m, CA_W), F32)] * 2,
        compiler_params=_cparams("parallel"),
        name="memkv",
    )(mem, nm, wk, wv, kn)


def _memory_attend(qs, ks, vs):
    s = [_dot_nt(q, k) for q, k in zip(qs, ks)]
    e = [jnp.exp(si - jnp.max(si, axis=-1, keepdims=True)) for si in s]
    pr = [(ei / jnp.sum(ei, axis=-1, keepdims=True)).astype(BF16) for ei in e]
    return [_dot(p, v) for p, v in zip(pr, vs)]


def _head_cols(h):
    return slice(h * CA_DH, (h + 1) * CA_DH)


def _cross_kernel(q_ref, mk_ref, mv_ref, o_ref, pad_scr, *, n_valid):
    groups = range(q_ref.shape[0])
    pad_scr[...] = jnp.zeros(pad_scr.shape, F32)
    for g in groups:
        pad_scr[g, 0:n_valid, :] = q_ref[g].astype(F32)
    q = [pad_scr[g].astype(BF16) for g in groups]
    units = [(g, h) for g in groups for h in range(CA_H)]
    outs = _memory_attend([q[g][:, _head_cols(h)] for g, h in units],
                          [mk_ref[g, :, h, :].astype(BF16) for g, h in units],
                          [mv_ref[g, :, h, :].astype(BF16) for g, h in units])
    for (g, h), oh in zip(units, outs):
        o_ref[g, :, _head_cols(h)] = oh[0:n_valid].astype(o_ref.dtype)


def _cross(q, mk, mv, *, layer, group):
    nb, seq, _ = q.shape
    mem_spec = pl.BlockSpec((None, group, mk.shape[2], CA_H, CA_DH), lambda b: (layer, b, 0, 0, 0))
    q_spec = pl.BlockSpec((group, seq, CA_W), lambda b: (b, 0, 0))
    return pl.pallas_call(
        functools.partial(_cross_kernel, n_valid=seq),
        grid=(nb // group,),
        in_specs=[q_spec, mem_spec, mem_spec],
        out_specs=q_spec,
        out_shape=jax.ShapeDtypeStruct((nb, seq, CA_W), BF16),
        scratch_shapes=[pltpu.VMEM((group, 8, CA_W), F32)],
        compiler_params=_cparams("parallel"),
        name="cross_attn",
    )(q, mk, mv)


def _ffn_kernel(x_ref, oc_ref, *rest, tm, stride, ncol, attend):
    if attend:
        mk_ref, mv_ref = rest[:2]
        rest = rest[2:]
    wco_ref, nf_ref, wg_ref, wu_ref, cw_ref, wd_ref, hist_ref, y_ref, hout_ref, hbuf, gbuf, acc = rest
    j = pl.program_id(1)
    d_ff = wg_ref.shape[1]
    hrows = (FFN_CONV - 1) * stride
    base = hbuf.shape[0]

    @pl.when(j == 0)
    def _():
        hbuf[base - hrows:base, :] = hist_ref[0]

    if attend:
        q = oc_ref[...]
        heads = range(CA_H)
        oc = jnp.concatenate(_memory_attend([q[:, _head_cols(h)] for h in heads],
                                            [mk_ref[0, :, _head_cols(h)].astype(BF16) for h in heads],
                                            [mv_ref[0, :, _head_cols(h)].astype(BF16) for h in heads]),
                             axis=1).astype(BF16)
    else:
        oc = oc_ref[...]
    x2 = x_ref[...] + _dot(oc, wco_ref[...])
    hf = _rms(x2, nf_ref[...]).astype(BF16)
    acc[...] = x2
    for c0 in range(0, d_ff, ncol):
        cs = slice(c0, c0 + ncol)
        gbuf[base - hrows:base, :] = hbuf[base - hrows:base, cs]
        gbuf[base:base + tm, :] = _dot(hf, wg_ref[:, cs])
        hbuf[base - hrows:base, cs] = gbuf[base + tm - hrows:base + tm, :]
        w = cw_ref[:, cs]
        gt = gbuf[base - hrows:base - hrows + tm, :] * w[0:1]
        for t in range(1, FFN_CONV):
            off = base - hrows + t * stride
            gt = gt + gbuf[off:off + tm, :] * w[t:t + 1]
        act = (_silu(gt) * _dot(hf, wu_ref[:, cs])).astype(BF16)
        acc[...] += _dot(act, wd_ref[cs, :])
    y_ref[...] = acc[...]
    hout_ref[0] = hbuf[base - hrows:base, :]


def _ffn(x, oc, wco, nf, wg, wu, cw, wd, hist, *, ngroups, tm, stride, ncol, layer, mem=None):
    m, d = x.shape
    d_ff = wg.shape[-1]
    nt = m // (ngroups * tm)
    hrows = (FFN_CONV - 1) * stride
    base = -(-hrows // 8) * 8
    row = lambda b, j: (b * nt + j, 0)
    grp = lambda b, j: (b, 0, 0)
    mem = tuple(mem or ())
    return pl.pallas_call(
        functools.partial(_ffn_kernel, tm=tm, stride=stride, ncol=ncol, attend=bool(mem)),
        grid=(ngroups, nt),
        in_specs=[pl.BlockSpec((tm, d), row), pl.BlockSpec((tm, CA_W), row)]
                 + [pl.BlockSpec((1,) + a.shape[1:], grp) for a in mem]
                 + [_weight_spec(wco.shape, layer),
                  _layer_spec(nf, layer), _weight_spec(wg.shape, layer), _weight_spec(wu.shape, layer),
                  _layer_spec(cw, layer), _weight_spec(wd.shape, layer),
                  pl.BlockSpec((1, hrows, d_ff), grp)],
        out_specs=[pl.BlockSpec((tm, d), row), pl.BlockSpec((1, hrows, d_ff), grp)],
        out_shape=[jax.ShapeDtypeStruct((m, d), F32), jax.ShapeDtypeStruct((ngroups, hrows, d_ff), F32)],
        scratch_shapes=[pltpu.VMEM((base, d_ff), F32), pltpu.VMEM((base + tm, ncol), F32),
                        pltpu.VMEM((tm, d), F32)],
        compiler_params=_cparams("parallel", "arbitrary"),
        name="ffn",
    )(x, oc, *mem, wco, nf, wg, wu, cw, wd, hist)


def _rotary_tables(pos):
    half = DIFF_DH // 2
    inv = ROPE_THETA ** (-jnp.arange(half, dtype=F32) / half)
    ang = pos.astype(F32)[:, None] * inv[None, :]
    cos, sin, zero = jnp.cos(ang), jnp.sin(ang), jnp.zeros_like(ang)
    reps = (1, LANE // DIFF_DH)
    return (jnp.tile(jnp.concatenate([cos, cos], axis=1), reps),
            jnp.tile(jnp.concatenate([-sin, zero], axis=1), reps),
            jnp.tile(jnp.concatenate([zero, sin], axis=1), reps))


def kernel(x_prompt, x_sample, mem_prompt, cache_k, cache_v, page_table, state_gdn, state_gdn_conv, cache_mem_k, cache_mem_v, state_ffn_conv, norm_mix, w_in, conv_qkv, a_log, dt_bias, gdn_norm, qnorm_diff, knorm_diff, lam_q1, lam_k1, lam_q2, lam_k2, diff_norm, w_out, norm_cross, norm_mem, w_cq, w_ck, w_cv, qnorm_cross, knorm_cross, w_co, norm_ffn, w_gate, w_up, conv_ffn, w_down):
    depth = w_in.shape[0]
    bp, lp, d = x_prompt.shape
    bs, ls, _ = x_sample.shape
    mlen = mem_prompt.shape[1]
    d_ff = w_gate.shape[2]
    past_len = page_table.shape[1] * PAGE
    mp, ms = bp * lp, bs * ls

    cos_p, sa_p, sb_p = _rotary_tables(jnp.arange(lp, dtype=jnp.int32))
    cos_s, sa_s, sb_s = [jnp.tile(t, (bs, 1)) for t in
                         _rotary_tables(past_len + jnp.arange(ls, dtype=jnp.int32))]
    ii = jnp.arange(DIFF_W)
    gmat = jnp.where(ii[:, None] // DIFF_DH == ii[None, :] // DIFF_DH, 1.0 / DIFF_DH, 0.0).astype(BF16)
    c_qkv, c_gate = 3 * GDN_W, 4 * GDN_W
    c_ba = c_gate + 2 * GDN_H

    xp = x_prompt.reshape(mp, d)
    xs = x_sample.reshape(ms, d)
    memp = mem_prompt.reshape(bp * mlen, d)
    zeros_s0 = jnp.zeros((bp, GDN_H, GDN_D, GDN_D), F32)
    zeros_cb = jnp.zeros((bp, GDN_CONV - 1, 3 * GDN_W), F32)
    zeros_fb = jnp.zeros((bp, FFN_CONV - 1, d_ff), F32)

    wb = {name: w.astype(BF16) for name, w in
          dict(out=w_out, cq=w_cq, ck=w_ck, cv=w_cv, co=w_co, gate=w_gate, up=w_up, down=w_down).items()}

    w_in_t = w_in.transpose(0, 2, 1)
    wb["in"] = jnp.concatenate([w_in_t[:, :c_gate], w_in_t[:, c_ba:], w_in_t[:, c_gate:c_ba],
                                jnp.zeros((depth, LANE - 2 * GDN_H, d), F32)], axis=1).astype(BF16)
    rows = lambda a: a.astype(F32)[:, None, :]
    lane_pad = lambda a: jnp.pad(rows(a), ((0, 0), (0, 0), (GDN_H, LANE - 2 * GDN_H)))
    head_tile = lambda a: jnp.tile(rows(a), (1, 1, 2 * DIFF_H))
    lams = (rows(lam_q1), rows(lam_k1), rows(lam_q2), rows(lam_k2))
    dn = rows(diff_norm)

    p_out, s_out, kv_stack = [], [], ()
    for l in range(depth):
        lam_init = 0.8 - 0.6 * math.exp(-0.3 * l)
        inproj = functools.partial(_inproj, gain=rows(norm_mix), w=wb["in"], qg=head_tile(qnorm_diff),
                                   kg=head_tile(knorm_diff), gmat=gmat, layer=l)
        gdn = functools.partial(_gdn, cw=conv_qkv, alog=lane_pad(a_log), dtb=lane_pad(dt_bias),
                                gn=rows(gdn_norm), layer=l)
        mixout = functools.partial(_mixout, wo=wb["out"], nc=rows(norm_cross), wq=wb["cq"],
                                   qn=rows(qnorm_cross), layer=l)
        ffn = functools.partial(_ffn, wco=wb["co"], nf=rows(norm_ffn), wg=wb["gate"], wu=wb["up"],
                                cw=conv_ffn, wd=wb["down"], layer=l)

        mk_p, mv_p = _memkv(memp, rows(norm_mem), wb["ck"], wb["cv"], rows(knorm_cross),
                            tm=math.gcd(bp * mlen, ROWS_MEM), layer=l)
        qkv, gate, ba, q, kt, k16, v2, vt16 = inproj(xp, cos=cos_p, sa=sa_p, sb=sb_p, q_dtype=BF16, seq=lp,
                                                     tm=math.gcd(lp, ROWS_PROJ_STACKING if kv_stack else ROWS_PROJ),
                                                     prev=kv_stack)
        kv_stack = (kt, v2)
        og, sp = gdn(qkv, gate, ba, s0=zeros_s0, cb=zeros_cb, nb=bp, seq=lp, lt=math.gcd(lp, ROWS_GDN))
        od = _flash(q, k16, vt16, lams, dn, nb=bp, seq=lp, tq=math.gcd(lp, ROWS_FLASH), lam_init=lam_init,
                    layer=l)
        x1, qc = mixout(xp, og, od, tm=math.gcd(lp, ROWS_PROJ), q_dtype=BF16)
        xp, fp = ffn(x1, qc, hist=zeros_fb, ngroups=bp, tm=math.gcd(lp, ROWS_MLP), stride=1, ncol=_mlp_cols(d_ff),
                     mem=(mk_p.reshape(bp, mlen, CA_W), mv_p.reshape(bp, mlen, CA_W)))
        p_out.append((sp, qkv.reshape(bp, lp, c_qkv)[:, lp - (GDN_CONV - 1):],
                      mk_p.reshape(bp, mlen, CA_H, CA_DH), mv_p.reshape(bp, mlen, CA_H, CA_DH), fp))

        qkv, gate, ba, q, k, v = inproj(xs, cos=cos_s, sa=sa_s, sb=sb_s, tm=ms, q_dtype=F32)
        r3 = lambda a: a.reshape(bs, ls, a.shape[-1])
        og, ss = gdn(r3(qkv), r3(gate), r3(ba), s0=state_gdn[l], cb=state_gdn_conv[l], nb=bs, seq=ls,
                     lt=GDN_CHUNK)
        od = _decode(page_table, r3(q), r3(k), r3(v), cache_k, cache_v, lams, dn,
                     layer=l, ppg=math.gcd(page_table.shape[1], PAGES_PER_STEP), lam_init=lam_init)
        x1, qc = mixout(xs, og.reshape(ms, GDN_W), od.reshape(ms, DIFF_W), tm=ms, q_dtype=F32)
        oc = _cross(r3(qc), cache_mem_k, cache_mem_v, layer=l, group=math.gcd(bs, SEQS_PER_STEP))
        tmaj = lambda a: a.reshape(bs, ls, -1).transpose(1, 0, 2).reshape(ms, -1)
        hist = state_ffn_conv[l].transpose(1, 0, 2).reshape(1, (FFN_CONV - 1) * bs, d_ff)
        y_t, fs_t = ffn(tmaj(x1), tmaj(oc), hist=hist, ngroups=1, tm=ms, stride=bs, ncol=_mlp_cols(d_ff))
        xs = y_t.reshape(ls, bs, d).transpose(1, 0, 2).reshape(ms, d)
        fs = fs_t.reshape(FFN_CONV - 1, bs, d_ff).transpose(1, 0, 2)
        conv_s = jnp.concatenate([state_gdn_conv[l], r3(qkv)], axis=1)[:, ls:]
        s_out.append((k.reshape(bs, ls, 2 * DIFF_H, DIFF_DH), v.reshape(bs, ls, DIFF_H, DIFF_DV), ss, conv_s, fs))

    pg, pc, pmk, pmv, pf = [jnp.stack(t, axis=0) for t in zip(*p_out)]
    pk = kv_stack[0].reshape(depth, bp, 2 * DIFF_H, DIFF_DH, lp).transpose(0, 1, 4, 2, 3)
    pv = kv_stack[1].reshape(depth, bp, lp, DIFF_H, DIFF_DV)
    sk, sv, sg, sc, sf = [jnp.stack(t, axis=0) for t in zip(*s_out)]
    return (xp.reshape(bp, lp, d), xs.reshape(bs, ls, d), pk, pv, pg, pc, pmk, pmv, pf, sk, sv, sg, sc, sf)
```
